```python
import jax, jax.numpy as jnp
from jax import lax
import numpy as np

D_MODEL = 2048
BATCH = 2
SEQ = 4096
DEPTH = 1

D_MIX = D_MODEL
RWKV_W = D_MIX // 2
HGRN_W = D_MIX - RWKV_W
RWKV_HEAD = 64
RWKV_HEADS = RWKV_W // RWKV_HEAD
HGRN_EXPAND = 128
HGRN_HEADS = HGRN_W // HGRN_EXPAND
HGRN_DV = HGRN_W // HGRN_HEADS
DECAY_LORA = 96
A_LORA = 96
CHUNK = 64
NORM_EPS = 1e-6
LNX_EPS = 64e-5

N_RWKV_COLS = 4 * RWKV_W + DECAY_LORA + A_LORA
N_HGRN_COLS = 4 * HGRN_W
N_IN = N_RWKV_COLS + N_HGRN_COLS

kernel_name = "hymba_rwkv7_hgrn2_layer"


def rmsnorm(x, g, eps=NORM_EPS):
    xf = x.astype(jnp.float32)
    y = xf * lax.rsqrt(jnp.mean(xf * xf, axis=-1, keepdims=True) + eps)
    return (y * g.astype(jnp.float32)).astype(x.dtype)


def token_shift(p):
    return jnp.pad(p[:, :-1], ((0, 0), (1, 0), (0, 0)))


def rwkv7_mix(pr, mu, w0, w2, a0, a2, k_k, k_a, r_k, lnx_w, lnx_b):
    B, T, _ = pr.shape
    H, N = RWKV_HEADS, RWKV_HEAD
    pm = pr + mu * (token_shift(pr) - pr)
    idx = np.cumsum([RWKV_W, RWKV_W, RWKV_W, RWKV_W, DECAY_LORA])
    r, k, v, gate, wd, ad = jnp.split(pm, idx, axis=-1)
    w_log = -jax.nn.softplus(-(w0 + jnp.tanh(wd) @ w2)) - 0.5
    decay = jnp.exp(-jnp.exp(w_log))
    a = jax.nn.sigmoid(a0 + ad @ a2)
    kk = (k * k_k).reshape(B, T, H, N)
    kk = kk / jnp.maximum(jnp.linalg.norm(kk, axis=-1, keepdims=True), 1e-12)
    k = k * (1.0 + (a - 1.0) * k_a)
    hs = lambda z: z.reshape(B, T, H, N)
    r4, k4, v4, d4, a4 = hs(r), hs(k), hs(v), hs(decay), hs(a)
    avec = -kk
    bvec = kk * a4

    def step(S, inp):
        rt, wt, kt, vt, at, bt = inp
        Sa = jnp.einsum('bhvk,bhk->bhv', S, at)
        S = S * wt[:, :, None, :] + Sa[..., None] * bt[:, :, None, :] + vt[..., None] * kt[:, :, None, :]
        y = jnp.einsum('bhvk,bhk->bhv', S, rt)
        return S, y

    tm = lambda z: jnp.transpose(z, (1, 0, 2, 3))
    S0 = jnp.zeros((B, H, N, N), jnp.float32)
    _, y = lax.scan(step, S0, (tm(r4), tm(d4), tm(k4), tm(v4), tm(avec), tm(bvec)))
    y = jnp.transpose(y, (1, 0, 2, 3))
    mean = jnp.mean(y, axis=-1, keepdims=True)
    var = jnp.mean(jnp.square(y - mean), axis=-1, keepdims=True)
    y = (y - mean) * lax.rsqrt(var + LNX_EPS)
    y = y.reshape(B, T, RWKV_W) * lnx_w + lnx_b
    bonus = jnp.sum(r4 * k4 * r_k, axis=-1, keepdims=True) * v4
    y = y + bonus.reshape(B, T, RWKV_W)
    return y * jax.nn.silu(gate)


def hgrn2_mix(ph, lb, norm_g):
    B, T, _ = ph.shape
    H, DK, DV, C = HGRN_HEADS, HGRN_EXPAND, HGRN_DV, CHUNK
    nC = T // C
    q, f_raw, i, gate = jnp.split(ph, 4, axis=-1)
    f = lb + (1.0 - lb) * jax.nn.sigmoid(f_raw)
    k = 1.0 - f
    logf = jnp.log(f)

    def chunked(z, d):
        return jnp.transpose(z.reshape(B, nC, C, H, d), (1, 0, 3, 2, 4))

    qc, kc, vc = chunked(q, DK), chunked(k, DK), chunked(i, DV)
    Gc = jnp.cumsum(chunked(logf, DK), axis=-2)
    causal = (jnp.arange(C)[:, None] >= jnp.arange(C)[None, :])[None, None, :, :, None]

    def step(S, inp):
        qt, kt, vt, Gt = inp
        G_last = Gt[:, :, -1, :]
        inter = jnp.einsum('bhtk,bhkv->bhtv', qt * jnp.exp(Gt), S)
        Dlt = Gt[:, :, :, None, :] - Gt[:, :, None, :, :]
        E = jnp.where(causal, jnp.exp(jnp.where(causal, Dlt, 0.0)), 0.0)
        A = jnp.einsum('bhtk,bhsk,bhtsk->bhts', qt, kt, E)
        intra = jnp.einsum('bhts,bhsv->bhtv', A, vt)
        kdec = kt * jnp.exp(G_last[:, :, None, :] - Gt)
        S = S * jnp.exp(G_last)[..., None] + jnp.einsum('bhsk,bhsv->bhkv', kdec, vt)
        return S, inter + intra

    S0 = jnp.zeros((B, H, DK, DV), jnp.float32)
    _, o = lax.scan(step, S0, (qc, kc, vc, Gc))
    o = jnp.transpose(o, (1, 0, 3, 2, 4)).reshape(B, T, H, DV)
    o = o * lax.rsqrt(jnp.mean(o * o, axis=-1, keepdims=True) + NORM_EPS)
    o = o.reshape(B, T, HGRN_W) * norm_g
    return o * jax.nn.silu(gate)


def setup_inputs(seed: int = 0) -> dict:
    key = jax.random.key(seed)
    ks = jax.random.split(key, 20)
    f32 = jnp.float32
    nrm = lambda k, s, sc: (jax.random.normal(k, s, f32) * sc)
    return {
        "x": nrm(ks[0], (BATCH, SEQ, D_MODEL), 1.0),
        "norm_g": 1.0 + nrm(ks[1], (DEPTH, D_MODEL), 0.02),
        "w_in": nrm(ks[2], (DEPTH, D_MODEL, N_IN), D_MODEL ** -0.5),
        "mu": jax.random.uniform(ks[3], (DEPTH, N_RWKV_COLS), f32, 0.0, 1.0),
        "w0": jax.random.uniform(ks[4], (DEPTH, RWKV_W), f32, -4.0, 0.0),
        "w2": nrm(ks[5], (DEPTH, DECAY_LORA, RWKV_W), 0.1),
        "a0": nrm(ks[6], (DEPTH, RWKV_W), 0.1),
        "a2": nrm(ks[7], (DEPTH, A_LORA, RWKV_W), 0.1),
        "k_k": 0.85 + nrm(ks[8], (DEPTH, RWKV_W), 0.02),
        "k_a": 1.0 + nrm(ks[9], (DEPTH, RWKV_W), 0.02),
        "r_k": nrm(ks[10], (DEPTH, RWKV_HEADS, RWKV_HEAD), 0.1),
        "lnx_w": 1.0 + nrm(ks[11], (DEPTH, RWKV_W), 0.02),
        "lnx_b": nrm(ks[12], (DEPTH, RWKV_W), 0.01),
        "hgrn_norm_g": 1.0 + nrm(ks[13], (DEPTH, HGRN_W), 0.02),
        "lb_param": nrm(ks[14], (DEPTH + 1, HGRN_W), 0.1),
        "w_out": nrm(ks[15], (DEPTH, D_MIX, D_MODEL), D_MIX ** -0.5),
        "final_g": 1.0 + nrm(ks[16], (D_MODEL,), 0.02),
    }


def reference(x, norm_g, w_in, mu, w0, w2, a0, a2, k_k, k_a, r_k, lnx_w, lnx_b,
              hgrn_norm_g, lb_param, w_out, final_g):
    f32 = jnp.float32
    lbs = jnp.cumsum(jax.nn.softmax(lb_param.astype(f32), axis=0), axis=0)
    h = x
    for l in range(DEPTH):
        hn = rmsnorm(h, norm_g[l])
        p = (hn @ w_in[l]).astype(f32)
        pr, ph = p[..., :N_RWKV_COLS], p[..., N_RWKV_COLS:]
        y_r = rwkv7_mix(pr, mu[l].astype(f32), w0[l].astype(f32), w2[l].astype(f32),
                        a0[l].astype(f32), a2[l].astype(f32), k_k[l].astype(f32),
                        k_a[l].astype(f32), r_k[l].astype(f32), lnx_w[l].astype(f32),
                        lnx_b[l].astype(f32))
        y_h = hgrn2_mix(ph, lbs[l], hgrn_norm_g[l].astype(f32))
        y = jnp.concatenate([y_r, y_h], axis=-1).astype(h.dtype)
        h = h + y @ w_out[l]
    return rmsnorm(h, final_g)
```

```python
import functools
import math

import jax
import jax.numpy as jnp
from jax import lax
from jax.experimental import pallas as pl
from jax.experimental.pallas import tpu as pltpu

F32 = jnp.float32
BF16 = jnp.bfloat16

NORM_EPS = 1e-6
LNX_EPS = 64e-5
RWKV_HEAD = 64
HGRN_HEAD = 128
LORA = 96
LORA_PAD = 128
CHUNK = 64
QUAD = 4
QW = QUAD * RWKV_HEAD
EXP_M05 = math.exp(-0.5)
VMEM_LIMIT = 48 * 1024 * 1024


def _mm(a, b):
    return jnp.dot(a.astype(BF16), b.astype(BF16), preferred_element_type=F32)


def _mm_nt(a, b):
    return lax.dot_general(a.astype(BF16), b.astype(BF16), (((1,), (1,)), ((), ())),
                           preferred_element_type=F32)


def _mm_tn(a, b):
    return lax.dot_general(a.astype(BF16), b.astype(BF16), (((0,), (0,)), ((), ())),
                           preferred_element_type=F32)


def _split3(x):
    hi = x.astype(BF16)
    r1 = x - hi.astype(F32)
    mid = r1.astype(BF16)
    lo = (r1 - mid.astype(F32)).astype(BF16)
    return hi, mid, lo


def _cumsum_rows(x, tril_bf16):
    hi, mid, lo = _split3(x)
    dot = functools.partial(jnp.dot, preferred_element_type=F32)
    return dot(tril_bf16, hi) + dot(tril_bf16, mid) + dot(tril_bf16, lo)


def _sigmoid(x):
    return 1.0 / (1.0 + jnp.exp(-x))


def _inproj_kernel(x_ref, g_ref, w_ref, o_ref, hn_ref):
    @pl.when(pl.program_id(1) == 0)
    def _():
        x = x_ref[...]
        ms = jnp.mean(x * x, axis=-1, keepdims=True)
        hn_ref[...] = (x * lax.rsqrt(ms + NORM_EPS) * g_ref[...]).astype(BF16)

    o_ref[...] = jnp.dot(hn_ref[...], w_ref[...], preferred_element_type=F32)


def _inproj(x2, g, w, tm, tn):
    m, d = x2.shape
    n = w.shape[1]
    return pl.pallas_call(
        _inproj_kernel,
        grid=(m // tm, n // tn),
        in_specs=[
            pl.BlockSpec((tm, d), lambda i, j: (i, 0)),
            pl.BlockSpec((1, d), lambda i, j: (0, 0)),
            pl.BlockSpec((d, tn), lambda i, j: (0, j)),
        ],
        out_specs=pl.BlockSpec((tm, tn), lambda i, j: (i, j)),
        out_shape=jax.ShapeDtypeStruct((m, n), F32),
        scratch_shapes=[pltpu.VMEM((tm, d), BF16)],
        compiler_params=pltpu.CompilerParams(
            dimension_semantics=("arbitrary", "arbitrary"), vmem_limit_bytes=VMEM_LIMIT),
        name="inproj",
    )(x2, g, w)


def _rwkv_kernel(pm_ref, pl_ref, mum_ref, mul_ref, w0_ref, w2_ref, a0_ref, a2_ref,
                 kk_ref, ka_ref, rk_ref, lw_ref, lb_ref, ones_ref,
                 o_ref, carry_m, carry_l, state):
    c = pl.program_id(1)
    C = CHUNK
    W = w0_ref.shape[1]

    @pl.when(c == 0)
    def _():
        carry_m[...] = jnp.zeros_like(carry_m)
        carry_l[...] = jnp.zeros_like(carry_l)
        state[...] = jnp.zeros_like(state)

    row = lax.broadcasted_iota(jnp.int32, (C, 1), 0)

    def shift_mix(p, carry_ref, mu):
        prev = jnp.where(row == 0, carry_ref[...], pltpu.roll(p, 1, axis=0))
        carry_ref[...] = p[C - 1:C, :]
        return p + mu * (prev - p)

    xm = shift_mix(pm_ref[0], carry_m, mum_ref[...])
    xl = shift_mix(pl_ref[0], carry_l, mul_ref[...])
    r = xm[:, 0 * W:1 * W]
    k = xm[:, 1 * W:2 * W]
    v = xm[:, 2 * W:3 * W]
    gate = xm[:, 3 * W:4 * W]
    wd = xl[:, :LORA_PAD]
    ad = xl[:, LORA_PAD:]

    z = w0_ref[...] + _mm(jnp.tanh(wd), w2_ref[...])
    ld = -EXP_M05 * _sigmoid(z)
    a = _sigmoid(a0_ref[...] + _mm(ad, a2_ref[...]))

    t_i = lax.broadcasted_iota(jnp.int32, (C, C), 0)
    s_i = lax.broadcasted_iota(jnp.int32, (C, C), 1)
    tril = (t_i >= s_i).astype(BF16)
    G = _cumsum_rows(ld, tril)
    Gx = G - ld
    GC = G[C - 1:C, :]

    ones_bd = ones_ref[...]

    def segsum(x):
        hi = x.astype(BF16)
        lo = (x - hi.astype(F32)).astype(BF16)
        dot = functools.partial(jnp.dot, preferred_element_type=F32)
        return dot(hi, ones_bd) + dot(lo, ones_bd)

    lane = lax.broadcasted_iota(jnp.int32, (C, QW), 1)
    lane_blk = lane // RWKV_HEAD
    lane_s = lane % RWKV_HEAD
    trow = lax.broadcasted_iota(jnp.int32, (C, QW), 0)
    strict = lane_s < trow
    incl = lane_s <= trow
    eye_q = jnp.where(lane_s == trow, 1.0, 0.0).astype(F32)
    bd_mask = (lax.broadcasted_iota(jnp.int32, (QW, QW), 0) // RWKV_HEAD
               == lax.broadcasted_iota(jnp.int32, (QW, QW), 1) // RWKV_HEAD)

    def bd(x):
        xb = x.astype(BF16)
        zero = jnp.zeros_like(xb)
        return jnp.concatenate(
            [jnp.where(lane_blk == h, xb, zero) for h in range(QUAD)], axis=0)

    for q in range(W // QW):
        sl = slice(q * QW, (q + 1) * QW)
        rq, kq, vq, gq, aq = r[:, sl], k[:, sl], v[:, sl], gate[:, sl], a[:, sl]
        Gq, Gxq, GCq = G[:, sl], Gx[:, sl], GC[:, sl]

        kkq = kq * kk_ref[:, sl]
        n2 = segsum(kkq * kkq)
        kkq = kkq / jnp.maximum(jnp.sqrt(n2), 1e-12)
        k2 = kq * (1.0 + (aq - 1.0) * ka_ref[:, sl])
        av = -kkq
        bv = kkq * aq

        einv = jnp.exp(-Gq)
        Rt = rq * jnp.exp(Gq)
        At = av * jnp.exp(Gxq)
        Bt = bv * einv
        Kt = k2 * einv
        eC = jnp.exp(GCq - Gq)
        Bte = bv * eC
        Kte = k2 * eC
        PC = jnp.exp(GCq)

        lhs = jnp.concatenate([At, Rt], axis=0)
        SB = _mm_nt(lhs, bd(Bt))
        SK = _mm_nt(lhs, bd(Kt))
        N = jnp.where(strict, SB[:C], 0.0)
        Aak = jnp.where(strict, SK[:C], 0.0)
        Lrb = jnp.where(incl, SB[C:], 0.0)
        Lrk = jnp.where(incl, SK[C:], 0.0)

        P = eye_q + N
        Np = _mm(N, bd(N))
        for _ in range(4):
            out = _mm(jnp.concatenate([P, Np], axis=0), bd(Np))
            P = P + out[:C]
            Np = out[C:]
        P = P + _mm(P, bd(Np))

        S = state[q]
        Sb = S.astype(BF16)
        BDV = bd(vq)
        Z0 = _mm_nt(At, Sb) + _mm(Aak, BDV)
        U = _mm(P, bd(Z0))
        Y = _mm_nt(Rt, Sb) + _mm(Lrb, bd(U)) + _mm(Lrk, BDV)
        upd = _mm_tn(jnp.concatenate([U, vq], axis=0), jnp.concatenate([Bte, Kte], axis=0))
        state[q] = S * PC + jnp.where(bd_mask, upd, 0.0)

        mean = segsum(Y) * (1.0 / RWKV_HEAD)
        yc = Y - mean
        var = segsum(yc * yc) * (1.0 / RWKV_HEAD)
        yn = yc * lax.rsqrt(var + LNX_EPS) * lw_ref[:, sl] + lb_ref[:, sl]
        bonus = segsum(rq * k2 * rk_ref[:, sl]) * vq
        res = (yn + bonus) * (gq * _sigmoid(gq))
        o_ref[0, :, sl] = res.astype(o_ref.dtype)


def _rwkv(p3, main_blk, lora_blk, mu_m, mu_l, w0, w2p, a0, a2p, k_k, k_a, r_k, lnx_w, lnx_b, ones_bd):
    B, T, _ = p3.shape
    W = w0.shape[1]
    C = CHUNK
    full = lambda shape: pl.BlockSpec(shape, lambda b, c: (0,) * len(shape))
    return pl.pallas_call(
        _rwkv_kernel,
        grid=(B, T // C),
        in_specs=[
            pl.BlockSpec((1, C, 4 * W), lambda b, c: (b, c, main_blk)),
            pl.BlockSpec((1, C, 2 * LORA_PAD), lambda b, c: (b, c, lora_blk)),
            full((1, 4 * W)), full((1, 2 * LORA_PAD)),
            full((1, W)), full((LORA_PAD, W)), full((1, W)), full((LORA_PAD, W)),
            full((1, W)), full((1, W)), full((1, W)), full((1, W)), full((1, W)),
            full((QW, QW)),
        ],
        out_specs=pl.BlockSpec((1, C, W), lambda b, c: (b, c, 0)),
        out_shape=jax.ShapeDtypeStruct((B, T, W), BF16),
        scratch_shapes=[
            pltpu.VMEM((1, 4 * W), F32),
            pltpu.VMEM((1, 2 * LORA_PAD), F32),
            pltpu.VMEM((W // QW, QW, QW), F32),
        ],
        compiler_params=pltpu.CompilerParams(
            dimension_semantics=("arbitrary", "arbitrary"), vmem_limit_bytes=VMEM_LIMIT),
        name="rwkv7_mix",
    )(p3, p3, mu_m, mu_l, w0, w2p, a0, a2p, k_k, k_a, r_k, lnx_w, lnx_b, ones_bd)


def _hgrn_kernel(layer, ph_ref, lbp_ref, ng_ref, o_ref, state):
    c = pl.program_id(1)
    C = CHUNK
    W = ng_ref.shape[1]
    D = HGRN_HEAD

    @pl.when(c == 0)
    def _():
        state[...] = jnp.zeros_like(state)

    lp = lbp_ref[...]
    e = jnp.exp(lp - jnp.max(lp, axis=0, keepdims=True))
    lb = jnp.sum(e[:layer + 1], axis=0, keepdims=True) / jnp.sum(e, axis=0, keepdims=True)

    x = ph_ref[0]
    qv = x[:, 0 * W:1 * W]
    fr = x[:, 1 * W:2 * W]
    iv = x[:, 2 * W:3 * W]
    gate = x[:, 3 * W:4 * W]

    f = lb + (1.0 - lb) * _sigmoid(fr)
    kx = 1.0 - f
    lf = jnp.log(f)

    t_i = lax.broadcasted_iota(jnp.int32, (C, C), 0)
    s_i = lax.broadcasted_iota(jnp.int32, (C, C), 1)
    causal = t_i >= s_i
    G = _cumsum_rows(lf, causal.astype(BF16))
    GC = G[C - 1:C, :]
    qt = qv * jnp.exp(G)
    kb = kx * jnp.exp(-G)
    kd = kx * jnp.exp(GC - G)
    PC = jnp.exp(GC)

    for h in range(W // D):
        sl = slice(h * D, (h + 1) * D)
        A = jnp.where(causal, _mm_nt(qt[:, sl], kb[:, sl]), 0.0)
        S = state[h]
        o = _mm(A, iv[:, sl]) + _mm_nt(qt[:, sl], S)
        state[h] = S * PC[:, sl] + _mm_tn(iv[:, sl], kd[:, sl])
        ms = jnp.mean(o * o, axis=-1, keepdims=True)
        g = gate[:, sl]
        res = o * lax.rsqrt(ms + NORM_EPS) * ng_ref[:, sl] * (g * _sigmoid(g))
        o_ref[0, :, sl] = res.astype(o_ref.dtype)


def _hgrn(p3, blk, lb_param, norm_g, layer):
    B, T, _ = p3.shape
    W = norm_g.shape[1]
    C = CHUNK
    L = lb_param.shape[0]
    return pl.pallas_call(
        functools.partial(_hgrn_kernel, layer),
        grid=(B, T // C),
        in_specs=[
            pl.BlockSpec((1, C, 4 * W), lambda b, c: (b, c, blk)),
            pl.BlockSpec((L, W), lambda b, c: (0, 0)),
            pl.BlockSpec((1, W), lambda b, c: (0, 0)),
        ],
        out_specs=pl.BlockSpec((1, C, W), lambda b, c: (b, c, 0)),
        out_shape=jax.ShapeDtypeStruct((B, T, W), BF16),
        scratch_shapes=[pltpu.VMEM((W // HGRN_HEAD, HGRN_HEAD, HGRN_HEAD), F32)],
        compiler_params=pltpu.CompilerParams(
            dimension_semantics=("arbitrary", "arbitrary"), vmem_limit_bytes=VMEM_LIMIT),
        name="hgrn2_mix",
    )(p3, lb_param, norm_g)


def _outproj_kernel(final, yr_ref, yh_ref, x_ref, wr_ref, wh_ref, g_ref, o_ref):
    acc = jnp.dot(yr_ref[...], wr_ref[...], preferred_element_type=F32)
    acc += jnp.dot(yh_ref[...], wh_ref[...], preferred_element_type=F32)
    h = x_ref[...] + acc
    if final:
        ms = jnp.mean(h * h, axis=-1, keepdims=True)
        h = h * lax.rsqrt(ms + NORM_EPS) * g_ref[...]
    o_ref[...] = h


def _outproj(yr, yh, x2, wr, wh, g, tm, final):
    m, d = x2.shape
    wr_w = yr.shape[1]
    wh_w = yh.shape[1]
    return pl.pallas_call(
        functools.partial(_outproj_kernel, final),
        grid=(m // tm,),
        in_specs=[
            pl.BlockSpec((tm, wr_w), lambda i: (i, 0)),
            pl.BlockSpec((tm, wh_w), lambda i: (i, 0)),
            pl.BlockSpec((tm, d), lambda i: (i, 0)),
            pl.BlockSpec((wr_w, d), lambda i: (0, 0)),
            pl.BlockSpec((wh_w, d), lambda i: (0, 0)),
            pl.BlockSpec((1, d), lambda i: (0, 0)),
        ],
        out_specs=pl.BlockSpec((tm, d), lambda i: (i, 0)),
        out_shape=jax.ShapeDtypeStruct((m, d), F32),
        compiler_params=pltpu.CompilerParams(
            dimension_semantics=("arbitrary",), vmem_limit_bytes=VMEM_LIMIT),
        name="outproj",
    )(yr, yh, x2, wr, wh, g)


def kernel(x, norm_g, w_in, mu, w0, w2, a0, a2, k_k, k_a, r_k, lnx_w, lnx_b,
           hgrn_norm_g, lb_param, w_out, final_g):
    B, T, D = x.shape
    depth = w_in.shape[0]
    RW = w0.shape[1]
    HW = hgrn_norm_g.shape[1]
    assert RW == HW and 4 * RW % (2 * LORA_PAD) == 0 and w2.shape[1] == LORA
    n_r = 4 * RW + 2 * LORA

    row = lambda z: z.reshape(1, -1).astype(F32)
    zc = lambda rows, n: jnp.zeros((rows, n), F32)
    pad = LORA_PAD - LORA
    ones_bd = (jnp.arange(QW)[:, None] // RWKV_HEAD == jnp.arange(QW)[None, :] // RWKV_HEAD).astype(BF16)

    h = x.reshape(B * T, D)
    for l in range(depth):
        w = w_in[l]
        wcat = jnp.concatenate(
            [w[:, n_r:], w[:, :4 * RW],
             w[:, 4 * RW:4 * RW + LORA], zc(D, pad),
             w[:, 4 * RW + LORA:n_r], zc(D, pad)], axis=1).astype(BF16)
        mu_l = mu[l].astype(F32)
        mu_main = mu_l[:4 * RW].reshape(1, -1)
        mu_lora = jnp.concatenate(
            [mu_l[4 * RW:4 * RW + LORA], jnp.zeros((pad,), F32),
             mu_l[4 * RW + LORA:], jnp.zeros((pad,), F32)]).reshape(1, -1)
        w2p = jnp.concatenate([w2[l].astype(F32), zc(pad, RW)], axis=0).astype(BF16)
        a2p = jnp.concatenate([a2[l].astype(F32), zc(pad, RW)], axis=0).astype(BF16)

        p = _inproj(h, row(norm_g[l]), wcat, tm=1024, tn=768)
        p3 = p.reshape(B, T, -1)
        y_r = _rwkv(p3, 1, (4 * HW + 4 * RW) // (2 * LORA_PAD), mu_main, mu_lora,
                    row(w0[l]), w2p, row(a0[l]), a2p, row(k_k[l]), row(k_a[l]), row(r_k[l]),
                    row(lnx_w[l]), row(lnx_b[l]), ones_bd)
        y_h = _hgrn(p3, 0, lb_param.astype(F32), row(hgrn_norm_g[l]), l)
        wo = w_out[l].astype(BF16)
        h = _outproj(y_r.reshape(B * T, RW), y_h.reshape(B * T, HW), h,
                     wo[:RW], wo[RW:], row(final_g), tm=256, final=(l == depth - 1))
    return h.reshape(B, T, D)
```

```python
import functools
import math

import jax
import jax.numpy as jnp
from jax import lax
from jax.experimental import pallas as pl
from jax.experimental.pallas import tpu as pltpu

F32 = jnp.float32
BF16 = jnp.bfloat16

NORM_EPS = 1e-6
LNX_EPS = 64e-5
RWKV_HEAD = 64
HGRN_HEAD = 128
LORA = 96
LORA_PAD = 128
CHUNK = 64
QUAD = 4
QW = QUAD * RWKV_HEAD
EXP_M05 = math.exp(-0.5)
VMEM_LIMIT = 48 * 1024 * 1024


def _mm(a, b):
    return jnp.dot(a.astype(BF16), b.astype(BF16), preferred_element_type=F32)


def _mm_nt(a, b):
    return lax.dot_general(a.astype(BF16), b.astype(BF16), (((1,), (1,)), ((), ())),
                           preferred_element_type=F32)


def _mm_tn(a, b):
    return lax.dot_general(a.astype(BF16), b.astype(BF16), (((0,), (0,)), ((), ())),
                           preferred_element_type=F32)


def _split3(x):
    hi = x.astype(BF16)
    r1 = x - hi.astype(F32)
    mid = r1.astype(BF16)
    lo = (r1 - mid.astype(F32)).astype(BF16)
    return hi, mid, lo


def _cumsum_rows(x, tril_bf16):
    hi, mid, lo = _split3(x)
    dot = functools.partial(jnp.dot, preferred_element_type=F32)
    return dot(tril_bf16, hi) + dot(tril_bf16, mid) + dot(tril_bf16, lo)


def _sigmoid(x):
    return 1.0 / (1.0 + jnp.exp(-x))


def _inproj_kernel(x_ref, g_ref, w_ref, o_ref, hn_ref):
    @pl.when(pl.program_id(1) == 0)
    def _():
        x = x_ref[...]
        ms = jnp.mean(x * x, axis=-1, keepdims=True)
        hn_ref[...] = (x * lax.rsqrt(ms + NORM_EPS) * g_ref[...]).astype(BF16)

    o_ref[...] = jnp.dot(hn_ref[...], w_ref[...], preferred_element_type=F32)


def _inproj(x2, g, w, tm, tn):
    m, d = x2.shape
    n = w.shape[1]
    return pl.pallas_call(
        _inproj_kernel,
        grid=(m // tm, n // tn),
        in_specs=[
            pl.BlockSpec((tm, d), lambda i, j: (i, 0)),
            pl.BlockSpec((1, d), lambda i, j: (0, 0)),
            pl.BlockSpec((d, tn), lambda i, j: (0, j)),
        ],
        out_specs=pl.BlockSpec((tm, tn), lambda i, j: (i, j)),
        out_shape=jax.ShapeDtypeStruct((m, n), F32),
        scratch_shapes=[pltpu.VMEM((tm, d), BF16)],
        compiler_params=pltpu.CompilerParams(
            dimension_semantics=("arbitrary", "arbitrary"), vmem_limit_bytes=VMEM_LIMIT),
        name="inproj",
    )(x2, g, w)


def _rwkv_kernel(pm_ref, pl_ref, mum_ref, mul_ref, w0_ref, w2_ref, a0_ref, a2_ref,
                 kk_ref, ka_ref, rk_ref, lw_ref, lb_ref, ones_ref,
                 o_ref, carry_m, carry_l, state):
    c = pl.program_id(0)
    C = CHUNK
    NB = pm_ref.shape[0]
    W = w0_ref.shape[1]
    NQ = W // QW

    @pl.when(c == 0)
    def _():
        carry_m[...] = jnp.zeros_like(carry_m)
        carry_l[...] = jnp.zeros_like(carry_l)
        state[...] = jnp.zeros_like(state)

    row = lax.broadcasted_iota(jnp.int32, (C, 1), 0)
    t_i = lax.broadcasted_iota(jnp.int32, (C, C), 0)
    s_i = lax.broadcasted_iota(jnp.int32, (C, C), 1)
    tril = (t_i >= s_i).astype(BF16)
    ones_bd = ones_ref[...]
    dot = functools.partial(jnp.dot, preferred_element_type=F32)

    lane = lax.broadcasted_iota(jnp.int32, (C, QW), 1)
    lane_blk = lane // RWKV_HEAD
    lane_s = lane % RWKV_HEAD
    trow = lax.broadcasted_iota(jnp.int32, (C, QW), 0)
    strict = lane_s < trow
    incl = lane_s <= trow
    eye_q = jnp.where(lane_s == trow, 1.0, 0.0).astype(F32)
    bd_mask = (lax.broadcasted_iota(jnp.int32, (QW, QW), 0) // RWKV_HEAD
               == lax.broadcasted_iota(jnp.int32, (QW, QW), 1) // RWKV_HEAD)

    def bd(x):
        xb = x.astype(BF16)
        zero = jnp.zeros_like(xb)
        return jnp.concatenate(
            [jnp.where(lane_blk == h, xb, zero) for h in range(QUAD)], axis=0)

    def segsum(x):
        hi = x.astype(BF16)
        lo = (x - hi.astype(F32)).astype(BF16)
        return dot(hi, ones_bd) + dot(lo, ones_bd)

    def shift_mix(p, carry_ref, b, mu):
        prev = jnp.where(row == 0, carry_ref[b:b + 1, :], pltpu.roll(p, 1, axis=0))
        carry_ref[b:b + 1, :] = p[C - 1:C, :]
        return p + mu * (prev - p)

    units = []
    for b in range(NB):
        xm = shift_mix(pm_ref[b], carry_m, b, mum_ref[...])
        xl = shift_mix(pl_ref[b], carry_l, b, mul_ref[...])
        r = xm[:, 0 * W:1 * W]
        k = xm[:, 1 * W:2 * W]
        v = xm[:, 2 * W:3 * W]
        gate = xm[:, 3 * W:4 * W]
        z = w0_ref[...] + _mm(jnp.tanh(xl[:, :LORA_PAD]), w2_ref[...])
        ld = -EXP_M05 * _sigmoid(z)
        a = _sigmoid(a0_ref[...] + _mm(xl[:, LORA_PAD:], a2_ref[...]))
        G = _cumsum_rows(ld, tril)
        Gx = G - ld
        GC = G[C - 1:C, :]
        for q in range(NQ):
            sl = slice(q * QW, (q + 1) * QW)
            units.append(dict(b=b, q=q, sl=sl, r=r[:, sl], k=k[:, sl], v=v[:, sl], g=gate[:, sl],
                              a=a[:, sl], G=G[:, sl], Gx=Gx[:, sl], GC=GC[:, sl]))

    for u in units:
        u["kk"] = u["k"] * kk_ref[:, u["sl"]]
    for u in units:
        u["n2"] = segsum(u["kk"] * u["kk"])
    for u in units:
        sl = u["sl"]
        kkn = u["kk"] / jnp.maximum(jnp.sqrt(u["n2"]), 1e-12)
        k2 = u["k"] * (1.0 + (u["a"] - 1.0) * ka_ref[:, sl])
        av = -kkn
        bv = kkn * u["a"]
        einv = jnp.exp(-u["G"])
        eC = jnp.exp(u["GC"] - u["G"])
        u["k2"] = k2
        u["Rt"] = u["r"] * jnp.exp(u["G"])
        u["At"] = av * jnp.exp(u["Gx"])
        u["BDB"] = bd(bv * einv)
        u["BDK"] = bd(k2 * einv)
        u["BKe"] = jnp.concatenate([bv * eC, k2 * eC], axis=0).astype(BF16)
        u["PC"] = jnp.exp(u["GC"])
        u["BDV"] = bd(u["v"])
        u["lhs"] = jnp.concatenate([u["At"], u["Rt"]], axis=0).astype(BF16)

    for u in units:
        u["SB"] = _mm_nt(u["lhs"], u["BDB"])
    for u in units:
        u["SK"] = _mm_nt(u["lhs"], u["BDK"])
    for u in units:
        u["N"] = jnp.where(strict, u["SB"][:C], 0.0)
        u["Aak"] = jnp.where(strict, u["SK"][:C], 0.0)
        u["Lrb"] = jnp.where(incl, u["SB"][C:], 0.0)
        u["Lrk"] = jnp.where(incl, u["SK"][C:], 0.0)

    for u in units:
        u["P"] = eye_q + u["N"]
        u["Np"] = _mm(u["N"], bd(u["N"]))
    for _ in range(4):
        for u in units:
            out = _mm(jnp.concatenate([u["P"], u["Np"]], axis=0), bd(u["Np"]))
            u["P"] = u["P"] + out[:C]
            u["Np"] = out[C:]
    for u in units:
        u["P"] = u["P"] + _mm(u["P"], bd(u["Np"]))

    for u in units:
        u["S"] = state[u["b"] * NQ + u["q"]]
        u["Sb"] = u["S"].astype(BF16)
        u["Z0"] = _mm_nt(u["At"], u["Sb"]) + _mm(u["Aak"], u["BDV"])
    for u in units:
        u["U"] = _mm(u["P"], bd(u["Z0"]))
    for u in units:
        u["Y"] = (_mm_nt(u["Rt"], u["Sb"]) + _mm(u["Lrb"], bd(u["U"]))
                  + _mm(u["Lrk"], u["BDV"]))
    for u in units:
        upd = _mm_tn(jnp.concatenate([u["U"], u["v"]], axis=0), u["BKe"])
        state[u["b"] * NQ + u["q"]] = u["S"] * u["PC"] + jnp.where(bd_mask, upd, 0.0)

    for u in units:
        u["mean"] = segsum(u["Y"]) * (1.0 / RWKV_HEAD)
        u["bonus"] = segsum(u["r"] * u["k2"] * rk_ref[:, u["sl"]])
    for u in units:
        u["yc"] = u["Y"] - u["mean"]
        u["var"] = segsum(u["yc"] * u["yc"]) * (1.0 / RWKV_HEAD)
    for u in units:
        sl = u["sl"]
        yn = u["yc"] * lax.rsqrt(u["var"] + LNX_EPS) * lw_ref[:, sl] + lb_ref[:, sl]
        res = (yn + u["bonus"] * u["v"]) * (u["g"] * _sigmoid(u["g"]))
        o_ref[u["b"], :, sl] = res.astype(o_ref.dtype)


def _rwkv(p3, main_blk, lora_blk, mu_m, mu_l, w0, w2p, a0, a2p, k_k, k_a, r_k, lnx_w, lnx_b, ones_bd):
    B, T, _ = p3.shape
    W = w0.shape[1]
    C = CHUNK
    full = lambda shape: pl.BlockSpec(shape, lambda c: (0,) * len(shape))
    return pl.pallas_call(
        _rwkv_kernel,
        grid=(T // C,),
        in_specs=[
            pl.BlockSpec((B, C, 4 * W), lambda c: (0, c, main_blk)),
            pl.BlockSpec((B, C, 2 * LORA_PAD), lambda c: (0, c, lora_blk)),
            full((1, 4 * W)), full((1, 2 * LORA_PAD)),
            full((1, W)), full((LORA_PAD, W)), full((1, W)), full((LORA_PAD, W)),
            full((1, W)), full((1, W)), full((1, W)), full((1, W)), full((1, W)),
            full((QW, QW)),
        ],
        out_specs=pl.BlockSpec((B, C, W), lambda c: (0, c, 0)),
        out_shape=jax.ShapeDtypeStruct((B, T, W), BF16),
        scratch_shapes=[
            pltpu.VMEM((B, 4 * W), F32),
            pltpu.VMEM((B, 2 * LORA_PAD), F32),
            pltpu.VMEM((B * (W // QW), QW, QW), F32),
        ],
        compiler_params=pltpu.CompilerParams(
            dimension_semantics=("arbitrary",), vmem_limit_bytes=VMEM_LIMIT),
        name="rwkv7_mix",
    )(p3, p3, mu_m, mu_l, w0, w2p, a0, a2p, k_k, k_a, r_k, lnx_w, lnx_b, ones_bd)


def _hgrn_kernel(layer, ph_ref, lbp_ref, ng_ref, o_ref, state):
    c = pl.program_id(1)
    C = CHUNK
    W = ng_ref.shape[1]
    D = HGRN_HEAD

    @pl.when(c == 0)
    def _():
        state[...] = jnp.zeros_like(state)

    lp = lbp_ref[...]
    e = jnp.exp(lp - jnp.max(lp, axis=0, keepdims=True))
    lb = jnp.sum(e[:layer + 1], axis=0, keepdims=True) / jnp.sum(e, axis=0, keepdims=True)

    x = ph_ref[0]
    qv = x[:, 0 * W:1 * W]
    fr = x[:, 1 * W:2 * W]
    iv = x[:, 2 * W:3 * W]
    gate = x[:, 3 * W:4 * W]

    f = lb + (1.0 - lb) * _sigmoid(fr)
    kx = 1.0 - f
    lf = jnp.log(f)

    t_i = lax.broadcasted_iota(jnp.int32, (C, C), 0)
    s_i = lax.broadcasted_iota(jnp.int32, (C, C), 1)
    causal = t_i >= s_i
    G = _cumsum_rows(lf, causal.astype(BF16))
    GC = G[C - 1:C, :]
    qt = qv * jnp.exp(G)
    kb = kx * jnp.exp(-G)
    kd = kx * jnp.exp(GC - G)
    PC = jnp.exp(GC)

    for h in range(W // D):
        sl = slice(h * D, (h + 1) * D)
        A = jnp.where(causal, _mm_nt(qt[:, sl], kb[:, sl]), 0.0)
        S = state[h]
        o = _mm(A, iv[:, sl]) + _mm_nt(qt[:, sl], S)
        state[h] = S * PC[:, sl] + _mm_tn(iv[:, sl], kd[:, sl])
        ms = jnp.mean(o * o, axis=-1, keepdims=True)
        g = gate[:, sl]
        res = o * lax.rsqrt(ms + NORM_EPS) * ng_ref[:, sl] * (g * _sigmoid(g))
        o_ref[0, :, sl] = res.astype(o_ref.dtype)


def _hgrn(p3, blk, lb_param, norm_g, layer):
    B, T, _ = p3.shape
    W = norm_g.shape[1]
    C = CHUNK
    L = lb_param.shape[0]
    return pl.pallas_call(
        functools.partial(_hgrn_kernel, layer),
        grid=(B, T // C),
        in_specs=[
            pl.BlockSpec((1, C, 4 * W), lambda b, c: (b, c, blk)),
            pl.BlockSpec((L, W), lambda b, c: (0, 0)),
            pl.BlockSpec((1, W), lambda b, c: (0, 0)),
        ],
        out_specs=pl.BlockSpec((1, C, W), lambda b, c: (b, c, 0)),
        out_shape=jax.ShapeDtypeStruct((B, T, W), BF16),
        scratch_shapes=[pltpu.VMEM((W // HGRN_HEAD, HGRN_HEAD, HGRN_HEAD), F32)],
        compiler_params=pltpu.CompilerParams(
            dimension_semantics=("arbitrary", "arbitrary"), vmem_limit_bytes=VMEM_LIMIT),
        name="hgrn2_mix",
    )(p3, lb_param, norm_g)


def _outproj_kernel(final, yr_ref, yh_ref, x_ref, wr_ref, wh_ref, g_ref, o_ref):
    acc = jnp.dot(yr_ref[...], wr_ref[...], preferred_element_type=F32)
    acc += jnp.dot(yh_ref[...], wh_ref[...], preferred_element_type=F32)
    h = x_ref[...] + acc
    if final:
        ms = jnp.mean(h * h, axis=-1, keepdims=True)
        h = h * lax.rsqrt(ms + NORM_EPS) * g_ref[...]
    o_ref[...] = h


def _outproj(yr, yh, x2, wr, wh, g, tm, final):
    m, d = x2.shape
    wr_w = yr.shape[1]
    wh_w = yh.shape[1]
    return pl.pallas_call(
        functools.partial(_outproj_kernel, final),
        grid=(m // tm,),
        in_specs=[
            pl.BlockSpec((tm, wr_w), lambda i: (i, 0)),
            pl.BlockSpec((tm, wh_w), lambda i: (i, 0)),
            pl.BlockSpec((tm, d), lambda i: (i, 0)),
            pl.BlockSpec((wr_w, d), lambda i: (0, 0)),
            pl.BlockSpec((wh_w, d), lambda i: (0, 0)),
            pl.BlockSpec((1, d), lambda i: (0, 0)),
        ],
        out_specs=pl.BlockSpec((tm, d), lambda i: (i, 0)),
        out_shape=jax.ShapeDtypeStruct((m, d), F32),
        compiler_params=pltpu.CompilerParams(
            dimension_semantics=("arbitrary",), vmem_limit_bytes=VMEM_LIMIT),
        name="outproj",
    )(yr, yh, x2, wr, wh, g)


def kernel(x, norm_g, w_in, mu, w0, w2, a0, a2, k_k, k_a, r_k, lnx_w, lnx_b,
           hgrn_norm_g, lb_param, w_out, final_g):
    B, T, D = x.shape
    depth = w_in.shape[0]
    RW = w0.shape[1]
    HW = hgrn_norm_g.shape[1]
    assert RW == HW and 4 * RW % (2 * LORA_PAD) == 0 and w2.shape[1] == LORA
    n_r = 4 * RW + 2 * LORA

    row = lambda z: z.reshape(1, -1).astype(F32)
    zc = lambda rows, n: jnp.zeros((rows, n), F32)
    pad = LORA_PAD - LORA
    ones_bd = (jnp.arange(QW)[:, None] // RWKV_HEAD == jnp.arange(QW)[None, :] // RWKV_HEAD).astype(BF16)

    h = x.reshape(B * T, D)
    for l in range(depth):
        w = w_in[l]
        wcat = jnp.concatenate(
            [w[:, n_r:], w[:, :4 * RW],
             w[:, 4 * RW:4 * RW + LORA], zc(D, pad),
             w[:, 4 * RW + LORA:n_r], zc(D, pad)], axis=1).astype(BF16)
        mu_l = mu[l].astype(F32)
        mu_main = mu_l[:4 * RW].reshape(1, -1)
        mu_lora = jnp.concatenate(
            [mu_l[4 * RW:4 * RW + LORA], jnp.zeros((pad,), F32),
             mu_l[4 * RW + LORA:], jnp.zeros((pad,), F32)]).reshape(1, -1)
        w2p = jnp.concatenate([w2[l].astype(F32), zc(pad, RW)], axis=0).astype(BF16)
        a2p = jnp.concatenate([a2[l].astype(F32), zc(pad, RW)], axis=0).astype(BF16)

        p = _inproj(h, row(norm_g[l]), wcat, tm=1024, tn=768)
        p3 = p.reshape(B, T, -1)
        y_r = _rwkv(p3, 1, (4 * HW + 4 * RW) // (2 * LORA_PAD), mu_main, mu_lora,
                    row(w0[l]), w2p, row(a0[l]), a2p, row(k_k[l]), row(k_a[l]), row(r_k[l]),
                    row(lnx_w[l]), row(lnx_b[l]), ones_bd)
        y_h = _hgrn(p3, 0, lb_param.astype(F32), row(hgrn_norm_g[l]), l)
        wo = w_out[l].astype(BF16)
        h = _outproj(y_r.reshape(B * T, RW), y_h.reshape(B * T, HW), h,
                     wo[:RW], wo[RW:], row(final_g), tm=256, final=(l == depth - 1))
    return h.reshape(B, T, D)
```

```python
import functools
import math

import jax
import jax.numpy as jnp
from jax import lax
from jax.experimental import pallas as pl
from jax.experimental.pallas import tpu as pltpu

F32 = jnp.float32
BF16 = jnp.bfloat16

NORM_EPS = 1e-6
LNX_EPS = 64e-5
RWKV_HEAD = 64
HGRN_HEAD = 128
LORA = 96
LORA_BLK = 256
LANE = 128
CHUNK = 64
QUAD = 4
QW = QUAD * RWKV_HEAD
EXP_M05 = math.exp(-0.5)
VMEM_LIMIT = 48 * 1024 * 1024


def _mm(a, b):
    return jnp.dot(a.astype(BF16), b.astype(BF16), preferred_element_type=F32)


def _mm_nt(a, b):
    return lax.dot_general(a.astype(BF16), b.astype(BF16), (((1,), (1,)), ((), ())),
                           preferred_element_type=F32)


def _mm_tn(a, b):
    return lax.dot_general(a.astype(BF16), b.astype(BF16), (((0,), (0,)), ((), ())),
                           preferred_element_type=F32)


def _split3(x):
    hi = x.astype(BF16)
    r1 = x - hi.astype(F32)
    mid = r1.astype(BF16)
    lo = (r1 - mid.astype(F32)).astype(BF16)
    return hi, mid, lo


def _cumsum_rows(x, tril_bf16):
    hi, mid, lo = _split3(x)
    dot = functools.partial(jnp.dot, preferred_element_type=F32)
    return dot(tril_bf16, hi) + dot(tril_bf16, mid) + dot(tril_bf16, lo)


def _sigmoid(x):
    return 1.0 / (1.0 + jnp.exp(-x))


def _inproj_kernel(x_ref, g_ref, w_ref, o_ref, hn_ref):
    @pl.when(pl.program_id(1) == 0)
    def _():
        x = x_ref[...]
        ms = jnp.mean(x * x, axis=-1, keepdims=True)
        hn_ref[...] = (x * lax.rsqrt(ms + NORM_EPS) * g_ref[...]).astype(BF16)

    o_ref[...] = jnp.dot(hn_ref[...], w_ref[...].astype(BF16), preferred_element_type=F32)


def _inproj(x2, g, w_all, layer, tm, tn):
    m, d = x2.shape
    n = pl.cdiv(w_all.shape[2], tn) * tn
    return pl.pallas_call(
        _inproj_kernel,
        grid=(m // tm, n // tn),
        in_specs=[
            pl.BlockSpec((tm, d), lambda i, j: (i, 0)),
            pl.BlockSpec((1, d), lambda i, j: (0, 0)),
            pl.BlockSpec((None, d, tn), lambda i, j: (layer, 0, j)),
        ],
        out_specs=pl.BlockSpec((tm, tn), lambda i, j: (i, j)),
        out_shape=jax.ShapeDtypeStruct((m, n), F32),
        scratch_shapes=[pltpu.VMEM((tm, d), BF16)],
        compiler_params=pltpu.CompilerParams(
            dimension_semantics=("arbitrary", "arbitrary"), vmem_limit_bytes=VMEM_LIMIT),
        name="inproj",
    )(x2, g, w_all)


def _rwkv_kernel(pm_ref, pl_ref, mum_ref, mul_ref, w0_ref, w2_ref, a0_ref, a2_ref,
                 kk_ref, ka_ref, rk_ref, lw_ref, lb_ref, ones_ref,
                 o_ref, carry_m, carry_l, state):
    c = pl.program_id(0)
    C = CHUNK
    NB = pm_ref.shape[0]
    W = w0_ref.shape[1]
    NQ = W // QW

    @pl.when(c == 0)
    def _():
        carry_m[...] = jnp.zeros_like(carry_m)
        carry_l[...] = jnp.zeros_like(carry_l)
        state[...] = jnp.zeros_like(state)

    row = lax.broadcasted_iota(jnp.int32, (C, 1), 0)
    t_i = lax.broadcasted_iota(jnp.int32, (C, C), 0)
    s_i = lax.broadcasted_iota(jnp.int32, (C, C), 1)
    tril = (t_i >= s_i).astype(BF16)
    ones_bd = ones_ref[...]
    dot = functools.partial(jnp.dot, preferred_element_type=F32)

    lane = lax.broadcasted_iota(jnp.int32, (C, QW), 1)
    lane_blk = lane // RWKV_HEAD
    lane_s = lane % RWKV_HEAD
    trow = lax.broadcasted_iota(jnp.int32, (C, QW), 0)
    strict = lane_s < trow
    incl = lane_s <= trow
    eye_q = jnp.where(lane_s == trow, 1.0, 0.0).astype(F32)
    bd_mask = (lax.broadcasted_iota(jnp.int32, (QW, QW), 0) // RWKV_HEAD
               == lax.broadcasted_iota(jnp.int32, (QW, QW), 1) // RWKV_HEAD)

    def bd(x):
        xb = x.astype(BF16)
        zero = jnp.zeros_like(xb)
        return jnp.concatenate(
            [jnp.where(lane_blk == h, xb, zero) for h in range(QUAD)], axis=0)

    def segsum(x):
        hi = x.astype(BF16)
        lo = (x - hi.astype(F32)).astype(BF16)
        return dot(hi, ones_bd) + dot(lo, ones_bd)

    def shift_mix(p, carry_ref, b, mu):
        prev = jnp.where(row == 0, carry_ref[b:b + 1, :], pltpu.roll(p, 1, axis=0))
        carry_ref[b:b + 1, :] = p[C - 1:C, :]
        return p + mu * (prev - p)

    units = []
    for b in range(NB):
        xm = shift_mix(pm_ref[b], carry_m, b, mum_ref[...])
        xl = shift_mix(pl_ref[b], carry_l, b, mul_ref[...])
        r = xm[:, 0 * W:1 * W]
        k = xm[:, 1 * W:2 * W]
        v = xm[:, 2 * W:3 * W]
        gate = xm[:, 3 * W:4 * W]
        z = w0_ref[...] + _mm(jnp.tanh(xl), w2_ref[...])
        ld = -EXP_M05 * _sigmoid(z)
        a = _sigmoid(a0_ref[...] + _mm(xl, a2_ref[...]))
        G = _cumsum_rows(ld, tril)
        Gx = G - ld
        GC = G[C - 1:C, :]
        for q in range(NQ):
            sl = slice(q * QW, (q + 1) * QW)
            units.append(dict(b=b, q=q, sl=sl, r=r[:, sl], k=k[:, sl], v=v[:, sl], g=gate[:, sl],
                              a=a[:, sl], G=G[:, sl], Gx=Gx[:, sl], GC=GC[:, sl]))

    for u in units:
        u["kk"] = u["k"] * kk_ref[:, u["sl"]]
    for u in units:
        u["n2"] = segsum(u["kk"] * u["kk"])
    for u in units:
        sl = u["sl"]
        kkn = u["kk"] / jnp.maximum(jnp.sqrt(u["n2"]), 1e-12)
        k2 = u["k"] * (1.0 + (u["a"] - 1.0) * ka_ref[:, sl])
        av = -kkn
        bv = kkn * u["a"]
        einv = jnp.exp(-u["G"])
        eC = jnp.exp(u["GC"] - u["G"])
        u["k2"] = k2
        u["Rt"] = u["r"] * jnp.exp(u["G"])
        u["At"] = av * jnp.exp(u["Gx"])
        u["BDB"] = bd(bv * einv)
        u["BDK"] = bd(k2 * einv)
        u["BKe"] = jnp.concatenate([bv * eC, k2 * eC], axis=0).astype(BF16)
        u["PC"] = jnp.exp(u["GC"])
        u["BDV"] = bd(u["v"])
        u["lhs"] = jnp.concatenate([u["At"], u["Rt"]], axis=0).astype(BF16)

    for u in units:
        u["SB"] = _mm_nt(u["lhs"], u["BDB"])
    for u in units:
        u["SK"] = _mm_nt(u["lhs"], u["BDK"])
    for u in units:
        u["N"] = jnp.where(strict, u["SB"][:C], 0.0)
        u["Aak"] = jnp.where(strict, u["SK"][:C], 0.0)
        u["Lrb"] = jnp.where(incl, u["SB"][C:], 0.0)
        u["Lrk"] = jnp.where(incl, u["SK"][C:], 0.0)

    for u in units:
        u["P"] = eye_q + u["N"]
        u["Np"] = _mm(u["N"], bd(u["N"]))
    for _ in range(4):
        for u in units:
            out = _mm(jnp.concatenate([u["P"], u["Np"]], axis=0), bd(u["Np"]))
            u["P"] = u["P"] + out[:C]
            u["Np"] = out[C:]
    for u in units:
        u["P"] = u["P"] + _mm(u["P"], bd(u["Np"]))

    for u in units:
        u["S"] = state[u["b"] * NQ + u["q"]]
        u["Sb"] = u["S"].astype(BF16)
        u["Z0"] = _mm_nt(u["At"], u["Sb"]) + _mm(u["Aak"], u["BDV"])
    for u in units:
        u["U"] = _mm(u["P"], bd(u["Z0"]))
    for u in units:
        u["Y"] = (_mm_nt(u["Rt"], u["Sb"]) + _mm(u["Lrb"], bd(u["U"]))
                  + _mm(u["Lrk"], u["BDV"]))
    for u in units:
        upd = _mm_tn(jnp.concatenate([u["U"], u["v"]], axis=0), u["BKe"])
        state[u["b"] * NQ + u["q"]] = u["S"] * u["PC"] + jnp.where(bd_mask, upd, 0.0)

    for u in units:
        u["mean"] = segsum(u["Y"]) * (1.0 / RWKV_HEAD)
        u["bonus"] = segsum(u["r"] * u["k2"] * rk_ref[:, u["sl"]])
    for u in units:
        u["yc"] = u["Y"] - u["mean"]
        u["var"] = segsum(u["yc"] * u["yc"]) * (1.0 / RWKV_HEAD)
    for u in units:
        sl = u["sl"]
        yn = u["yc"] * lax.rsqrt(u["var"] + LNX_EPS) * lw_ref[:, sl] + lb_ref[:, sl]
        res = (yn + u["bonus"] * u["v"]) * (u["g"] * _sigmoid(u["g"]))
        o_ref[u["b"], :, sl] = res.astype(o_ref.dtype)


def _rwkv(p3, main_blk, lora_blk, mu_m, mu_l, w0, w2p, a0, a2p, k_k, k_a, r_k, lnx_w, lnx_b, ones_bd):
    B, T, _ = p3.shape
    W = w0.shape[1]
    C = CHUNK
    full = lambda shape: pl.BlockSpec(shape, lambda c: (0,) * len(shape))
    return pl.pallas_call(
        _rwkv_kernel,
        grid=(T // C,),
        in_specs=[
            pl.BlockSpec((B, C, 4 * W), lambda c: (0, c, main_blk)),
            pl.BlockSpec((B, C, LORA_BLK), lambda c: (0, c, lora_blk)),
            full((1, 4 * W)), full((1, LORA_BLK)),
            full((1, W)), full((LORA_BLK, W)), full((1, W)), full((LORA_BLK, W)),
            full((1, W)), full((1, W)), full((1, W)), full((1, W)), full((1, W)),
            full((QW, QW)),
        ],
        out_specs=pl.BlockSpec((B, C, W), lambda c: (0, c, 0)),
        out_shape=jax.ShapeDtypeStruct((B, T, W), BF16),
        scratch_shapes=[
            pltpu.VMEM((B, 4 * W), F32),
            pltpu.VMEM((B, LORA_BLK), F32),
            pltpu.VMEM((B * (W // QW), QW, QW), F32),
        ],
        compiler_params=pltpu.CompilerParams(
            dimension_semantics=("arbitrary",), vmem_limit_bytes=VMEM_LIMIT),
        name="rwkv7_mix",
    )(p3, p3, mu_m, mu_l, w0, w2p, a0, a2p, k_k, k_a, r_k, lnx_w, lnx_b, ones_bd)


def _hgrn_kernel(layer, off, ph_ref, lbp_ref, ng_ref, o_ref, state):
    c = pl.program_id(0)
    C = CHUNK
    NB = ph_ref.shape[0]
    W = ng_ref.shape[1]
    D = HGRN_HEAD
    NH = W // D

    @pl.when(c == 0)
    def _():
        state[...] = jnp.zeros_like(state)

    lp = lbp_ref[...]
    e = jnp.exp(lp - jnp.max(lp, axis=0, keepdims=True))
    lb = jnp.sum(e[:layer + 1], axis=0, keepdims=True) / jnp.sum(e, axis=0, keepdims=True)

    t_i = lax.broadcasted_iota(jnp.int32, (C, C), 0)
    s_i = lax.broadcasted_iota(jnp.int32, (C, C), 1)
    causal = t_i >= s_i
    tril = causal.astype(BF16)

    units = []
    for b in range(NB):
        x = ph_ref[b]
        qv = x[:, off + 0 * W:off + 1 * W]
        fr = x[:, off + 1 * W:off + 2 * W]
        iv = x[:, off + 2 * W:off + 3 * W]
        gate = x[:, off + 3 * W:off + 4 * W]
        f = lb + (1.0 - lb) * _sigmoid(fr)
        kx = 1.0 - f
        G = _cumsum_rows(jnp.log(f), tril)
        GC = G[C - 1:C, :]
        qt = (qv * jnp.exp(G)).astype(BF16)
        kb = (kx * jnp.exp(-G)).astype(BF16)
        kd = (kx * jnp.exp(GC - G)).astype(BF16)
        PC = jnp.exp(GC)
        ivb = iv.astype(BF16)
        for h in range(NH):
            sl = slice(h * D, (h + 1) * D)
            units.append(dict(b=b, h=h, sl=sl, qt=qt[:, sl], kb=kb[:, sl], kd=kd[:, sl],
                              iv=ivb[:, sl], PC=PC[:, sl], g=gate[:, sl]))

    for u in units:
        u["A"] = _mm_nt(u["qt"], u["kb"])
    for u in units:
        u["S"] = state[u["b"] * NH + u["h"]]
        u["inter"] = _mm_nt(u["qt"], u["S"])
    for u in units:
        u["o"] = _mm(jnp.where(causal, u["A"], 0.0), u["iv"]) + u["inter"]
    for u in units:
        state[u["b"] * NH + u["h"]] = u["S"] * u["PC"] + _mm_tn(u["iv"], u["kd"])
    for u in units:
        o = u["o"]
        g = u["g"]
        ms = jnp.mean(o * o, axis=-1, keepdims=True)
        res = o * lax.rsqrt(ms + NORM_EPS) * ng_ref[:, u["sl"]] * (g * _sigmoid(g))
        o_ref[u["b"], :, u["sl"]] = res.astype(o_ref.dtype)


def _hgrn(p3, blk_w, blk, off, lb_param, norm_g, layer):
    B, T, _ = p3.shape
    W = norm_g.shape[1]
    C = CHUNK
    L = lb_param.shape[0]
    return pl.pallas_call(
        functools.partial(_hgrn_kernel, layer, off),
        grid=(T // C,),
        in_specs=[
            pl.BlockSpec((B, C, blk_w), lambda c: (0, c, blk)),
            pl.BlockSpec((L, W), lambda c: (0, 0)),
            pl.BlockSpec((1, W), lambda c: (0, 0)),
        ],
        out_specs=pl.BlockSpec((B, C, W), lambda c: (0, c, 0)),
        out_shape=jax.ShapeDtypeStruct((B, T, W), BF16),
        scratch_shapes=[pltpu.VMEM((B * (W // HGRN_HEAD), HGRN_HEAD, HGRN_HEAD), F32)],
        compiler_params=pltpu.CompilerParams(
            dimension_semantics=("arbitrary",), vmem_limit_bytes=VMEM_LIMIT),
        name="hgrn2_mix",
    )(p3, lb_param, norm_g)


def _outproj_kernel(final, yr_ref, yh_ref, x_ref, wr_ref, wh_ref, g_ref, o_ref):
    acc = jnp.dot(yr_ref[...], wr_ref[...], preferred_element_type=F32)
    acc += jnp.dot(yh_ref[...], wh_ref[...], preferred_element_type=F32)
    h = x_ref[...] + acc
    if final:
        ms = jnp.mean(h * h, axis=-1, keepdims=True)
        h = h * lax.rsqrt(ms + NORM_EPS) * g_ref[...]
    o_ref[...] = h


def _outproj(yr, yh, x2, wr, wh, g, tm, final):
    m, d = x2.shape
    wr_w = yr.shape[1]
    wh_w = yh.shape[1]
    return pl.pallas_call(
        functools.partial(_outproj_kernel, final),
        grid=(m // tm,),
        in_specs=[
            pl.BlockSpec((tm, wr_w), lambda i: (i, 0)),
            pl.BlockSpec((tm, wh_w), lambda i: (i, 0)),
            pl.BlockSpec((tm, d), lambda i: (i, 0)),
            pl.BlockSpec((wr_w, d), lambda i: (0, 0)),
            pl.BlockSpec((wh_w, d), lambda i: (0, 0)),
            pl.BlockSpec((1, d), lambda i: (0, 0)),
        ],
        out_specs=pl.BlockSpec((tm, d), lambda i: (i, 0)),
        out_shape=jax.ShapeDtypeStruct((m, d), F32),
        compiler_params=pltpu.CompilerParams(
            dimension_semantics=("arbitrary",), vmem_limit_bytes=VMEM_LIMIT),
        name="outproj",
    )(yr, yh, x2, wr, wh, g)


def kernel(x, norm_g, w_in, mu, w0, w2, a0, a2, k_k, k_a, r_k, lnx_w, lnx_b,
           hgrn_norm_g, lb_param, w_out, final_g):
    B, T, D = x.shape
    depth = w_in.shape[0]
    RW = w0.shape[1]
    HW = hgrn_norm_g.shape[1]
    n_r = 4 * RW + 2 * LORA
    tn = 768
    n_p = pl.cdiv(w_in.shape[2], tn) * tn
    h_start = (n_r // LANE) * LANE
    h_off = n_r - h_start
    h_blk_w = n_p - h_start
    assert (4 * RW) % LORA_BLK == 0 and 2 * LORA <= LORA_BLK and w2.shape[1] == LORA
    assert h_start % h_blk_w == 0 and h_off + 4 * HW <= h_blk_w

    row = lambda z: z.reshape(1, -1).astype(F32)
    zr = lambda rows: jnp.zeros((rows, RW), F32)
    ones_bd = (jnp.arange(QW)[:, None] // RWKV_HEAD == jnp.arange(QW)[None, :] // RWKV_HEAD).astype(BF16)

    h = x.reshape(B * T, D)
    for l in range(depth):
        mu_l = mu[l].astype(F32)
        mu_main = mu_l[:4 * RW].reshape(1, -1)
        mu_lora = jnp.concatenate(
            [mu_l[4 * RW:], jnp.zeros((LORA_BLK - 2 * LORA,), F32)]).reshape(1, -1)
        w2p = jnp.concatenate([w2[l].astype(F32), zr(LORA_BLK - LORA)], axis=0).astype(BF16)
        a2p = jnp.concatenate([zr(LORA), a2[l].astype(F32), zr(LORA_BLK - 2 * LORA)],
                              axis=0).astype(BF16)

        p = _inproj(h, row(norm_g[l]), w_in, l, tm=1024, tn=tn)
        p3 = p.reshape(B, T, n_p)
        y_r = _rwkv(p3, 0, (4 * RW) // LORA_BLK, mu_main, mu_lora,
                    row(w0[l]), w2p, row(a0[l]), a2p, row(k_k[l]), row(k_a[l]), row(r_k[l]),
                    row(lnx_w[l]), row(lnx_b[l]), ones_bd)
        y_h = _hgrn(p3, h_blk_w, h_start // h_blk_w, h_off, lb_param.astype(F32),
                    row(hgrn_norm_g[l]), l)
        wo = w_out[l].astype(BF16)
        h = _outproj(y_r.reshape(B * T, RW), y_h.reshape(B * T, HW), h,
                     wo[:RW], wo[RW:], row(final_g), tm=256, final=(l == depth - 1))
    return h.reshape(B, T, D)
```

```python
import functools
import math

import jax
import jax.numpy as jnp
from jax import lax
from jax.experimental import pallas as pl
from jax.experimental.pallas import tpu as pltpu

F32 = jnp.float32
BF16 = jnp.bfloat16

NORM_EPS = 1e-6
LNX_EPS = 64e-5
RWKV_HEAD = 64
HGRN_HEAD = 128
LORA = 96
LORA_BLK = 256
LANE = 128
CHUNK = 64
QUAD = 4
QW = QUAD * RWKV_HEAD
EXP_M05 = math.exp(-0.5)
VMEM_LIMIT = 48 * 1024 * 1024


def _mm(a, b):
    return jnp.dot(a.astype(BF16), b.astype(BF16), preferred_element_type=F32)


def _mm_nt(a, b):
    return lax.dot_general(a.astype(BF16), b.astype(BF16), (((1,), (1,)), ((), ())),
                           preferred_element_type=F32)


def _mm_tn(a, b):
    return lax.dot_general(a.astype(BF16), b.astype(BF16), (((0,), (0,)), ((), ())),
                           preferred_element_type=F32)


def _split3(x):
    hi = x.astype(BF16)
    r1 = x - hi.astype(F32)
    mid = r1.astype(BF16)
    lo = (r1 - mid.astype(F32)).astype(BF16)
    return hi, mid, lo


def _cumsum_rows(x, tril_bf16):
    hi, mid, lo = _split3(x)
    dot = functools.partial(jnp.dot, preferred_element_type=F32)
    return dot(tril_bf16, hi) + dot(tril_bf16, mid) + dot(tril_bf16, lo)


def _sigmoid(x):
    return 1.0 / (1.0 + jnp.exp(-x))


def _inproj_kernel(x_ref, g_ref, w_ref, o_ref, hn_ref):
    @pl.when(pl.program_id(1) == 0)
    def _():
        x = x_ref[...]
        ms = jnp.mean(x * x, axis=-1, keepdims=True)
        hn_ref[...] = (x * lax.rsqrt(ms + NORM_EPS) * g_ref[...]).astype(BF16)

    o_ref[...] = _mm_nt(hn_ref[...], w_ref[...])


def _inproj(x2, g, wt_all, layer, tm, tn):
    m, d = x2.shape
    n = pl.cdiv(wt_all.shape[1], tn) * tn
    return pl.pallas_call(
        _inproj_kernel,
        grid=(m // tm, n // tn),
        in_specs=[
            pl.BlockSpec((tm, d), lambda i, j: (i, 0)),
            pl.BlockSpec((1, d), lambda i, j: (0, 0)),
            pl.BlockSpec((None, tn, d), lambda i, j: (layer, j, 0)),
        ],
        out_specs=pl.BlockSpec((tm, tn), lambda i, j: (i, j)),
        out_shape=jax.ShapeDtypeStruct((m, n), F32),
        scratch_shapes=[pltpu.VMEM((tm, d), BF16)],
        compiler_params=pltpu.CompilerParams(
            dimension_semantics=("arbitrary", "arbitrary"), vmem_limit_bytes=VMEM_LIMIT),
        name="inproj",
    )(x2, g, wt_all)


def _rwkv_kernel(pm_ref, pl_ref, mum_ref, mul_ref, w0_ref, w2_ref, a0_ref, a2_ref,
                 kk_ref, ka_ref, rk_ref, lw_ref, lb_ref, ones_ref,
                 o_ref, carry_m, carry_l, state):
    c = pl.program_id(0)
    C = CHUNK
    NB = pm_ref.shape[0]
    W = w0_ref.shape[1]
    NQ = W // QW

    @pl.when(c == 0)
    def _():
        carry_m[...] = jnp.zeros_like(carry_m)
        carry_l[...] = jnp.zeros_like(carry_l)
        state[...] = jnp.zeros_like(state)

    row = lax.broadcasted_iota(jnp.int32, (C, 1), 0)
    t_i = lax.broadcasted_iota(jnp.int32, (C, C), 0)
    s_i = lax.broadcasted_iota(jnp.int32, (C, C), 1)
    tril = (t_i >= s_i).astype(BF16)
    ones_bd = ones_ref[...]
    dot = functools.partial(jnp.dot, preferred_element_type=F32)

    lane = lax.broadcasted_iota(jnp.int32, (C, QW), 1)
    lane_blk = lane // RWKV_HEAD
    lane_s = lane % RWKV_HEAD
    trow = lax.broadcasted_iota(jnp.int32, (C, QW), 0)
    strict = lane_s < trow
    incl = lane_s <= trow
    eye_q = jnp.where(lane_s == trow, 1.0, 0.0).astype(F32)
    bd_mask = (lax.broadcasted_iota(jnp.int32, (QW, QW), 0) // RWKV_HEAD
               == lax.broadcasted_iota(jnp.int32, (QW, QW), 1) // RWKV_HEAD)

    def bd(x):
        xb = x.astype(BF16)
        zero = jnp.zeros_like(xb)
        return jnp.concatenate(
            [jnp.where(lane_blk == h, xb, zero) for h in range(QUAD)], axis=0)

    def segsum(x):
        hi = x.astype(BF16)
        lo = (x - hi.astype(F32)).astype(BF16)
        return dot(hi, ones_bd) + dot(lo, ones_bd)

    def shift_mix(p, carry_ref, b, mu):
        prev = jnp.where(row == 0, carry_ref[b:b + 1, :], pltpu.roll(p, 1, axis=0))
        carry_ref[b:b + 1, :] = p[C - 1:C, :]
        return p + mu * (prev - p)

    units = []
    for b in range(NB):
        xm = shift_mix(pm_ref[b], carry_m, b, mum_ref[...])
        xl = shift_mix(pl_ref[b], carry_l, b, mul_ref[...])
        r = xm[:, 0 * W:1 * W]
        k = xm[:, 1 * W:2 * W]
        v = xm[:, 2 * W:3 * W]
        gate = xm[:, 3 * W:4 * W]
        z = w0_ref[...] + _mm(jnp.tanh(xl), w2_ref[...])
        ld = -EXP_M05 * _sigmoid(z)
        a = _sigmoid(a0_ref[...] + _mm(xl, a2_ref[...]))
        G = _cumsum_rows(ld, tril)
        Gx = G - ld
        GC = G[C - 1:C, :]
        for q in range(NQ):
            sl = slice(q * QW, (q + 1) * QW)
            units.append(dict(b=b, q=q, sl=sl, r=r[:, sl], k=k[:, sl], v=v[:, sl], g=gate[:, sl],
                              a=a[:, sl], G=G[:, sl], Gx=Gx[:, sl], GC=GC[:, sl]))

    for u in units:
        u["kk"] = u["k"] * kk_ref[:, u["sl"]]
    for u in units:
        u["n2"] = segsum(u["kk"] * u["kk"])
    for u in units:
        sl = u["sl"]
        kkn = u["kk"] / jnp.maximum(jnp.sqrt(u["n2"]), 1e-12)
        k2 = u["k"] * (1.0 + (u["a"] - 1.0) * ka_ref[:, sl])
        av = -kkn
        bv = kkn * u["a"]
        einv = jnp.exp(-u["G"])
        eC = jnp.exp(u["GC"] - u["G"])
        u["k2"] = k2
        u["Rt"] = u["r"] * jnp.exp(u["G"])
        u["At"] = av * jnp.exp(u["Gx"])
        u["BDB"] = bd(bv * einv)
        u["BDK"] = bd(k2 * einv)
        u["BKe"] = jnp.concatenate([bv * eC, k2 * eC], axis=0).astype(BF16)
        u["PC"] = jnp.exp(u["GC"])
        u["BDV"] = bd(u["v"])
        u["lhs"] = jnp.concatenate([u["At"], u["Rt"]], axis=0).astype(BF16)

    for u in units:
        u["SB"] = _mm_nt(u["lhs"], u["BDB"])
    for u in units:
        u["SK"] = _mm_nt(u["lhs"], u["BDK"])
    for u in units:
        u["N"] = jnp.where(strict, u["SB"][:C], 0.0)
        u["Aak"] = jnp.where(strict, u["SK"][:C], 0.0)
        u["Lrb"] = jnp.where(incl, u["SB"][C:], 0.0)
        u["Lrk"] = jnp.where(incl, u["SK"][C:], 0.0)

    for u in units:
        u["P"] = eye_q + u["N"]
        u["Np"] = _mm(u["N"], bd(u["N"]))
    for _ in range(4):
        for u in units:
            out = _mm(jnp.concatenate([u["P"], u["Np"]], axis=0), bd(u["Np"]))
            u["P"] = u["P"] + out[:C]
            u["Np"] = out[C:]
    for u in units:
        u["P"] = u["P"] + _mm(u["P"], bd(u["Np"]))

    for u in units:
        u["S"] = state[u["b"] * NQ + u["q"]]
        u["Sb"] = u["S"].astype(BF16)
        u["Z0"] = _mm_nt(u["At"], u["Sb"]) + _mm(u["Aak"], u["BDV"])
    for u in units:
        u["U"] = _mm(u["P"], bd(u["Z0"]))
    for u in units:
        u["Y"] = (_mm_nt(u["Rt"], u["Sb"]) + _mm(u["Lrb"], bd(u["U"]))
                  + _mm(u["Lrk"], u["BDV"]))
    for u in units:
        upd = _mm_tn(jnp.concatenate([u["U"], u["v"]], axis=0), u["BKe"])
        state[u["b"] * NQ + u["q"]] = u["S"] * u["PC"] + jnp.where(bd_mask, upd, 0.0)

    for u in units:
        u["mean"] = segsum(u["Y"]) * (1.0 / RWKV_HEAD)
        u["bonus"] = segsum(u["r"] * u["k2"] * rk_ref[:, u["sl"]])
    for u in units:
        u["yc"] = u["Y"] - u["mean"]
        u["var"] = segsum(u["yc"] * u["yc"]) * (1.0 / RWKV_HEAD)
    for u in units:
        sl = u["sl"]
        yn = u["yc"] * lax.rsqrt(u["var"] + LNX_EPS) * lw_ref[:, sl] + lb_ref[:, sl]
        res = (yn + u["bonus"] * u["v"]) * (u["g"] * _sigmoid(u["g"]))
        o_ref[u["b"], :, sl] = res.astype(o_ref.dtype)


def _rwkv(p3, main_blk, lora_blk, mu_m, mu_l, w0, w2p, a0, a2p, k_k, k_a, r_k, lnx_w, lnx_b, ones_bd):
    B, T, _ = p3.shape
    W = w0.shape[1]
    C = CHUNK
    full = lambda shape: pl.BlockSpec(shape, lambda c: (0,) * len(shape))
    return pl.pallas_call(
        _rwkv_kernel,
        grid=(T // C,),
        in_specs=[
            pl.BlockSpec((B, C, 4 * W), lambda c: (0, c, main_blk)),
            pl.BlockSpec((B, C, LORA_BLK), lambda c: (0, c, lora_blk)),
            full((1, 4 * W)), full((1, LORA_BLK)),
            full((1, W)), full((LORA_BLK, W)), full((1, W)), full((LORA_BLK, W)),
            full((1, W)), full((1, W)), full((1, W)), full((1, W)), full((1, W)),
            full((QW, QW)),
        ],
        out_specs=pl.BlockSpec((B, C, W), lambda c: (0, c, 0)),
        out_shape=jax.ShapeDtypeStruct((B, T, W), BF16),
        scratch_shapes=[
            pltpu.VMEM((B, 4 * W), F32),
            pltpu.VMEM((B, LORA_BLK), F32),
            pltpu.VMEM((B * (W // QW), QW, QW), F32),
        ],
        compiler_params=pltpu.CompilerParams(
            dimension_semantics=("arbitrary",), vmem_limit_bytes=VMEM_LIMIT),
        name="rwkv7_mix",
    )(p3, p3, mu_m, mu_l, w0, w2p, a0, a2p, k_k, k_a, r_k, lnx_w, lnx_b, ones_bd)


def _hgrn_kernel(layer, off, ph_ref, lbp_ref, ng_ref, o_ref, state):
    c = pl.program_id(0)
    C = CHUNK
    NB = ph_ref.shape[0]
    W = ng_ref.shape[1]
    D = HGRN_HEAD
    NH = W // D

    @pl.when(c == 0)
    def _():
        state[...] = jnp.zeros_like(state)

    lp = lbp_ref[...]
    e = jnp.exp(lp - jnp.max(lp, axis=0, keepdims=True))
    lb = jnp.sum(e[:layer + 1], axis=0, keepdims=True) / jnp.sum(e, axis=0, keepdims=True)

    t_i = lax.broadcasted_iota(jnp.int32, (C, C), 0)
    s_i = lax.broadcasted_iota(jnp.int32, (C, C), 1)
    causal = t_i >= s_i
    tril = causal.astype(BF16)

    units = []
    for b in range(NB):
        x = ph_ref[b]
        qv = x[:, off + 0 * W:off + 1 * W]
        fr = x[:, off + 1 * W:off + 2 * W]
        iv = x[:, off + 2 * W:off + 3 * W]
        gate = x[:, off + 3 * W:off + 4 * W]
        f = lb + (1.0 - lb) * _sigmoid(fr)
        kx = 1.0 - f
        G = _cumsum_rows(jnp.log(f), tril)
        GC = G[C - 1:C, :]
        qt = (qv * jnp.exp(G)).astype(BF16)
        kb = (kx * jnp.exp(-G)).astype(BF16)
        kd = (kx * jnp.exp(GC - G)).astype(BF16)
        PC = jnp.exp(GC)
        ivb = iv.astype(BF16)
        for h in range(NH):
            sl = slice(h * D, (h + 1) * D)
            units.append(dict(b=b, h=h, sl=sl, qt=qt[:, sl], kb=kb[:, sl], kd=kd[:, sl],
                              iv=ivb[:, sl], PC=PC[:, sl], g=gate[:, sl]))

    for u in units:
        u["A"] = _mm_nt(u["qt"], u["kb"])
    for u in units:
        u["S"] = state[u["b"] * NH + u["h"]]
        u["inter"] = _mm_nt(u["qt"], u["S"])
    for u in units:
        u["o"] = _mm(jnp.where(causal, u["A"], 0.0), u["iv"]) + u["inter"]
    for u in units:
        state[u["b"] * NH + u["h"]] = u["S"] * u["PC"] + _mm_tn(u["iv"], u["kd"])
    for u in units:
        o = u["o"]
        g = u["g"]
        ms = jnp.mean(o * o, axis=-1, keepdims=True)
        res = o * lax.rsqrt(ms + NORM_EPS) * ng_ref[:, u["sl"]] * (g * _sigmoid(g))
        o_ref[u["b"], :, u["sl"]] = res.astype(o_ref.dtype)


def _hgrn(p3, blk_w, blk, off, lb_param, norm_g, layer):
    B, T, _ = p3.shape
    W = norm_g.shape[1]
    C = CHUNK
    L = lb_param.shape[0]
    return pl.pallas_call(
        functools.partial(_hgrn_kernel, layer, off),
        grid=(T // C,),
        in_specs=[
            pl.BlockSpec((B, C, blk_w), lambda c: (0, c, blk)),
            pl.BlockSpec((L, W), lambda c: (0, 0)),
            pl.BlockSpec((1, W), lambda c: (0, 0)),
        ],
        out_specs=pl.BlockSpec((B, C, W), lambda c: (0, c, 0)),
        out_shape=jax.ShapeDtypeStruct((B, T, W), BF16),
        scratch_shapes=[pltpu.VMEM((B * (W // HGRN_HEAD), HGRN_HEAD, HGRN_HEAD), F32)],
        compiler_params=pltpu.CompilerParams(
            dimension_semantics=("arbitrary",), vmem_limit_bytes=VMEM_LIMIT),
        name="hgrn2_mix",
    )(p3, lb_param, norm_g)


def _outproj_kernel(final, yr_ref, yh_ref, x_ref, wr_ref, wh_ref, g_ref, o_ref):
    acc = jnp.dot(yr_ref[...], wr_ref[...], preferred_element_type=F32)
    acc += jnp.dot(yh_ref[...], wh_ref[...], preferred_element_type=F32)
    h = x_ref[...] + acc
    if final:
        ms = jnp.mean(h * h, axis=-1, keepdims=True)
        h = h * lax.rsqrt(ms + NORM_EPS) * g_ref[...]
    o_ref[...] = h


def _outproj(yr, yh, x2, wr, wh, g, tm, final):
    m, d = x2.shape
    wr_w = yr.shape[1]
    wh_w = yh.shape[1]
    return pl.pallas_call(
        functools.partial(_outproj_kernel, final),
        grid=(m // tm,),
        in_specs=[
            pl.BlockSpec((tm, wr_w), lambda i: (i, 0)),
            pl.BlockSpec((tm, wh_w), lambda i: (i, 0)),
            pl.BlockSpec((tm, d), lambda i: (i, 0)),
            pl.BlockSpec((wr_w, d), lambda i: (0, 0)),
            pl.BlockSpec((wh_w, d), lambda i: (0, 0)),
            pl.BlockSpec((1, d), lambda i: (0, 0)),
        ],
        out_specs=pl.BlockSpec((tm, d), lambda i: (i, 0)),
        out_shape=jax.ShapeDtypeStruct((m, d), F32),
        compiler_params=pltpu.CompilerParams(
            dimension_semantics=("arbitrary",), vmem_limit_bytes=VMEM_LIMIT),
        name="outproj",
    )(yr, yh, x2, wr, wh, g)


def kernel(x, norm_g, w_in, mu, w0, w2, a0, a2, k_k, k_a, r_k, lnx_w, lnx_b,
           hgrn_norm_g, lb_param, w_out, final_g):
    B, T, D = x.shape
    depth = w_in.shape[0]
    RW = w0.shape[1]
    HW = hgrn_norm_g.shape[1]
    n_r = 4 * RW + 2 * LORA
    tn = 768
    n_p = pl.cdiv(w_in.shape[2], tn) * tn
    h_start = (n_r // LANE) * LANE
    h_off = n_r - h_start
    h_blk_w = n_p - h_start
    assert (4 * RW) % LORA_BLK == 0 and 2 * LORA <= LORA_BLK and w2.shape[1] == LORA
    assert h_start % h_blk_w == 0 and h_off + 4 * HW <= h_blk_w

    row = lambda z: z.reshape(1, -1).astype(F32)
    zr = lambda rows: jnp.zeros((rows, RW), F32)
    ones_bd = (jnp.arange(QW)[:, None] // RWKV_HEAD == jnp.arange(QW)[None, :] // RWKV_HEAD).astype(BF16)

    w_in_t = jnp.swapaxes(w_in, 1, 2)
    h = x.reshape(B * T, D)
    for l in range(depth):
        mu_l = mu[l].astype(F32)
        mu_main = mu_l[:4 * RW].reshape(1, -1)
        mu_lora = jnp.concatenate(
            [mu_l[4 * RW:], jnp.zeros((LORA_BLK - 2 * LORA,), F32)]).reshape(1, -1)
        w2p = jnp.concatenate([w2[l].astype(F32), zr(LORA_BLK - LORA)], axis=0).astype(BF16)
        a2p = jnp.concatenate([zr(LORA), a2[l].astype(F32), zr(LORA_BLK - 2 * LORA)],
                              axis=0).astype(BF16)

        p = _inproj(h, row(norm_g[l]), w_in_t, l, tm=1024, tn=tn)
        p3 = p.reshape(B, T, n_p)
        y_r = _rwkv(p3, 0, (4 * RW) // LORA_BLK, mu_main, mu_lora,
                    row(w0[l]), w2p, row(a0[l]), a2p, row(k_k[l]), row(k_a[l]), row(r_k[l]),
                    row(lnx_w[l]), row(lnx_b[l]), ones_bd)
        y_h = _hgrn(p3, h_blk_w, h_start // h_blk_w, h_off, lb_param.astype(F32),
                    row(hgrn_norm_g[l]), l)
        wo = w_out[l].astype(BF16)
        h = _outproj(y_r.reshape(B * T, RW), y_h.reshape(B * T, HW), h,
                     wo[:RW], wo[RW:], row(final_g), tm=256, final=(l == depth - 1))
    return h.reshape(B, T, D)
```

```python
import functools
import math

import jax
import jax.numpy as jnp
from jax import lax
from jax.experimental import pallas as pl
from jax.experimental.pallas import tpu as pltpu

F32 = jnp.float32
BF16 = jnp.bfloat16

NORM_EPS = 1e-6
LNX_EPS = 64e-5
RWKV_HEAD = 64
HGRN_HEAD = 128
LORA = 96
LORA_BLK = 256
LANE = 128
CHUNK = 64
QUAD = 4
QW = QUAD * RWKV_HEAD
BD_ROLES = ("B", "K", "V", "N", "Z", "U")
ST_BF = ("At", "Rt", "P", "Lrb", "v", "BKe")
ST_F = ("AV", "LV", "bonusv", "sg")
EXP_M05 = math.exp(-0.5)
VMEM_LIMIT = 48 * 1024 * 1024


def _mm(a, b):
    return jnp.dot(a.astype(BF16), b.astype(BF16), preferred_element_type=F32)


def _mm_nt(a, b):
    return lax.dot_general(a.astype(BF16), b.astype(BF16), (((1,), (1,)), ((), ())),
                           preferred_element_type=F32)


def _mm_tn(a, b):
    return lax.dot_general(a.astype(BF16), b.astype(BF16), (((0,), (0,)), ((), ())),
                           preferred_element_type=F32)


def _split3(x):
    hi = x.astype(BF16)
    r1 = x - hi.astype(F32)
    mid = r1.astype(BF16)
    lo = (r1 - mid.astype(F32)).astype(BF16)
    return hi, mid, lo


def _cumsum_rows(x, tril_bf16):
    hi, mid, lo = _split3(x)
    dot = functools.partial(jnp.dot, preferred_element_type=F32)
    return dot(tril_bf16, hi) + dot(tril_bf16, mid) + dot(tril_bf16, lo)


def _sigmoid(x):
    return 1.0 / (1.0 + jnp.exp(-x))


def _inproj_kernel(x_ref, g_ref, w_ref, o_ref, hn_ref):
    @pl.when(pl.program_id(1) == 0)
    def _():
        x = x_ref[...]
        ms = jnp.mean(x * x, axis=-1, keepdims=True)
        hn_ref[...] = (x * lax.rsqrt(ms + NORM_EPS) * g_ref[...]).astype(BF16)

    o_ref[...] = _mm_nt(hn_ref[...], w_ref[...])


def _inproj(x2, g, wt_all, layer, tm, tn):
    m, d = x2.shape
    n = pl.cdiv(wt_all.shape[1], tn) * tn
    return pl.pallas_call(
        _inproj_kernel,
        grid=(m // tm, n // tn),
        in_specs=[
            pl.BlockSpec((tm, d), lambda i, j: (i, 0)),
            pl.BlockSpec((1, d), lambda i, j: (0, 0)),
            pl.BlockSpec((None, tn, d), lambda i, j: (layer, j, 0)),
        ],
        out_specs=pl.BlockSpec((tm, tn), lambda i, j: (i, j)),
        out_shape=jax.ShapeDtypeStruct((m, n), F32),
        scratch_shapes=[pltpu.VMEM((tm, d), BF16)],
        compiler_params=pltpu.CompilerParams(
            dimension_semantics=("arbitrary", "arbitrary"), vmem_limit_bytes=VMEM_LIMIT),
        name="inproj",
    )(x2, g, wt_all)


def _rwkv_kernel(pm_ref, pl_ref, mum_ref, mul_ref, w0_ref, w2_ref, a0_ref, a2_ref,
                 kk_ref, ka_ref, rk_ref, lw_ref, lb_ref, ones_ref,
                 o_ref, carry_m, carry_l, state, bd_ref, st_bf, st_f):
    step = pl.program_id(0)
    C = CHUNK
    NB = pm_ref.shape[0]
    W = w0_ref.shape[1]
    NQ = W // QW

    @pl.when(step == 0)
    def _():
        for ref in (carry_m, carry_l, state, bd_ref, st_bf, st_f):
            ref[...] = jnp.zeros_like(ref)

    row = lax.broadcasted_iota(jnp.int32, (C, 1), 0)
    t_i = lax.broadcasted_iota(jnp.int32, (C, C), 0)
    s_i = lax.broadcasted_iota(jnp.int32, (C, C), 1)
    tril = (t_i >= s_i).astype(BF16)
    ones_bd = ones_ref[...]
    dot = functools.partial(jnp.dot, preferred_element_type=F32)

    lane = lax.broadcasted_iota(jnp.int32, (C, QW), 1)
    lane_blk = lane // RWKV_HEAD
    lane_s = lane % RWKV_HEAD
    trow = lax.broadcasted_iota(jnp.int32, (C, QW), 0)
    strict = lane_s < trow
    incl = lane_s <= trow
    eye_q = jnp.where(lane_s == trow, 1.0, 0.0).astype(F32)
    bd_mask = (lax.broadcasted_iota(jnp.int32, (QW, QW), 0) // RWKV_HEAD
               == lax.broadcasted_iota(jnp.int32, (QW, QW), 1) // RWKV_HEAD)

    def bd(u, role, x):
        slot = (u["b"] * NQ + u["q"]) * len(BD_ROLES) + BD_ROLES.index(role)
        xb = x.astype(BF16)
        for h in range(QUAD):
            hs = slice(h * RWKV_HEAD, (h + 1) * RWKV_HEAD)
            bd_ref[slot, h * C:(h + 1) * C, hs] = xb[:, hs]
        return bd_ref[slot]

    def segsum(x):
        return dot(x.astype(BF16), ones_bd)

    def shift_mix(p, carry_ref, b, mu):
        prev = jnp.where(row == 0, carry_ref[b:b + 1, :], pltpu.roll(p, 1, axis=0))
        carry_ref[b:b + 1, :] = p[C - 1:C, :]
        return p + mu * (prev - p)

    par_w = lax.rem(step, 2)
    par_r = 1 - par_w
    units = [dict(b=b, q=q, ui=b * NQ + q, sl=slice(q * QW, (q + 1) * QW))
             for b in range(NB) for q in range(NQ)]

    def stash_bf(u, name):
        i = ST_BF.index(name)
        return st_bf[par_r, u["ui"], i * C:(i + 1) * C, :]

    def stash_f(u, name):
        i = ST_F.index(name)
        return st_f[par_r, u["ui"], i * C:(i + 1) * C, :]

    for u in units:
        u["S"] = state[u["ui"]]
        Sb = u["S"].astype(BF16)
        u["Z0"] = _mm_nt(stash_bf(u, "At"), Sb) + stash_f(u, "AV")
        u["YS"] = _mm_nt(stash_bf(u, "Rt"), Sb)

    for b in range(NB):
        xm = shift_mix(pm_ref[b], carry_m, b, mum_ref[...])
        xl = shift_mix(pl_ref[b], carry_l, b, mul_ref[...])
        r = xm[:, 0 * W:1 * W]
        k = xm[:, 1 * W:2 * W]
        v = xm[:, 2 * W:3 * W]
        gate = xm[:, 3 * W:4 * W]
        z = w0_ref[...] + _mm(jnp.tanh(xl), w2_ref[...])
        ld = -EXP_M05 * _sigmoid(z)
        a = _sigmoid(a0_ref[...] + _mm(xl, a2_ref[...]))
        G = _cumsum_rows(ld, tril)
        Gx = G - ld
        GC = G[C - 1:C, :]
        for u in units[b * NQ:(b + 1) * NQ]:
            sl = u["sl"]
            u.update(r=r[:, sl], k=k[:, sl], v=v[:, sl], g=gate[:, sl], a=a[:, sl],
                     G=G[:, sl], Gx=Gx[:, sl], GC=GC[:, sl])

    for u in units:
        u["kk"] = u["k"] * kk_ref[:, u["sl"]]
    for u in units:
        u["n2"] = segsum(u["kk"] * u["kk"])
    for u in units:
        sl = u["sl"]
        kkn = u["kk"] / jnp.maximum(jnp.sqrt(u["n2"]), 1e-12)
        k2 = u["k"] * (1.0 + (u["a"] - 1.0) * ka_ref[:, sl])
        av = -kkn
        bv = kkn * u["a"]
        einv = jnp.exp(-u["G"])
        eC = jnp.exp(u["GC"] - u["G"])
        u["rkk"] = u["r"] * k2 * rk_ref[:, sl]
        u["BDB"] = bd(u, "B", bv * einv)
        u["BDK"] = bd(u, "K", k2 * einv)
        u["BKe"] = jnp.concatenate([bv * eC, k2 * eC], axis=0).astype(BF16)
        u["PC"] = jnp.exp(u["GC"])
        u["BDV"] = bd(u, "V", u["v"])
        u["lhs"] = jnp.concatenate([av * jnp.exp(u["Gx"]), u["r"] * jnp.exp(u["G"])],
                                   axis=0).astype(BF16)
        u["sg"] = u["g"] * _sigmoid(u["g"])
    for u in units:
        u["SB"] = _mm_nt(u["lhs"], u["BDB"])

    for u in units:
        u["U"] = _mm(stash_bf(u, "P"), bd(u, "Z", u["Z0"]))

    for u in units:
        u["SK"] = _mm_nt(u["lhs"], u["BDK"])
    for u in units:
        u["N"] = jnp.where(strict, u["SB"][:C], 0.0)
        u["Lrb"] = jnp.where(incl, u["SB"][C:], 0.0)
        u["P"] = eye_q + u["N"]
        u["Np"] = _mm(u["N"], bd(u, "N", u["N"]))

    for u in units:
        u["Y"] = u["YS"] + _mm(stash_bf(u, "Lrb"), bd(u, "U", u["U"])) + stash_f(u, "LV")

    def neumann_level(last=False):
        for u in units:
            if last:
                u["P"] = u["P"] + _mm(u["P"], bd(u, "N", u["Np"]))
            else:
                out = _mm(jnp.concatenate([u["P"], u["Np"]], axis=0), bd(u, "N", u["Np"]))
                u["P"] = u["P"] + out[:C]
                u["Np"] = out[C:]

    neumann_level()
    for u in units:
        Aak = jnp.where(strict, u["SK"][:C], 0.0)
        Lrk = jnp.where(incl, u["SK"][C:], 0.0)
        both = _mm(jnp.concatenate([Aak, Lrk], axis=0), u["BDV"])
        u["AV"] = both[:C]
        u["LV"] = both[C:]

    for u in units:
        upd = _mm_tn(jnp.concatenate([u["U"].astype(BF16), stash_bf(u, "v")], axis=0),
                     st_bf[par_r, u["ui"], ST_BF.index("BKe") * C:, :])
        pc = st_f[par_r, u["ui"], len(ST_F) * C:len(ST_F) * C + 1, :]
        state[u["ui"]] = u["S"] * pc + jnp.where(bd_mask, upd, 0.0)

    neumann_level()
    neumann_level()

    for u in units:
        u["mean"] = segsum(u["Y"]) * (1.0 / RWKV_HEAD)
    neumann_level()
    for u in units:
        u["yc"] = u["Y"] - u["mean"]
        u["var"] = segsum(u["yc"] * u["yc"]) * (1.0 / RWKV_HEAD)
    neumann_level(last=True)
    for u in units:
        u["bonus"] = segsum(u["rkk"])
    for u in units:
        sl = u["sl"]
        yn = u["yc"] * lax.rsqrt(u["var"] + LNX_EPS) * lw_ref[:, sl] + lb_ref[:, sl]
        res = (yn + stash_f(u, "bonusv")) * stash_f(u, "sg")
        o_ref[u["b"], :, sl] = res.astype(o_ref.dtype)

    for u in units:
        ui = u["ui"]
        for name, val in (("At", u["lhs"][:C]), ("Rt", u["lhs"][C:]), ("P", u["P"]),
                          ("Lrb", u["Lrb"]), ("v", u["v"])):
            i = ST_BF.index(name)
            st_bf[par_w, ui, i * C:(i + 1) * C, :] = val.astype(BF16)
        st_bf[par_w, ui, ST_BF.index("BKe") * C:, :] = u["BKe"]
        for name, val in (("AV", u["AV"]), ("LV", u["LV"]), ("bonusv", u["bonus"] * u["v"]),
                          ("sg", u["sg"])):
            i = ST_F.index(name)
            st_f[par_w, ui, i * C:(i + 1) * C, :] = val
        st_f[par_w, ui, len(ST_F) * C:len(ST_F) * C + 1, :] = u["PC"]


def _rwkv(p3, main_blk, lora_blk, mu_m, mu_l, w0, w2p, a0, a2p, k_k, k_a, r_k, lnx_w, lnx_b, ones_bd):
    B, T, _ = p3.shape
    W = w0.shape[1]
    C = CHUNK
    NU = B * (W // QW)
    nc = T // C
    full = lambda shape: pl.BlockSpec(shape, lambda c: (0,) * len(shape))
    front = lambda c: jnp.minimum(c, nc - 1)
    back = lambda c: jnp.maximum(c - 1, 0)
    return pl.pallas_call(
        _rwkv_kernel,
        grid=(nc + 1,),
        in_specs=[
            pl.BlockSpec((B, C, 4 * W), lambda c: (0, front(c), main_blk)),
            pl.BlockSpec((B, C, LORA_BLK), lambda c: (0, front(c), lora_blk)),
            full((1, 4 * W)), full((1, LORA_BLK)),
            full((1, W)), full((LORA_BLK, W)), full((1, W)), full((LORA_BLK, W)),
            full((1, W)), full((1, W)), full((1, W)), full((1, W)), full((1, W)),
            full((QW, QW)),
        ],
        out_specs=pl.BlockSpec((B, C, W), lambda c: (0, back(c), 0)),
        out_shape=jax.ShapeDtypeStruct((B, T, W), BF16),
        scratch_shapes=[
            pltpu.VMEM((B, 4 * W), F32),
            pltpu.VMEM((B, LORA_BLK), F32),
            pltpu.VMEM((NU, QW, QW), F32),
            pltpu.VMEM((NU * len(BD_ROLES), QW, QW), BF16),
            pltpu.VMEM((2, NU, (len(ST_BF) + 1) * C, QW), BF16),
            pltpu.VMEM((2, NU, len(ST_F) * C + 8, QW), F32),
        ],
        compiler_params=pltpu.CompilerParams(
            dimension_semantics=("arbitrary",), vmem_limit_bytes=VMEM_LIMIT),
        name="rwkv7_mix",
    )(p3, p3, mu_m, mu_l, w0, w2p, a0, a2p, k_k, k_a, r_k, lnx_w, lnx_b, ones_bd)


def _hgrn_kernel(layer, off, ph_ref, lbp_ref, ng_ref, o_ref, state):
    c = pl.program_id(0)
    C = CHUNK
    NB = ph_ref.shape[0]
    W = ng_ref.shape[1]
    D = HGRN_HEAD
    NH = W // D

    @pl.when(c == 0)
    def _():
        state[...] = jnp.zeros_like(state)

    lp = lbp_ref[...]
    e = jnp.exp(lp - jnp.max(lp, axis=0, keepdims=True))
    lb = jnp.sum(e[:layer + 1], axis=0, keepdims=True) / jnp.sum(e, axis=0, keepdims=True)

    t_i = lax.broadcasted_iota(jnp.int32, (C, C), 0)
    s_i = lax.broadcasted_iota(jnp.int32, (C, C), 1)
    causal = t_i >= s_i
    tril = causal.astype(BF16)

    units = []
    for b in range(NB):
        x = ph_ref[b]
        qv = x[:, off + 0 * W:off + 1 * W]
        fr = x[:, off + 1 * W:off + 2 * W]
        iv = x[:, off + 2 * W:off + 3 * W]
        gate = x[:, off + 3 * W:off + 4 * W]
        f = lb + (1.0 - lb) * _sigmoid(fr)
        kx = 1.0 - f
        G = _cumsum_rows(jnp.log(f), tril)
        GC = G[C - 1:C, :]
        qt = (qv * jnp.exp(G)).astype(BF16)
        kb = (kx * jnp.exp(-G)).astype(BF16)
        kd = (kx * jnp.exp(GC - G)).astype(BF16)
        PC = jnp.exp(GC)
        ivb = iv.astype(BF16)
        for h in range(NH):
            sl = slice(h * D, (h + 1) * D)
            units.append(dict(b=b, h=h, sl=sl, qt=qt[:, sl], kb=kb[:, sl], kd=kd[:, sl],
                              iv=ivb[:, sl], PC=PC[:, sl], g=gate[:, sl]))

    for u in units:
        u["A"] = _mm_nt(u["qt"], u["kb"])
    for u in units:
        u["S"] = state[u["b"] * NH + u["h"]]
        u["inter"] = _mm_nt(u["qt"], u["S"])
    for u in units:
        u["o"] = _mm(jnp.where(causal, u["A"], 0.0), u["iv"]) + u["inter"]
    for u in units:
        state[u["b"] * NH + u["h"]] = u["S"] * u["PC"] + _mm_tn(u["iv"], u["kd"])
    for u in units:
        o = u["o"]
        g = u["g"]
        ms = jnp.mean(o * o, axis=-1, keepdims=True)
        res = o * lax.rsqrt(ms + NORM_EPS) * ng_ref[:, u["sl"]] * (g * _sigmoid(g))
        o_ref[u["b"], :, u["sl"]] = res.astype(o_ref.dtype)


def _hgrn(p3, blk_w, blk, off, lb_param, norm_g, layer):
    B, T, _ = p3.shape
    W = norm_g.shape[1]
    C = CHUNK
    L = lb_param.shape[0]
    return pl.pallas_call(
        functools.partial(_hgrn_kernel, layer, off),
        grid=(T // C,),
        in_specs=[
            pl.BlockSpec((B, C, blk_w), lambda c: (0, c, blk)),
            pl.BlockSpec((L, W), lambda c: (0, 0)),
            pl.BlockSpec((1, W), lambda c: (0, 0)),
        ],
        out_specs=pl.BlockSpec((B, C, W), lambda c: (0, c, 0)),
        out_shape=jax.ShapeDtypeStruct((B, T, W), BF16),
        scratch_shapes=[pltpu.VMEM((B * (W // HGRN_HEAD), HGRN_HEAD, HGRN_HEAD), F32)],
        compiler_params=pltpu.CompilerParams(
            dimension_semantics=("arbitrary",), vmem_limit_bytes=VMEM_LIMIT),
        name="hgrn2_mix",
    )(p3, lb_param, norm_g)


def _outproj_kernel(final, yr_ref, yh_ref, x_ref, wr_ref, wh_ref, g_ref, o_ref):
    acc = jnp.dot(yr_ref[...], wr_ref[...], preferred_element_type=F32)
    acc += jnp.dot(yh_ref[...], wh_ref[...], preferred_element_type=F32)
    h = x_ref[...] + acc
    if final:
        ms = jnp.mean(h * h, axis=-1, keepdims=True)
        h = h * lax.rsqrt(ms + NORM_EPS) * g_ref[...]
    o_ref[...] = h


def _outproj(yr, yh, x2, wr, wh, g, tm, final):
    m, d = x2.shape
    wr_w = yr.shape[1]
    wh_w = yh.shape[1]
    return pl.pallas_call(
        functools.partial(_outproj_kernel, final),
        grid=(m // tm,),
        in_specs=[
            pl.BlockSpec((tm, wr_w), lambda i: (i, 0)),
            pl.BlockSpec((tm, wh_w), lambda i: (i, 0)),
            pl.BlockSpec((tm, d), lambda i: (i, 0)),
            pl.BlockSpec((wr_w, d), lambda i: (0, 0)),
            pl.BlockSpec((wh_w, d), lambda i: (0, 0)),
            pl.BlockSpec((1, d), lambda i: (0, 0)),
        ],
        out_specs=pl.BlockSpec((tm, d), lambda i: (i, 0)),
        out_shape=jax.ShapeDtypeStruct((m, d), F32),
        compiler_params=pltpu.CompilerParams(
            dimension_semantics=("arbitrary",), vmem_limit_bytes=VMEM_LIMIT),
        name="outproj",
    )(yr, yh, x2, wr, wh, g)


def kernel(x, norm_g, w_in, mu, w0, w2, a0, a2, k_k, k_a, r_k, lnx_w, lnx_b,
           hgrn_norm_g, lb_param, w_out, final_g):
    B, T, D = x.shape
    depth = w_in.shape[0]
    RW = w0.shape[1]
    HW = hgrn_norm_g.shape[1]
    n_r = 4 * RW + 2 * LORA
    tn = 768
    n_p = pl.cdiv(w_in.shape[2], tn) * tn
    h_start = (n_r // LANE) * LANE
    h_off = n_r - h_start
    h_blk_w = n_p - h_start
    assert (4 * RW) % LORA_BLK == 0 and 2 * LORA <= LORA_BLK and w2.shape[1] == LORA
    assert h_start % h_blk_w == 0 and h_off + 4 * HW <= h_blk_w

    row = lambda z: z.reshape(1, -1).astype(F32)
    zr = lambda rows: jnp.zeros((rows, RW), F32)
    ones_bd = (jnp.arange(QW)[:, None] // RWKV_HEAD == jnp.arange(QW)[None, :] // RWKV_HEAD).astype(BF16)

    w_in_t = jnp.swapaxes(w_in, 1, 2)
    h = x.reshape(B * T, D)
    for l in range(depth):
        mu_l = mu[l].astype(F32)
        mu_main = mu_l[:4 * RW].reshape(1, -1)
        mu_lora = jnp.concatenate(
            [mu_l[4 * RW:], jnp.zeros((LORA_BLK - 2 * LORA,), F32)]).reshape(1, -1)
        w2p = jnp.concatenate([w2[l].astype(F32), zr(LORA_BLK - LORA)], axis=0).astype(BF16)
        a2p = jnp.concatenate([zr(LORA), a2[l].astype(F32), zr(LORA_BLK - 2 * LORA)],
                              axis=0).astype(BF16)

        p = _inproj(h, row(norm_g[l]), w_in_t, l, tm=1024, tn=tn)
        p3 = p.reshape(B, T, n_p)
        y_r = _rwkv(p3, 0, (4 * RW) // LORA_BLK, mu_main, mu_lora,
                    row(w0[l]), w2p, row(a0[l]), a2p, row(k_k[l]), row(k_a[l]), row(r_k[l]),
                    row(lnx_w[l]), row(lnx_b[l]), ones_bd)
        y_h = _hgrn(p3, h_blk_w, h_start // h_blk_w, h_off, lb_param.astype(F32),
                    row(hgrn_norm_g[l]), l)
        wo = w_out[l].astype(BF16)
        h = _outproj(y_r.reshape(B * T, RW), y_h.reshape(B * T, HW), h,
                     wo[:RW], wo[RW:], row(final_g), tm=256, final=(l == depth - 1))
    return h.reshape(B, T, D)
```

```python
import functools
import math

import jax
import jax.numpy as jnp
from jax import lax
from jax.experimental import pallas as pl
from jax.experimental.pallas import tpu as pltpu

F32 = jnp.float32
BF16 = jnp.bfloat16

NORM_EPS = 1e-6
LNX_EPS = 64e-5
RWKV_HEAD = 64
HGRN_HEAD = 128
LORA = 96
LORA_BLK = 256
LANE = 128
CHUNK = 64
QUAD = 4
QW = QUAD * RWKV_HEAD
BD_ROLES = ("B", "K", "V", "N", "Z", "U")
EXP_M05 = math.exp(-0.5)
VMEM_LIMIT = 48 * 1024 * 1024


def _mm(a, b):
    return jnp.dot(a.astype(BF16), b.astype(BF16), preferred_element_type=F32)


def _mm_nt(a, b):
    return lax.dot_general(a.astype(BF16), b.astype(BF16), (((1,), (1,)), ((), ())),
                           preferred_element_type=F32)


def _mm_tn(a, b):
    return lax.dot_general(a.astype(BF16), b.astype(BF16), (((0,), (0,)), ((), ())),
                           preferred_element_type=F32)


def _split3(x):
    hi = x.astype(BF16)
    r1 = x - hi.astype(F32)
    mid = r1.astype(BF16)
    lo = (r1 - mid.astype(F32)).astype(BF16)
    return hi, mid, lo


def _cumsum_rows(x, tril_bf16):
    hi, mid, lo = _split3(x)
    dot = functools.partial(jnp.dot, preferred_element_type=F32)
    return dot(tril_bf16, hi) + dot(tril_bf16, mid) + dot(tril_bf16, lo)


def _sigmoid(x):
    return 1.0 / (1.0 + jnp.exp(-x))


def _inproj_kernel(x_ref, g_ref, w_ref, o_ref, hn_ref):
    @pl.when(pl.program_id(1) == 0)
    def _():
        x = x_ref[...]
        ms = jnp.mean(x * x, axis=-1, keepdims=True)
        hn_ref[...] = (x * lax.rsqrt(ms + NORM_EPS) * g_ref[...]).astype(BF16)

    o_ref[...] = _mm_nt(hn_ref[...], w_ref[...])


def _inproj(x2, g, wt_all, layer, tm, tn):
    m, d = x2.shape
    n = pl.cdiv(wt_all.shape[1], tn) * tn
    return pl.pallas_call(
        _inproj_kernel,
        grid=(m // tm, n // tn),
        in_specs=[
            pl.BlockSpec((tm, d), lambda i, j: (i, 0)),
            pl.BlockSpec((1, d), lambda i, j: (0, 0)),
            pl.BlockSpec((None, tn, d), lambda i, j: (layer, j, 0)),
        ],
        out_specs=pl.BlockSpec((tm, tn), lambda i, j: (i, j)),
        out_shape=jax.ShapeDtypeStruct((m, n), F32),
        scratch_shapes=[pltpu.VMEM((tm, d), BF16)],
        compiler_params=pltpu.CompilerParams(
            dimension_semantics=("arbitrary", "arbitrary"), vmem_limit_bytes=VMEM_LIMIT),
        name="inproj",
    )(x2, g, wt_all)


def _rwkv_kernel(pm_ref, pl_ref, mum_ref, mul_ref, w0_ref, w2_ref, a0_ref, a2_ref,
                 kk_ref, ka_ref, rk_ref, lw_ref, lb_ref, ones_ref,
                 o_ref, carry_m, carry_l, state, bd_ref):
    c = pl.program_id(0)
    C = CHUNK
    NB = pm_ref.shape[0]
    W = w0_ref.shape[1]
    NQ = W // QW

    @pl.when(c == 0)
    def _():
        for ref in (carry_m, carry_l, state, bd_ref):
            ref[...] = jnp.zeros_like(ref)

    row = lax.broadcasted_iota(jnp.int32, (C, 1), 0)
    t_i = lax.broadcasted_iota(jnp.int32, (C, C), 0)
    s_i = lax.broadcasted_iota(jnp.int32, (C, C), 1)
    tril = (t_i >= s_i).astype(BF16)
    ones_bd = ones_ref[...]
    dot = functools.partial(jnp.dot, preferred_element_type=F32)

    lane = lax.broadcasted_iota(jnp.int32, (C, QW), 1)
    lane_s = lane % RWKV_HEAD
    trow = lax.broadcasted_iota(jnp.int32, (C, QW), 0)
    strict = lane_s < trow
    incl = lane_s <= trow
    eye_q = jnp.where(lane_s == trow, 1.0, 0.0).astype(F32)
    bd_mask = (lax.broadcasted_iota(jnp.int32, (QW, QW), 0) // RWKV_HEAD
               == lax.broadcasted_iota(jnp.int32, (QW, QW), 1) // RWKV_HEAD)

    def bd(u, role, x):
        slot = u["ui"] * len(BD_ROLES) + BD_ROLES.index(role)
        xb = x.astype(BF16)
        for h in range(QUAD):
            hs = slice(h * RWKV_HEAD, (h + 1) * RWKV_HEAD)
            bd_ref[slot, h * C:(h + 1) * C, hs] = xb[:, hs]
        return bd_ref[slot]

    def segsum_all(xs):
        out = dot(jnp.concatenate([x.astype(BF16) for x in xs], axis=0), ones_bd)
        return [out[i * C:(i + 1) * C] for i in range(len(xs))]

    def shift_mix(p, carry_ref, b, mu):
        prev = jnp.where(row == 0, carry_ref[b:b + 1, :], pltpu.roll(p, 1, axis=0))
        carry_ref[b:b + 1, :] = p[C - 1:C, :]
        return p + mu * (prev - p)

    xms = [shift_mix(pm_ref[b], carry_m, b, mum_ref[...]) for b in range(NB)]
    xl = jnp.concatenate([shift_mix(pl_ref[b], carry_l, b, mul_ref[...]) for b in range(NB)], axis=0)
    z = w0_ref[...] + _mm(jnp.tanh(xl[:, :LANE]), w2_ref[...])
    ld_all = -EXP_M05 * _sigmoid(z)
    a_all = _sigmoid(a0_ref[...] + _mm(xl, a2_ref[...]))
    units = []
    for b in range(NB):
        xm = xms[b]
        r = xm[:, 0 * W:1 * W]
        k = xm[:, 1 * W:2 * W]
        v = xm[:, 2 * W:3 * W]
        gate = xm[:, 3 * W:4 * W]
        ld = ld_all[b * C:(b + 1) * C]
        a = a_all[b * C:(b + 1) * C]
        G = _cumsum_rows(ld, tril)
        Gx = G - ld
        GC = G[C - 1:C, :]
        for q in range(NQ):
            sl = slice(q * QW, (q + 1) * QW)
            units.append(dict(b=b, ui=b * NQ + q, sl=sl, r=r[:, sl], k=k[:, sl], v=v[:, sl],
                              g=gate[:, sl], a=a[:, sl], G=G[:, sl], Gx=Gx[:, sl], GC=GC[:, sl]))

    for u in units:
        u["kk"] = u["k"] * kk_ref[:, u["sl"]]
    for u, n2 in zip(units, segsum_all([u["kk"] * u["kk"] for u in units])):
        u["n2"] = n2
    for u in units:
        sl = u["sl"]
        kkn = u["kk"] / jnp.maximum(jnp.sqrt(u["n2"]), 1e-12)
        k2 = u["k"] * (1.0 + (u["a"] - 1.0) * ka_ref[:, sl])
        av = -kkn
        bv = kkn * u["a"]
        einv = jnp.exp(-u["G"])
        eC = jnp.exp(u["GC"] - u["G"])
        u["rkk"] = u["r"] * k2 * rk_ref[:, sl]
        u["BDB"] = bd(u, "B", bv * einv)
        u["BDK"] = bd(u, "K", k2 * einv)
        u["BKe"] = jnp.concatenate([bv * eC, k2 * eC], axis=0).astype(BF16)
        u["BDV"] = bd(u, "V", u["v"])
        u["lhs"] = jnp.concatenate([av * jnp.exp(u["Gx"]), u["r"] * jnp.exp(u["G"])],
                                   axis=0).astype(BF16)
        pc_col = jnp.transpose(jnp.broadcast_to(jnp.exp(u["GC"]), (LANE, QW)))
        u["PCc"] = jnp.concatenate([pc_col] * (QW // LANE), axis=1)

    for u in units:
        u["SB"] = _mm_nt(u["lhs"], u["BDB"])
    for u in units:
        u["SK"] = _mm_nt(u["lhs"], u["BDK"])
    for u in units:
        u["N"] = jnp.where(strict, u["SB"][:C], 0.0)
        u["Lrb"] = jnp.where(incl, u["SB"][C:], 0.0)

    for u in units:
        u["P"] = eye_q + u["N"]
        u["Np"] = _mm(u["N"], bd(u, "N", u["N"]))
    for u in units:
        akl = jnp.concatenate([jnp.where(strict, u["SK"][:C], 0.0),
                               jnp.where(incl, u["SK"][C:], 0.0)], axis=0)
        both = _mm(akl, u["BDV"])
        u["AV"] = both[:C]
        u["LV"] = both[C:]
    for _ in range(4):
        for u in units:
            out = _mm(jnp.concatenate([u["P"], u["Np"]], axis=0), bd(u, "N", u["Np"]))
            u["P"] = u["P"] + out[:C]
            u["Np"] = out[C:]
    for u in units:
        u["P"] = u["P"] + _mm(u["P"], bd(u, "N", u["Np"]))

    for u in units:
        u["H"] = state[u["ui"]]
        zy = _mm(u["lhs"], u["H"])
        u["Z0"] = zy[:C] + u["AV"]
        u["YS"] = zy[C:]
    for u in units:
        u["U"] = _mm(u["P"], bd(u, "Z", u["Z0"]))
    for u in units:
        u["Y"] = u["YS"] + _mm(u["Lrb"], bd(u, "U", u["U"])) + u["LV"]
    for u in units:
        upd = _mm_tn(u["BKe"], jnp.concatenate([u["U"], u["v"]], axis=0))
        state[u["ui"]] = u["H"] * u["PCc"] + jnp.where(bd_mask, upd, 0.0)

    for u, s in zip(units, segsum_all([u["Y"] for u in units])):
        u["yc"] = u["Y"] - s * (1.0 / RWKV_HEAD)
    for u, s in zip(units, segsum_all([u["yc"] * u["yc"] for u in units])):
        u["var"] = s * (1.0 / RWKV_HEAD)
    for u, s in zip(units, segsum_all([u["rkk"] for u in units])):
        u["bonus"] = s
    for u in units:
        sl = u["sl"]
        yn = u["yc"] * lax.rsqrt(u["var"] + LNX_EPS) * lw_ref[:, sl] + lb_ref[:, sl]
        res = (yn + u["bonus"] * u["v"]) * (u["g"] * _sigmoid(u["g"]))
        o_ref[u["b"], :, sl] = res.astype(o_ref.dtype)


def _rwkv(p3, main_blk, lora_blk, mu_m, mu_l, w0, w2p, a0, a2p, k_k, k_a, r_k, lnx_w, lnx_b, ones_bd):
    B, T, _ = p3.shape
    W = w0.shape[1]
    C = CHUNK
    NU = B * (W // QW)
    full = lambda shape: pl.BlockSpec(shape, lambda c: (0,) * len(shape))
    return pl.pallas_call(
        _rwkv_kernel,
        grid=(T // C,),
        in_specs=[
            pl.BlockSpec((B, C, 4 * W), lambda c: (0, c, main_blk)),
            pl.BlockSpec((B, C, LORA_BLK), lambda c: (0, c, lora_blk)),
            full((1, 4 * W)), full((1, LORA_BLK)),
            full((1, W)), full((LANE, W)), full((1, W)), full((LORA_BLK, W)),
            full((1, W)), full((1, W)), full((1, W)), full((1, W)), full((1, W)),
            full((QW, QW)),
        ],
        out_specs=pl.BlockSpec((B, C, W), lambda c: (0, c, 0)),
        out_shape=jax.ShapeDtypeStruct((B, T, W), BF16),
        scratch_shapes=[
            pltpu.VMEM((B, 4 * W), F32),
            pltpu.VMEM((B, LORA_BLK), F32),
            pltpu.VMEM((NU, QW, QW), F32),
            pltpu.VMEM((NU * len(BD_ROLES), QW, QW), BF16),
        ],
        compiler_params=pltpu.CompilerParams(
            dimension_semantics=("arbitrary",), vmem_limit_bytes=VMEM_LIMIT),
        name="rwkv7_mix",
    )(p3, p3, mu_m, mu_l, w0, w2p, a0, a2p, k_k, k_a, r_k, lnx_w, lnx_b, ones_bd)


def _hgrn_kernel(layer, off, ph_ref, lbp_ref, ng_ref, o_ref, state):
    c = pl.program_id(0)
    C = CHUNK
    NB = ph_ref.shape[0]
    W = ng_ref.shape[1]
    D = HGRN_HEAD
    NH = W // D

    @pl.when(c == 0)
    def _():
        state[...] = jnp.zeros_like(state)

    lp = lbp_ref[...]
    e = jnp.exp(lp - jnp.max(lp, axis=0, keepdims=True))
    lb = jnp.sum(e[:layer + 1], axis=0, keepdims=True) / jnp.sum(e, axis=0, keepdims=True)

    t_i = lax.broadcasted_iota(jnp.int32, (C, C), 0)
    s_i = lax.broadcasted_iota(jnp.int32, (C, C), 1)
    causal = t_i >= s_i
    tril = causal.astype(BF16)

    units = []
    for b in range(NB):
        x = ph_ref[b]
        qv = x[:, off + 0 * W:off + 1 * W]
        fr = x[:, off + 1 * W:off + 2 * W]
        iv = x[:, off + 2 * W:off + 3 * W]
        gate = x[:, off + 3 * W:off + 4 * W]
        f = lb + (1.0 - lb) * _sigmoid(fr)
        kx = 1.0 - f
        G = _cumsum_rows(jnp.log(f), tril)
        GC = G[C - 1:C, :]
        qt = (qv * jnp.exp(G)).astype(BF16)
        kb = (kx * jnp.exp(-G)).astype(BF16)
        kd = (kx * jnp.exp(GC - G)).astype(BF16)
        PC = jnp.exp(GC)
        ivb = iv.astype(BF16)
        for h in range(NH):
            sl = slice(h * D, (h + 1) * D)
            units.append(dict(b=b, h=h, sl=sl, qt=qt[:, sl], kb=kb[:, sl], kd=kd[:, sl],
                              iv=ivb[:, sl], PC=PC[:, sl], g=gate[:, sl]))

    for u in units:
        u["A"] = _mm_nt(u["qt"], u["kb"])
    for u in units:
        u["S"] = state[u["b"] * NH + u["h"]]
        u["inter"] = _mm_nt(u["qt"], u["S"])
    for u in units:
        u["o"] = _mm(jnp.where(causal, u["A"], 0.0), u["iv"]) + u["inter"]
    for u in units:
        state[u["b"] * NH + u["h"]] = u["S"] * u["PC"] + _mm_tn(u["iv"], u["kd"])
    for u in units:
        o = u["o"]
        g = u["g"]
        ms = jnp.mean(o * o, axis=-1, keepdims=True)
        res = o * lax.rsqrt(ms + NORM_EPS) * ng_ref[:, u["sl"]] * (g * _sigmoid(g))
        o_ref[u["b"], :, u["sl"]] = res.astype(o_ref.dtype)


def _hgrn(p3, blk_w, blk, off, lb_param, norm_g, layer):
    B, T, _ = p3.shape
    W = norm_g.shape[1]
    C = CHUNK
    L = lb_param.shape[0]
    return pl.pallas_call(
        functools.partial(_hgrn_kernel, layer, off),
        grid=(T // C,),
        in_specs=[
            pl.BlockSpec((B, C, blk_w), lambda c: (0, c, blk)),
            pl.BlockSpec((L, W), lambda c: (0, 0)),
            pl.BlockSpec((1, W), lambda c: (0, 0)),
        ],
        out_specs=pl.BlockSpec((B, C, W), lambda c: (0, c, 0)),
        out_shape=jax.ShapeDtypeStruct((B, T, W), BF16),
        scratch_shapes=[pltpu.VMEM((B * (W // HGRN_HEAD), HGRN_HEAD, HGRN_HEAD), F32)],
        compiler_params=pltpu.CompilerParams(
            dimension_semantics=("arbitrary",), vmem_limit_bytes=VMEM_LIMIT),
        name="hgrn2_mix",
    )(p3, lb_param, norm_g)


def _outproj_kernel(final, yr_ref, yh_ref, x_ref, wr_ref, wh_ref, g_ref, o_ref):
    acc = jnp.dot(yr_ref[...], wr_ref[...], preferred_element_type=F32)
    acc += jnp.dot(yh_ref[...], wh_ref[...], preferred_element_type=F32)
    h = x_ref[...] + acc
    if final:
        ms = jnp.mean(h * h, axis=-1, keepdims=True)
        h = h * lax.rsqrt(ms + NORM_EPS) * g_ref[...]
    o_ref[...] = h


def _outproj(yr, yh, x2, wr, wh, g, tm, final):
    m, d = x2.shape
    wr_w = yr.shape[1]
    wh_w = yh.shape[1]
    return pl.pallas_call(
        functools.partial(_outproj_kernel, final),
        grid=(m // tm,),
        in_specs=[
            pl.BlockSpec((tm, wr_w), lambda i: (i, 0)),
            pl.BlockSpec((tm, wh_w), lambda i: (i, 0)),
            pl.BlockSpec((tm, d), lambda i: (i, 0)),
            pl.BlockSpec((wr_w, d), lambda i: (0, 0)),
            pl.BlockSpec((wh_w, d), lambda i: (0, 0)),
            pl.BlockSpec((1, d), lambda i: (0, 0)),
        ],
        out_specs=pl.BlockSpec((tm, d), lambda i: (i, 0)),
        out_shape=jax.ShapeDtypeStruct((m, d), F32),
        compiler_params=pltpu.CompilerParams(
            dimension_semantics=("arbitrary",), vmem_limit_bytes=VMEM_LIMIT),
        name="outproj",
    )(yr, yh, x2, wr, wh, g)


def kernel(x, norm_g, w_in, mu, w0, w2, a0, a2, k_k, k_a, r_k, lnx_w, lnx_b,
           hgrn_norm_g, lb_param, w_out, final_g):
    B, T, D = x.shape
    depth = w_in.shape[0]
    RW = w0.shape[1]
    HW = hgrn_norm_g.shape[1]
    n_r = 4 * RW + 2 * LORA
    tn = 768
    n_p = pl.cdiv(w_in.shape[2], tn) * tn
    h_start = (n_r // LANE) * LANE
    h_off = n_r - h_start
    h_blk_w = n_p - h_start
    assert (4 * RW) % LORA_BLK == 0 and 2 * LORA <= LORA_BLK and w2.shape[1] == LORA <= LANE
    assert h_start % h_blk_w == 0 and h_off + 4 * HW <= h_blk_w

    row = lambda z: z.reshape(1, -1).astype(F32)
    zr = lambda rows: jnp.zeros((rows, RW), F32)
    ones_bd = (jnp.arange(QW)[:, None] // RWKV_HEAD == jnp.arange(QW)[None, :] // RWKV_HEAD).astype(BF16)

    w_in_t = jnp.swapaxes(w_in, 1, 2)
    h = x.reshape(B * T, D)
    for l in range(depth):
        mu_l = mu[l].astype(F32)
        mu_main = mu_l[:4 * RW].reshape(1, -1)
        mu_lora = jnp.concatenate(
            [mu_l[4 * RW:], jnp.zeros((LORA_BLK - 2 * LORA,), F32)]).reshape(1, -1)
        w2p = jnp.concatenate([w2[l].astype(F32), zr(LANE - LORA)], axis=0).astype(BF16)
        a2p = jnp.concatenate([zr(LORA), a2[l].astype(F32), zr(LORA_BLK - 2 * LORA)],
                              axis=0).astype(BF16)

        p = _inproj(h, row(norm_g[l]), w_in_t, l, tm=1024, tn=tn)
        p3 = p.reshape(B, T, n_p)
        y_r = _rwkv(p3, 0, (4 * RW) // LORA_BLK, mu_main, mu_lora,
                    row(w0[l]), w2p, row(a0[l]), a2p, row(k_k[l]), row(k_a[l]), row(r_k[l]),
                    row(lnx_w[l]), row(lnx_b[l]), ones_bd)
        y_h = _hgrn(p3, h_blk_w, h_start // h_blk_w, h_off, lb_param.astype(F32),
                    row(hgrn_norm_g[l]), l)
        wo = w_out[l].astype(BF16)
        h = _outproj(y_r.reshape(B * T, RW), y_h.reshape(B * T, HW), h,
                     wo[:RW], wo[RW:], row(final_g), tm=256, final=(l == depth - 1))
    return h.reshape(B, T, D)
```

```python
import functools
import math

import jax
import jax.numpy as jnp
from jax import lax
from jax.experimental import pallas as pl
from jax.experimental.pallas import tpu as pltpu

F32 = jnp.float32
BF16 = jnp.bfloat16

NORM_EPS = 1e-6
LNX_EPS = 64e-5
RWKV_HEAD = 64
HGRN_HEAD = 128
LORA = 96
LORA_BLK = 256
LANE = 128
CHUNK = 64
QUAD = 4
QW = QUAD * RWKV_HEAD
BD_ROLES = ("B", "K", "V", "N", "Z", "U")
EXP_M05 = math.exp(-0.5)
VMEM_LIMIT = 48 * 1024 * 1024


def _mm(a, b):
    return jnp.dot(a.astype(BF16), b.astype(BF16), preferred_element_type=F32)


def _mm_nt(a, b):
    return lax.dot_general(a.astype(BF16), b.astype(BF16), (((1,), (1,)), ((), ())),
                           preferred_element_type=F32)


def _mm_tn(a, b):
    return lax.dot_general(a.astype(BF16), b.astype(BF16), (((0,), (0,)), ((), ())),
                           preferred_element_type=F32)


def _split3(x):
    hi = x.astype(BF16)
    r1 = x - hi.astype(F32)
    mid = r1.astype(BF16)
    lo = (r1 - mid.astype(F32)).astype(BF16)
    return hi, mid, lo


def _cumsum_rows(x, tril_bf16):
    hi, mid, lo = _split3(x)
    dot = functools.partial(jnp.dot, preferred_element_type=F32)
    return dot(tril_bf16, hi) + dot(tril_bf16, mid) + dot(tril_bf16, lo)


def _sigmoid(x):
    return 1.0 / (1.0 + jnp.exp(-x))


def _inproj_kernel(n_valid, x_ref, g_ref, w_ref, o_ref, hn_ref):
    j = pl.program_id(1)
    i = pl.program_id(2)
    tn = w_ref.shape[0]

    @pl.when(j == 0)
    def _():
        x = x_ref[...]
        ms = jnp.mean(x * x, axis=-1, keepdims=True)
        hn_ref[i] = (x * lax.rsqrt(ms + NORM_EPS) * g_ref[...]).astype(BF16)

    w_row = j * tn + lax.broadcasted_iota(jnp.int32, (tn, 1), 0)
    w = jnp.where(w_row < n_valid, w_ref[...], 0.0)
    o_ref[...] = _mm_nt(hn_ref[i], w)


def _inproj(x2, g, wt_all, layer, tm, tn, groups):
    m, d = x2.shape
    n_valid = wt_all.shape[1]
    n = pl.cdiv(n_valid, tn) * tn
    ni = m // (groups * tm)
    x_map = lambda q, j, i: (q * ni + jnp.where(j == 0, i, ni - 1), 0)
    return pl.pallas_call(
        functools.partial(_inproj_kernel, n_valid),
        grid=(groups, n // tn, ni),
        in_specs=[
            pl.BlockSpec((tm, d), x_map),
            pl.BlockSpec((1, d), lambda q, j, i: (0, 0)),
            pl.BlockSpec((None, tn, d), lambda q, j, i: (layer, j, 0)),
        ],
        out_specs=pl.BlockSpec((tm, tn), lambda q, j, i: (q * ni + i, j)),
        out_shape=jax.ShapeDtypeStruct((m, n), F32),
        scratch_shapes=[pltpu.VMEM((ni, tm, d), BF16)],
        compiler_params=pltpu.CompilerParams(
            dimension_semantics=("arbitrary", "arbitrary", "arbitrary"),
            vmem_limit_bytes=VMEM_LIMIT),
        name="inproj",
    )(x2, g, wt_all)


def _rwkv_kernel(pm_ref, pl_ref, mum_ref, mul_ref, w0_ref, w2_ref, a0_ref, a2_ref,
                 kk_ref, ka_ref, rk_ref, lw_ref, lb_ref, ones_ref,
                 o_ref, carry_m, carry_l, state, bd_ref):
    c = pl.program_id(0)
    C = CHUNK
    NB = pm_ref.shape[0]
    W = w0_ref.shape[1]
    NQ = W // QW

    @pl.when(c == 0)
    def _():
        for ref in (carry_m, carry_l, state, bd_ref):
            ref[...] = jnp.zeros_like(ref)

    row = lax.broadcasted_iota(jnp.int32, (C, 1), 0)
    t_i = lax.broadcasted_iota(jnp.int32, (C, C), 0)
    s_i = lax.broadcasted_iota(jnp.int32, (C, C), 1)
    tril = (t_i >= s_i).astype(BF16)
    ones_bd = ones_ref[...]
    dot = functools.partial(jnp.dot, preferred_element_type=F32)

    lane = lax.broadcasted_iota(jnp.int32, (C, QW), 1)
    lane_s = lane % RWKV_HEAD
    trow = lax.broadcasted_iota(jnp.int32, (C, QW), 0)
    strict = lane_s < trow
    incl = lane_s <= trow
    eye_q = jnp.where(lane_s == trow, 1.0, 0.0).astype(F32)
    bd_mask = (lax.broadcasted_iota(jnp.int32, (QW, QW), 0) // RWKV_HEAD
               == lax.broadcasted_iota(jnp.int32, (QW, QW), 1) // RWKV_HEAD)

    def bd(u, role, x):
        slot = u["ui"] * len(BD_ROLES) + BD_ROLES.index(role)
        xb = x.astype(BF16)
        for h in range(QUAD):
            hs = slice(h * RWKV_HEAD, (h + 1) * RWKV_HEAD)
            bd_ref[slot, h * C:(h + 1) * C, hs] = xb[:, hs]
        return bd_ref[slot]

    def segsum_all(xs):
        out = dot(jnp.concatenate([x.astype(BF16) for x in xs], axis=0), ones_bd)
        return [out[i * C:(i + 1) * C] for i in range(len(xs))]

    def shift_mix(p, carry_ref, b, mu):
        prev = jnp.where(row == 0, carry_ref[b:b + 1, :], pltpu.roll(p, 1, axis=0))
        carry_ref[b:b + 1, :] = p[C - 1:C, :]
        return p + mu * (prev - p)

    xms = [shift_mix(pm_ref[b], carry_m, b, mum_ref[...]) for b in range(NB)]
    xl = jnp.concatenate([shift_mix(pl_ref[b], carry_l, b, mul_ref[...]) for b in range(NB)], axis=0)
    z = w0_ref[...] + _mm(jnp.tanh(xl[:, :LANE]), w2_ref[...])
    ld_all = -EXP_M05 * _sigmoid(z)
    a_all = _sigmoid(a0_ref[...] + _mm(xl, a2_ref[...]))
    units = []
    for b in range(NB):
        xm = xms[b]
        r = xm[:, 0 * W:1 * W]
        k = xm[:, 1 * W:2 * W]
        v = xm[:, 2 * W:3 * W]
        gate = xm[:, 3 * W:4 * W]
        ld = ld_all[b * C:(b + 1) * C]
        a = a_all[b * C:(b + 1) * C]
        G = _cumsum_rows(ld, tril)
        Gx = G - ld
        GC = G[C - 1:C, :]
        for q in range(NQ):
            sl = slice(q * QW, (q + 1) * QW)
            units.append(dict(b=b, ui=b * NQ + q, sl=sl, r=r[:, sl], k=k[:, sl], v=v[:, sl],
                              g=gate[:, sl], a=a[:, sl], G=G[:, sl], Gx=Gx[:, sl], GC=GC[:, sl]))

    for u in units:
        u["kk"] = u["k"] * kk_ref[:, u["sl"]]
    for u, n2 in zip(units, segsum_all([u["kk"] * u["kk"] for u in units])):
        u["n2"] = n2
    for u in units:
        sl = u["sl"]
        kkn = u["kk"] / jnp.maximum(jnp.sqrt(u["n2"]), 1e-12)
        k2 = u["k"] * (1.0 + (u["a"] - 1.0) * ka_ref[:, sl])
        av = -kkn
        bv = kkn * u["a"]
        einv = jnp.exp(-u["G"])
        eC = jnp.exp(u["GC"] - u["G"])
        u["rkk"] = u["r"] * k2 * rk_ref[:, sl]
        u["BDB"] = bd(u, "B", bv * einv)
        u["BDK"] = bd(u, "K", k2 * einv)
        u["BKe"] = jnp.concatenate([bv * eC, k2 * eC], axis=0).astype(BF16)
        u["BDV"] = bd(u, "V", u["v"])
        u["lhs"] = jnp.concatenate([av * jnp.exp(u["Gx"]), u["r"] * jnp.exp(u["G"])],
                                   axis=0).astype(BF16)
        pc_col = jnp.transpose(jnp.broadcast_to(jnp.exp(u["GC"]), (LANE, QW)))
        u["PCc"] = jnp.concatenate([pc_col] * (QW // LANE), axis=1)

    for u in units:
        u["SB"] = _mm_nt(u["lhs"], u["BDB"])
    for u in units:
        u["SK"] = _mm_nt(u["lhs"], u["BDK"])
    for u in units:
        u["N"] = jnp.where(strict, u["SB"][:C], 0.0)
        u["Lrb"] = jnp.where(incl, u["SB"][C:], 0.0)

    for u in units:
        u["P"] = eye_q + u["N"]
        u["Np"] = _mm(u["N"], bd(u, "N", u["N"]))
    for u in units:
        akl = jnp.concatenate([jnp.where(strict, u["SK"][:C], 0.0),
                               jnp.where(incl, u["SK"][C:], 0.0)], axis=0)
        both = _mm(akl, u["BDV"])
        u["AV"] = both[:C]
        u["LV"] = both[C:]
    for _ in range(4):
        for u in units:
            out = _mm(jnp.concatenate([u["P"], u["Np"]], axis=0), bd(u, "N", u["Np"]))
            u["P"] = u["P"] + out[:C]
            u["Np"] = out[C:]
    for u in units:
        u["P"] = u["P"] + _mm(u["P"], bd(u, "N", u["Np"]))

    for u in units:
        u["H"] = state[u["ui"]]
        zy = _mm(u["lhs"], u["H"])
        u["Z0"] = zy[:C] + u["AV"]
        u["YS"] = zy[C:]
    for u in units:
        u["U"] = _mm(u["P"], bd(u, "Z", u["Z0"]))
    for u in units:
        u["Y"] = u["YS"] + _mm(u["Lrb"], bd(u, "U", u["U"])) + u["LV"]
    for u in units:
        upd = _mm_tn(u["BKe"], jnp.concatenate([u["U"], u["v"]], axis=0))
        state[u["ui"]] = u["H"] * u["PCc"] + jnp.where(bd_mask, upd, 0.0)

    for u, s in zip(units, segsum_all([u["Y"] for u in units])):
        u["yc"] = u["Y"] - s * (1.0 / RWKV_HEAD)
    for u, s in zip(units, segsum_all([u["yc"] * u["yc"] for u in units])):
        u["var"] = s * (1.0 / RWKV_HEAD)
    for u, s in zip(units, segsum_all([u["rkk"] for u in units])):
        u["bonus"] = s
    for u in units:
        sl = u["sl"]
        yn = u["yc"] * lax.rsqrt(u["var"] + LNX_EPS) * lw_ref[:, sl] + lb_ref[:, sl]
        res = (yn + u["bonus"] * u["v"]) * (u["g"] * _sigmoid(u["g"]))
        o_ref[u["b"], :, sl] = res.astype(o_ref.dtype)


def _rwkv(p3, main_blk, lora_blk, mu_m, mu_l, w0, w2p, a0, a2p, k_k, k_a, r_k, lnx_w, lnx_b, ones_bd):
    B, T, _ = p3.shape
    W = w0.shape[1]
    C = CHUNK
    NU = B * (W // QW)
    full = lambda shape: pl.BlockSpec(shape, lambda c: (0,) * len(shape))
    return pl.pallas_call(
        _rwkv_kernel,
        grid=(T // C,),
        in_specs=[
            pl.BlockSpec((B, C, 4 * W), lambda c: (0, c, main_blk)),
            pl.BlockSpec((B, C, LORA_BLK), lambda c: (0, c, lora_blk)),
            full((1, 4 * W)), full((1, LORA_BLK)),
            full((1, W)), full((LANE, W)), full((1, W)), full((LORA_BLK, W)),
            full((1, W)), full((1, W)), full((1, W)), full((1, W)), full((1, W)),
            full((QW, QW)),
        ],
        out_specs=pl.BlockSpec((B, C, W), lambda c: (0, c, 0)),
        out_shape=jax.ShapeDtypeStruct((B, T, W), BF16),
        scratch_shapes=[
            pltpu.VMEM((B, 4 * W), F32),
            pltpu.VMEM((B, LORA_BLK), F32),
            pltpu.VMEM((NU, QW, QW), F32),
            pltpu.VMEM((NU * len(BD_ROLES), QW, QW), BF16),
        ],
        compiler_params=pltpu.CompilerParams(
            dimension_semantics=("arbitrary",), vmem_limit_bytes=VMEM_LIMIT),
        name="rwkv7_mix",
    )(p3, p3, mu_m, mu_l, w0, w2p, a0, a2p, k_k, k_a, r_k, lnx_w, lnx_b, ones_bd)


def _hgrn_kernel(layer, off, ph_ref, lbp_ref, ng_ref, o_ref, state):
    c = pl.program_id(0)
    C = CHUNK
    NB = ph_ref.shape[0]
    W = ng_ref.shape[1]
    D = HGRN_HEAD
    NH = W // D

    @pl.when(c == 0)
    def _():
        state[...] = jnp.zeros_like(state)

    lp = lbp_ref[...]
    e = jnp.exp(lp - jnp.max(lp, axis=0, keepdims=True))
    lb = jnp.sum(e[:layer + 1], axis=0, keepdims=True) / jnp.sum(e, axis=0, keepdims=True)

    t_i = lax.broadcasted_iota(jnp.int32, (C, C), 0)
    s_i = lax.broadcasted_iota(jnp.int32, (C, C), 1)
    causal = t_i >= s_i
    tril = causal.astype(BF16)

    units = []
    for b in range(NB):
        x = ph_ref[b]
        qv = x[:, off + 0 * W:off + 1 * W]
        fr = x[:, off + 1 * W:off + 2 * W]
        iv = x[:, off + 2 * W:off + 3 * W]
        gate = x[:, off + 3 * W:off + 4 * W]
        f = lb + (1.0 - lb) * _sigmoid(fr)
        kx = 1.0 - f
        G = _cumsum_rows(jnp.log(f), tril)
        GC = G[C - 1:C, :]
        qt = (qv * jnp.exp(G)).astype(BF16)
        kb = (kx * jnp.exp(-G)).astype(BF16)
        kd = (kx * jnp.exp(GC - G)).astype(BF16)
        PC = jnp.exp(GC)
        ivb = iv.astype(BF16)
        for h in range(NH):
            sl = slice(h * D, (h + 1) * D)
            units.append(dict(b=b, h=h, sl=sl, qt=qt[:, sl], kb=kb[:, sl], kd=kd[:, sl],
                              iv=ivb[:, sl], PC=PC[:, sl], g=gate[:, sl]))

    for u in units:
        u["A"] = _mm_nt(u["qt"], u["kb"])
    for u in units:
        u["S"] = state[u["b"] * NH + u["h"]]
        u["inter"] = _mm_nt(u["qt"], u["S"])
    for u in units:
        u["o"] = _mm(jnp.where(causal, u["A"], 0.0), u["iv"]) + u["inter"]
    for u in units:
        state[u["b"] * NH + u["h"]] = u["S"] * u["PC"] + _mm_tn(u["iv"], u["kd"])
    for u in units:
        o = u["o"]
        g = u["g"]
        ms = jnp.mean(o * o, axis=-1, keepdims=True)
        res = o * lax.rsqrt(ms + NORM_EPS) * ng_ref[:, u["sl"]] * (g * _sigmoid(g))
        o_ref[u["b"], :, u["sl"]] = res.astype(o_ref.dtype)


def _hgrn(p3, blk_w, blk, off, lb_param, norm_g, layer):
    B, T, _ = p3.shape
    W = norm_g.shape[1]
    C = CHUNK
    L = lb_param.shape[0]
    return pl.pallas_call(
        functools.partial(_hgrn_kernel, layer, off),
        grid=(T // C,),
        in_specs=[
            pl.BlockSpec((B, C, blk_w), lambda c: (0, c, blk)),
            pl.BlockSpec((L, W), lambda c: (0, 0)),
            pl.BlockSpec((1, W), lambda c: (0, 0)),
        ],
        out_specs=pl.BlockSpec((B, C, W), lambda c: (0, c, 0)),
        out_shape=jax.ShapeDtypeStruct((B, T, W), BF16),
        scratch_shapes=[pltpu.VMEM((B * (W // HGRN_HEAD), HGRN_HEAD, HGRN_HEAD), F32)],
        compiler_params=pltpu.CompilerParams(
            dimension_semantics=("arbitrary",), vmem_limit_bytes=VMEM_LIMIT),
        name="hgrn2_mix",
    )(p3, lb_param, norm_g)


def _outproj_kernel(final, yr_ref, yh_ref, x_ref, wr_ref, wh_ref, g_ref, o_ref):
    acc = jnp.dot(yr_ref[...], wr_ref[...], preferred_element_type=F32)
    acc += jnp.dot(yh_ref[...], wh_ref[...], preferred_element_type=F32)
    h = x_ref[...] + acc
    if final:
        ms = jnp.mean(h * h, axis=-1, keepdims=True)
        h = h * lax.rsqrt(ms + NORM_EPS) * g_ref[...]
    o_ref[...] = h


def _outproj(yr, yh, x2, wr, wh, g, tm, final):
    m, d = x2.shape
    wr_w = yr.shape[1]
    wh_w = yh.shape[1]
    return pl.pallas_call(
        functools.partial(_outproj_kernel, final),
        grid=(m // tm,),
        in_specs=[
            pl.BlockSpec((tm, wr_w), lambda i: (i, 0)),
            pl.BlockSpec((tm, wh_w), lambda i: (i, 0)),
            pl.BlockSpec((tm, d), lambda i: (i, 0)),
            pl.BlockSpec((wr_w, d), lambda i: (0, 0)),
            pl.BlockSpec((wh_w, d), lambda i: (0, 0)),
            pl.BlockSpec((1, d), lambda i: (0, 0)),
        ],
        out_specs=pl.BlockSpec((tm, d), lambda i: (i, 0)),
        out_shape=jax.ShapeDtypeStruct((m, d), F32),
        compiler_params=pltpu.CompilerParams(
            dimension_semantics=("arbitrary",), vmem_limit_bytes=VMEM_LIMIT),
        name="outproj",
    )(yr, yh, x2, wr, wh, g)


def kernel(x, norm_g, w_in, mu, w0, w2, a0, a2, k_k, k_a, r_k, lnx_w, lnx_b,
           hgrn_norm_g, lb_param, w_out, final_g):
    B, T, D = x.shape
    depth = w_in.shape[0]
    RW = w0.shape[1]
    HW = hgrn_norm_g.shape[1]
    n_r = 4 * RW + 2 * LORA
    tn = 768
    n_p = pl.cdiv(w_in.shape[2], tn) * tn
    h_start = (n_r // LANE) * LANE
    h_off = n_r - h_start
    h_blk_w = n_p - h_start
    assert (4 * RW) % LORA_BLK == 0 and 2 * LORA <= LORA_BLK and w2.shape[1] == LORA <= LANE
    assert h_start % h_blk_w == 0 and h_off + 4 * HW <= h_blk_w

    row = lambda z: z.reshape(1, -1).astype(F32)
    zr = lambda rows: jnp.zeros((rows, RW), F32)
    ones_bd = (jnp.arange(QW)[:, None] // RWKV_HEAD == jnp.arange(QW)[None, :] // RWKV_HEAD).astype(BF16)

    w_in_t = jnp.swapaxes(w_in, 1, 2)
    h = x.reshape(B * T, D)
    for l in range(depth):
        mu_l = mu[l].astype(F32)
        mu_main = mu_l[:4 * RW].reshape(1, -1)
        mu_lora = jnp.concatenate(
            [mu_l[4 * RW:], jnp.zeros((LORA_BLK - 2 * LORA,), F32)]).reshape(1, -1)
        w2p = jnp.concatenate([w2[l].astype(F32), zr(LANE - LORA)], axis=0).astype(BF16)
        a2p = jnp.concatenate([zr(LORA), a2[l].astype(F32), zr(LORA_BLK - 2 * LORA)],
                              axis=0).astype(BF16)

        p = _inproj(h, row(norm_g[l]), w_in_t, l, tm=1024, tn=tn, groups=4)
        p3 = p.reshape(B, T, n_p)
        y_r = _rwkv(p3, 0, (4 * RW) // LORA_BLK, mu_main, mu_lora,
                    row(w0[l]), w2p, row(a0[l]), a2p, row(k_k[l]), row(k_a[l]), row(r_k[l]),
                    row(lnx_w[l]), row(lnx_b[l]), ones_bd)
        y_h = _hgrn(p3, h_blk_w, h_start // h_blk_w, h_off, lb_param.astype(F32),
                    row(hgrn_norm_g[l]), l)
        wo = w_out[l].astype(BF16)
        h = _outproj(y_r.reshape(B * T, RW), y_h.reshape(B * T, HW), h,
                     wo[:RW], wo[RW:], row(final_g), tm=256, final=(l == depth - 1))
    return h.reshape(B, T, D)
```

```python
import functools
import math

import jax
import jax.numpy as jnp
from jax import lax
from jax.experimental import pallas as pl
from jax.experimental.pallas import tpu as pltpu

F32 = jnp.float32
BF16 = jnp.bfloat16

NORM_EPS = 1e-6
LNX_EPS = 64e-5
RWKV_HEAD = 64
HGRN_HEAD = 128
LORA = 96
LORA_BLK = 256
LANE = 128
CHUNK = 64
QUAD = 4
QW = QUAD * RWKV_HEAD
BD_ROLES = ("B", "K", "V", "N", "Z", "U")
EXP_M05 = math.exp(-0.5)
VMEM_LIMIT = 48 * 1024 * 1024


def _mm(a, b):
    return jnp.dot(a.astype(BF16), b.astype(BF16), preferred_element_type=F32)


def _mm_nt(a, b):
    return lax.dot_general(a.astype(BF16), b.astype(BF16), (((1,), (1,)), ((), ())),
                           preferred_element_type=F32)


def _mm_tn(a, b):
    return lax.dot_general(a.astype(BF16), b.astype(BF16), (((0,), (0,)), ((), ())),
                           preferred_element_type=F32)


def _split3(x):
    hi = x.astype(BF16)
    r1 = x - hi.astype(F32)
    mid = r1.astype(BF16)
    lo = (r1 - mid.astype(F32)).astype(BF16)
    return hi, mid, lo


def _cumsum_rows(x, tril_bf16):
    hi, mid, lo = _split3(x)
    dot = functools.partial(jnp.dot, preferred_element_type=F32)
    return dot(tril_bf16, hi) + dot(tril_bf16, mid) + dot(tril_bf16, lo)


def _sigmoid(x):
    return 0.5 + 0.5 * jnp.tanh(0.5 * x)


def _inproj_kernel(n_valid, seq_len, x_ref, g_ref, w_ref, mu_ref, o_ref, hn_ref, carry_ref):
    q = pl.program_id(0)
    j = pl.program_id(1)
    i = pl.program_id(2)
    tm, tn = o_ref.shape

    @pl.when((q == 0) & (j == 0) & (i == 0))
    def _():
        carry_ref[...] = jnp.zeros_like(carry_ref)

    @pl.when(j == 0)
    def _():
        x = x_ref[...]
        ms = jnp.mean(x * x, axis=-1, keepdims=True)
        hn_ref[i] = (x * lax.rsqrt(ms + NORM_EPS) * g_ref[...]).astype(BF16)

    w_row = j * tn + lax.broadcasted_iota(jnp.int32, (tn, 1), 0)
    w = jnp.where(w_row < n_valid, w_ref[...], 0.0)
    acc = _mm_nt(hn_ref[i], w)

    first_row = (q * pl.num_programs(2) + i) * tm
    prev_last = jnp.where(first_row % seq_len == 0, 0.0, carry_ref[j])
    carry_ref[j] = acc[tm - 1:tm, :]
    is_row0 = lax.broadcasted_iota(jnp.int32, (tm, 1), 0) == 0
    prev = jnp.where(is_row0, prev_last, pltpu.roll(acc, 1, axis=0))
    o_ref[...] = acc + mu_ref[...] * (prev - acc)


def _inproj(x2, g, wt_all, layer, mu_full, seq_len, tm, tn, groups):
    m, d = x2.shape
    n_valid = wt_all.shape[1]
    n = pl.cdiv(n_valid, tn) * tn
    ni = m // (groups * tm)
    assert seq_len % tm == 0 and mu_full.shape == (1, n)
    x_map = lambda q, j, i: (q * ni + jnp.where(j == 0, i, ni - 1), 0)
    return pl.pallas_call(
        functools.partial(_inproj_kernel, n_valid, seq_len),
        grid=(groups, n // tn, ni),
        in_specs=[
            pl.BlockSpec((tm, d), x_map),
            pl.BlockSpec((1, d), lambda q, j, i: (0, 0)),
            pl.BlockSpec((None, tn, d), lambda q, j, i: (layer, j, 0)),
            pl.BlockSpec((1, tn), lambda q, j, i: (0, j)),
        ],
        out_specs=pl.BlockSpec((tm, tn), lambda q, j, i: (q * ni + i, j)),
        out_shape=jax.ShapeDtypeStruct((m, n), F32),
        scratch_shapes=[pltpu.VMEM((ni, tm, d), BF16), pltpu.VMEM((n // tn, 1, tn), F32)],
        compiler_params=pltpu.CompilerParams(
            dimension_semantics=("arbitrary", "arbitrary", "arbitrary"),
            vmem_limit_bytes=VMEM_LIMIT),
        name="inproj",
    )(x2, g, wt_all, mu_full)


def _rwkv_kernel(pm_ref, pl_ref, w0_ref, w2_ref, a0_ref, a2_ref,
                 kk_ref, ka_ref, rk_ref, lw_ref, lb_ref, ones_ref,
                 o_ref, state, bd_ref):
    c = pl.program_id(0)
    C = CHUNK
    NB = pm_ref.shape[0]
    W = w0_ref.shape[1]
    NQ = W // QW

    @pl.when(c == 0)
    def _():
        for ref in (state, bd_ref):
            ref[...] = jnp.zeros_like(ref)

    t_i = lax.broadcasted_iota(jnp.int32, (C, C), 0)
    s_i = lax.broadcasted_iota(jnp.int32, (C, C), 1)
    tril = (t_i >= s_i).astype(BF16)
    ones_bd = ones_ref[...]
    dot = functools.partial(jnp.dot, preferred_element_type=F32)

    lane = lax.broadcasted_iota(jnp.int32, (C, QW), 1)
    lane_s = lane % RWKV_HEAD
    trow = lax.broadcasted_iota(jnp.int32, (C, QW), 0)
    strict = lane_s < trow
    incl = lane_s <= trow
    eye_q = jnp.where(lane_s == trow, 1.0, 0.0).astype(F32)
    bd_mask = (lax.broadcasted_iota(jnp.int32, (QW, QW), 0) // RWKV_HEAD
               == lax.broadcasted_iota(jnp.int32, (QW, QW), 1) // RWKV_HEAD)

    def bd(u, role, x):
        slot = u["ui"] * len(BD_ROLES) + BD_ROLES.index(role)
        xb = x.astype(BF16)
        for h in range(QUAD):
            hs = slice(h * RWKV_HEAD, (h + 1) * RWKV_HEAD)
            bd_ref[slot, h * C:(h + 1) * C, hs] = xb[:, hs]
        return bd_ref[slot]

    def segsum_all(xs):
        out = dot(jnp.concatenate([x.astype(BF16) for x in xs], axis=0), ones_bd)
        return [out[i * C:(i + 1) * C] for i in range(len(xs))]

    xl = jnp.concatenate([pl_ref[b] for b in range(NB)], axis=0)
    z = w0_ref[...] + _mm(jnp.tanh(xl[:, :LANE]), w2_ref[...])
    ld_all = -EXP_M05 * _sigmoid(z)
    a_all = _sigmoid(a0_ref[...] + _mm(xl, a2_ref[...]))
    units = []
    for b in range(NB):
        r = pm_ref[b, :, 0 * W:1 * W]
        k = pm_ref[b, :, 1 * W:2 * W]
        v = pm_ref[b, :, 2 * W:3 * W]
        gate = pm_ref[b, :, 3 * W:4 * W]
        ld = ld_all[b * C:(b + 1) * C]
        a = a_all[b * C:(b + 1) * C]
        G = _cumsum_rows(ld, tril)
        Gx = G - ld
        GC = G[C - 1:C, :]
        for q in range(NQ):
            sl = slice(q * QW, (q + 1) * QW)
            units.append(dict(b=b, ui=b * NQ + q, sl=sl, r=r[:, sl], k=k[:, sl], v=v[:, sl],
                              g=gate[:, sl], a=a[:, sl], G=G[:, sl], Gx=Gx[:, sl], GC=GC[:, sl]))

    for u in units:
        u["kk"] = u["k"] * kk_ref[:, u["sl"]]
    for u, n2 in zip(units, segsum_all([u["kk"] * u["kk"] for u in units])):
        u["n2"] = n2
    for u in units:
        sl = u["sl"]
        kkn = u["kk"] / jnp.maximum(jnp.sqrt(u["n2"]), 1e-12)
        k2 = u["k"] * (1.0 + (u["a"] - 1.0) * ka_ref[:, sl])
        av = -kkn
        bv = kkn * u["a"]
        einv = jnp.exp(-u["G"])
        eC = jnp.exp(u["GC"] - u["G"])
        u["rkk"] = u["r"] * k2 * rk_ref[:, sl]
        u["BDB"] = bd(u, "B", bv * einv)
        u["BDK"] = bd(u, "K", k2 * einv)
        u["BKe"] = jnp.concatenate([bv * eC, k2 * eC], axis=0).astype(BF16)
        u["BDV"] = bd(u, "V", u["v"])
        u["lhs"] = jnp.concatenate([av * jnp.exp(u["Gx"]), u["r"] * jnp.exp(u["G"])],
                                   axis=0).astype(BF16)
        pc_col = jnp.transpose(jnp.broadcast_to(jnp.exp(u["GC"]), (LANE, QW)))
        u["PCc"] = jnp.concatenate([pc_col] * (QW // LANE), axis=1)

    for u in units:
        u["SB"] = _mm_nt(u["lhs"], u["BDB"])
    for u in units:
        u["SK"] = _mm_nt(u["lhs"], u["BDK"])
    for u in units:
        u["N"] = jnp.where(strict, u["SB"][:C], 0.0)
        u["Lrb"] = jnp.where(incl, u["SB"][C:], 0.0)

    for u in units:
        u["P"] = eye_q + u["N"]
        u["Np"] = _mm(u["N"], bd(u, "N", u["N"]))
    for u in units:
        akl = jnp.concatenate([jnp.where(strict, u["SK"][:C], 0.0),
                               jnp.where(incl, u["SK"][C:], 0.0)], axis=0)
        both = _mm(akl, u["BDV"])
        u["AV"] = both[:C]
        u["LV"] = both[C:]
    for _ in range(4):
        for u in units:
            out = _mm(jnp.concatenate([u["P"], u["Np"]], axis=0), bd(u, "N", u["Np"]))
            u["P"] = u["P"] + out[:C]
            u["Np"] = out[C:]
    for u in units:
        u["P"] = u["P"] + _mm(u["P"], bd(u, "N", u["Np"]))

    for u in units:
        u["H"] = state[u["ui"]]
        zy = _mm(u["lhs"], u["H"])
        u["Z0"] = zy[:C] + u["AV"]
        u["YS"] = zy[C:]
    for u in units:
        u["U"] = _mm(u["P"], bd(u, "Z", u["Z0"]))
    for u in units:
        u["Y"] = u["YS"] + _mm(u["Lrb"], bd(u, "U", u["U"])) + u["LV"]
    for u in units:
        upd = _mm_tn(u["BKe"], jnp.concatenate([u["U"], u["v"]], axis=0))
        state[u["ui"]] = u["H"] * u["PCc"] + jnp.where(bd_mask, upd, 0.0)

    for u, s in zip(units, segsum_all([u["Y"] for u in units])):
        u["yc"] = u["Y"] - s * (1.0 / RWKV_HEAD)
    for u, s in zip(units, segsum_all([u["yc"] * u["yc"] for u in units])):
        u["var"] = s * (1.0 / RWKV_HEAD)
    for u, s in zip(units, segsum_all([u["rkk"] for u in units])):
        u["bonus"] = s
    for u in units:
        sl = u["sl"]
        yn = u["yc"] * lax.rsqrt(u["var"] + LNX_EPS) * lw_ref[:, sl] + lb_ref[:, sl]
        res = (yn + u["bonus"] * u["v"]) * (u["g"] * _sigmoid(u["g"]))
        o_ref[u["b"], :, sl] = res.astype(o_ref.dtype)


def _rwkv(p3, main_blk, lora_blk, w0, w2p, a0, a2p, k_k, k_a, r_k, lnx_w, lnx_b, ones_bd):
    B, T, _ = p3.shape
    W = w0.shape[1]
    C = CHUNK
    NU = B * (W // QW)
    full = lambda shape: pl.BlockSpec(shape, lambda c: (0,) * len(shape))
    return pl.pallas_call(
        _rwkv_kernel,
        grid=(T // C,),
        in_specs=[
            pl.BlockSpec((B, C, 4 * W), lambda c: (0, c, main_blk)),
            pl.BlockSpec((B, C, LORA_BLK), lambda c: (0, c, lora_blk)),
            full((1, W)), full((LANE, W)), full((1, W)), full((LORA_BLK, W)),
            full((1, W)), full((1, W)), full((1, W)), full((1, W)), full((1, W)),
            full((QW, QW)),
        ],
        out_specs=pl.BlockSpec((B, C, W), lambda c: (0, c, 0)),
        out_shape=jax.ShapeDtypeStruct((B, T, W), BF16),
        scratch_shapes=[
            pltpu.VMEM((NU, QW, QW), F32),
            pltpu.VMEM((NU * len(BD_ROLES), QW, QW), BF16),
        ],
        compiler_params=pltpu.CompilerParams(
            dimension_semantics=("arbitrary",), vmem_limit_bytes=VMEM_LIMIT),
        name="rwkv7_mix",
    )(p3, p3, w0, w2p, a0, a2p, k_k, k_a, r_k, lnx_w, lnx_b, ones_bd)


def _hgrn_kernel(layer, off, ph_ref, lbp_ref, ng_ref, o_ref, state):
    c = pl.program_id(0)
    C = CHUNK
    NB = ph_ref.shape[0]
    W = ng_ref.shape[1]
    D = HGRN_HEAD
    NH = W // D

    @pl.when(c == 0)
    def _():
        state[...] = jnp.zeros_like(state)

    lp = lbp_ref[...]
    e = jnp.exp(lp - jnp.max(lp, axis=0, keepdims=True))
    lb = jnp.sum(e[:layer + 1], axis=0, keepdims=True) / jnp.sum(e, axis=0, keepdims=True)

    t_i = lax.broadcasted_iota(jnp.int32, (C, C), 0)
    s_i = lax.broadcasted_iota(jnp.int32, (C, C), 1)
    causal = t_i >= s_i
    tril = causal.astype(BF16)

    units = []
    for b in range(NB):
        x = ph_ref[b]
        qv = x[:, off + 0 * W:off + 1 * W]
        fr = x[:, off + 1 * W:off + 2 * W]
        iv = x[:, off + 2 * W:off + 3 * W]
        gate = x[:, off + 3 * W:off + 4 * W]
        f = lb + (1.0 - lb) * _sigmoid(fr)
        kx = 1.0 - f
        G = _cumsum_rows(jnp.log(f), tril)
        GC = G[C - 1:C, :]
        qt = (qv * jnp.exp(G)).astype(BF16)
        kb = (kx * jnp.exp(-G)).astype(BF16)
        kd = (kx * jnp.exp(GC - G)).astype(BF16)
        PC = jnp.exp(GC)
        ivb = iv.astype(BF16)
        for h in range(NH):
            sl = slice(h * D, (h + 1) * D)
            units.append(dict(b=b, h=h, sl=sl, qt=qt[:, sl], kb=kb[:, sl], kd=kd[:, sl],
                              iv=ivb[:, sl], PC=PC[:, sl], g=gate[:, sl]))

    for u in units:
        u["A"] = _mm_nt(u["qt"], u["kb"])
    for u in units:
        u["S"] = state[u["b"] * NH + u["h"]]
        u["inter"] = _mm_nt(u["qt"], u["S"])
    for u in units:
        u["o"] = _mm(jnp.where(causal, u["A"], 0.0), u["iv"]) + u["inter"]
    for u in units:
        state[u["b"] * NH + u["h"]] = u["S"] * u["PC"] + _mm_tn(u["iv"], u["kd"])
    for u in units:
        o = u["o"]
        g = u["g"]
        ms = jnp.mean(o * o, axis=-1, keepdims=True)
        res = o * lax.rsqrt(ms + NORM_EPS) * ng_ref[:, u["sl"]] * (g * _sigmoid(g))
        o_ref[u["b"], :, u["sl"]] = res.astype(o_ref.dtype)


def _hgrn(p3, blk_w, blk, off, lb_param, norm_g, layer):
    B, T, _ = p3.shape
    W = norm_g.shape[1]
    C = CHUNK
    L = lb_param.shape[0]
    return pl.pallas_call(
        functools.partial(_hgrn_kernel, layer, off),
        grid=(T // C,),
        in_specs=[
            pl.BlockSpec((B, C, blk_w), lambda c: (0, c, blk)),
            pl.BlockSpec((L, W), lambda c: (0, 0)),
            pl.BlockSpec((1, W), lambda c: (0, 0)),
        ],
        out_specs=pl.BlockSpec((B, C, W), lambda c: (0, c, 0)),
        out_shape=jax.ShapeDtypeStruct((B, T, W), BF16),
        scratch_shapes=[pltpu.VMEM((B * (W // HGRN_HEAD), HGRN_HEAD, HGRN_HEAD), F32)],
        compiler_params=pltpu.CompilerParams(
            dimension_semantics=("arbitrary",), vmem_limit_bytes=VMEM_LIMIT),
        name="hgrn2_mix",
    )(p3, lb_param, norm_g)


def _outproj_kernel(final, yr_ref, yh_ref, x_ref, wr_ref, wh_ref, g_ref, o_ref):
    acc = jnp.dot(yr_ref[...], wr_ref[...], preferred_element_type=F32)
    acc += jnp.dot(yh_ref[...], wh_ref[...], preferred_element_type=F32)
    h = x_ref[...] + acc
    if final:
        ms = jnp.mean(h * h, axis=-1, keepdims=True)
        h = h * lax.rsqrt(ms + NORM_EPS) * g_ref[...]
    o_ref[...] = h


def _outproj(yr, yh, x2, wr, wh, g, tm, final):
    m, d = x2.shape
    wr_w = yr.shape[1]
    wh_w = yh.shape[1]
    return pl.pallas_call(
        functools.partial(_outproj_kernel, final),
        grid=(m // tm,),
        in_specs=[
            pl.BlockSpec((tm, wr_w), lambda i: (i, 0)),
            pl.BlockSpec((tm, wh_w), lambda i: (i, 0)),
            pl.BlockSpec((tm, d), lambda i: (i, 0)),
            pl.BlockSpec((wr_w, d), lambda i: (0, 0)),
            pl.BlockSpec((wh_w, d), lambda i: (0, 0)),
            pl.BlockSpec((1, d), lambda i: (0, 0)),
        ],
        out_specs=pl.BlockSpec((tm, d), lambda i: (i, 0)),
        out_shape=jax.ShapeDtypeStruct((m, d), F32),
        compiler_params=pltpu.CompilerParams(
            dimension_semantics=("arbitrary",), vmem_limit_bytes=VMEM_LIMIT),
        name="outproj",
    )(yr, yh, x2, wr, wh, g)


def kernel(x, norm_g, w_in, mu, w0, w2, a0, a2, k_k, k_a, r_k, lnx_w, lnx_b,
           hgrn_norm_g, lb_param, w_out, final_g):
    B, T, D = x.shape
    depth = w_in.shape[0]
    RW = w0.shape[1]
    HW = hgrn_norm_g.shape[1]
    n_r = 4 * RW + 2 * LORA
    tn = 768
    n_p = pl.cdiv(w_in.shape[2], tn) * tn
    h_start = (n_r // LANE) * LANE
    h_off = n_r - h_start
    h_blk_w = n_p - h_start
    assert (4 * RW) % LORA_BLK == 0 and 2 * LORA <= LORA_BLK and w2.shape[1] == LORA <= LANE
    assert h_start % h_blk_w == 0 and h_off + 4 * HW <= h_blk_w

    row = lambda z: z.reshape(1, -1).astype(F32)
    zr = lambda rows: jnp.zeros((rows, RW), F32)
    ones_bd = (jnp.arange(QW)[:, None] // RWKV_HEAD == jnp.arange(QW)[None, :] // RWKV_HEAD).astype(BF16)

    w_in_t = jnp.swapaxes(w_in, 1, 2)
    h = x.reshape(B * T, D)
    for l in range(depth):
        mu_full = jnp.concatenate([mu[l].astype(F32), jnp.zeros((n_p - n_r,), F32)]).reshape(1, -1)
        w2p = jnp.concatenate([w2[l].astype(F32), zr(LANE - LORA)], axis=0).astype(BF16)
        a2p = jnp.concatenate([zr(LORA), a2[l].astype(F32), zr(LORA_BLK - 2 * LORA)],
                              axis=0).astype(BF16)

        p = _inproj(h, row(norm_g[l]), w_in_t, l, mu_full, T, tm=1024, tn=tn, groups=4)
        p3 = p.reshape(B, T, n_p)
        y_r = _rwkv(p3, 0, (4 * RW) // LORA_BLK, row(w0[l]), w2p, row(a0[l]), a2p, row(k_k[l]), row(k_a[l]), row(r_k[l]),
                    row(lnx_w[l]), row(lnx_b[l]), ones_bd)
        y_h = _hgrn(p3, h_blk_w, h_start // h_blk_w, h_off, lb_param.astype(F32),
                    row(hgrn_norm_g[l]), l)
        wo = w_out[l].astype(BF16)
        h = _outproj(y_r.reshape(B * T, RW), y_h.reshape(B * T, HW), h,
                     wo[:RW], wo[RW:], row(final_g), tm=512, final=(l == depth - 1))
    return h.reshape(B, T, D)
```

```python
import functools
import math

import jax
import jax.numpy as jnp
from jax import lax
from jax.experimental import pallas as pl
from jax.experimental.pallas import tpu as pltpu

F32 = jnp.float32
BF16 = jnp.bfloat16

NORM_EPS = 1e-6
LNX_EPS = 64e-5
RWKV_HEAD = 64
HGRN_HEAD = 128
LORA = 96
LORA_BLK = 256
LANE = 128
CHUNK = 64
QUAD = 4
QW = QUAD * RWKV_HEAD
BD_ROLES = ("B", "K", "V", "N", "Z", "U")
UNIT_SKEW = 0
EXP_M05 = math.exp(-0.5)
VMEM_LIMIT = 48 * 1024 * 1024


def _mm(a, b):
    return jnp.dot(a.astype(BF16), b.astype(BF16), preferred_element_type=F32)


def _mm_nt(a, b):
    return lax.dot_general(a.astype(BF16), b.astype(BF16), (((1,), (1,)), ((), ())),
                           preferred_element_type=F32)


def _mm_tn(a, b):
    return lax.dot_general(a.astype(BF16), b.astype(BF16), (((0,), (0,)), ((), ())),
                           preferred_element_type=F32)


def _split3(x):
    hi = x.astype(BF16)
    r1 = x - hi.astype(F32)
    mid = r1.astype(BF16)
    lo = (r1 - mid.astype(F32)).astype(BF16)
    return hi, mid, lo


def _cumsum_rows(x, tril_bf16):
    hi, mid, lo = _split3(x)
    dot = functools.partial(jnp.dot, preferred_element_type=F32)
    return dot(tril_bf16, hi) + dot(tril_bf16, mid) + dot(tril_bf16, lo)


def _sigmoid(x):
    return 0.5 + 0.5 * jnp.tanh(0.5 * x)


def _inproj_kernel(n_valid, seq_len, x_ref, g_ref, w_ref, mu_ref, o_ref, hn_ref, carry_ref):
    q = pl.program_id(0)
    j = pl.program_id(1)
    i = pl.program_id(2)
    tm, tn = o_ref.shape

    @pl.when((q == 0) & (j == 0) & (i == 0))
    def _():
        carry_ref[...] = jnp.zeros_like(carry_ref)

    @pl.when(j == 0)
    def _():
        x = x_ref[...]
        ms = jnp.mean(x * x, axis=-1, keepdims=True)
        hn_ref[i] = (x * lax.rsqrt(ms + NORM_EPS) * g_ref[...]).astype(BF16)

    w_row = j * tn + lax.broadcasted_iota(jnp.int32, (tn, 1), 0)
    w = jnp.where(w_row < n_valid, w_ref[...], 0.0)
    acc = _mm_nt(hn_ref[i], w)

    first_row = (q * pl.num_programs(2) + i) * tm
    prev_last = jnp.where(first_row % seq_len == 0, 0.0, carry_ref[j])
    carry_ref[j] = acc[tm - 1:tm, :]
    is_row0 = lax.broadcasted_iota(jnp.int32, (tm, 1), 0) == 0
    prev = jnp.where(is_row0, prev_last, pltpu.roll(acc, 1, axis=0))
    o_ref[...] = acc + mu_ref[...] * (prev - acc)


def _inproj(x2, g, wt_all, layer, mu_full, seq_len, tm, tn, groups):
    m, d = x2.shape
    n_valid = wt_all.shape[1]
    n = pl.cdiv(n_valid, tn) * tn
    ni = m // (groups * tm)
    assert seq_len % tm == 0 and mu_full.shape == (1, n)
    x_map = lambda q, j, i: (q * ni + jnp.where(j == 0, i, ni - 1), 0)
    return pl.pallas_call(
        functools.partial(_inproj_kernel, n_valid, seq_len),
        grid=(groups, n // tn, ni),
        in_specs=[
            pl.BlockSpec((tm, d), x_map),
            pl.BlockSpec((1, d), lambda q, j, i: (0, 0)),
            pl.BlockSpec((None, tn, d), lambda q, j, i: (layer, j, 0)),
            pl.BlockSpec((1, tn), lambda q, j, i: (0, j)),
        ],
        out_specs=pl.BlockSpec((tm, tn), lambda q, j, i: (q * ni + i, j)),
        out_shape=jax.ShapeDtypeStruct((m, n), F32),
        scratch_shapes=[pltpu.VMEM((ni, tm, d), BF16), pltpu.VMEM((n // tn, 1, tn), F32)],
        compiler_params=pltpu.CompilerParams(
            dimension_semantics=("arbitrary", "arbitrary", "arbitrary"),
            vmem_limit_bytes=VMEM_LIMIT),
        name="inproj",
    )(x2, g, wt_all, mu_full)


def _rwkv_stages(pm_ref, pl_ref, w0_ref, w2_ref, a0_ref, a2_ref,
                 kk_ref, ka_ref, rk_ref, lw_ref, lb_ref, ones_ref,
                 o_ref, state, bd_ref):
    c = pl.program_id(0)
    C = CHUNK
    NB = pm_ref.shape[0]
    W = w0_ref.shape[1]
    NQ = W // QW

    @pl.when(c == 0)
    def _():
        for ref in (state, bd_ref):
            ref[...] = jnp.zeros_like(ref)

    t_i = lax.broadcasted_iota(jnp.int32, (C, C), 0)
    s_i = lax.broadcasted_iota(jnp.int32, (C, C), 1)
    tril = (t_i >= s_i).astype(BF16)
    ones_bd = ones_ref[...]
    dot = functools.partial(jnp.dot, preferred_element_type=F32)

    lane = lax.broadcasted_iota(jnp.int32, (C, QW), 1)
    lane_s = lane % RWKV_HEAD
    trow = lax.broadcasted_iota(jnp.int32, (C, QW), 0)
    strict = lane_s < trow
    incl = lane_s <= trow
    eye_q = jnp.where(lane_s == trow, 1.0, 0.0).astype(F32)
    bd_mask = (lax.broadcasted_iota(jnp.int32, (QW, QW), 0) // RWKV_HEAD
               == lax.broadcasted_iota(jnp.int32, (QW, QW), 1) // RWKV_HEAD)

    def bd(u, role, x):
        slot = u["ui"] * len(BD_ROLES) + BD_ROLES.index(role)
        xb = x.astype(BF16)
        for h in range(QUAD):
            hs = slice(h * RWKV_HEAD, (h + 1) * RWKV_HEAD)
            bd_ref[slot, h * C:(h + 1) * C, hs] = xb[:, hs]
        return bd_ref[slot]

    def segsum_all(xs):
        out = dot(jnp.concatenate([x.astype(BF16) for x in xs], axis=0), ones_bd)
        return [out[i * C:(i + 1) * C] for i in range(len(xs))]

    xl = jnp.concatenate([pl_ref[b] for b in range(NB)], axis=0)
    z = w0_ref[...] + _mm(jnp.tanh(xl[:, :LANE]), w2_ref[...])
    ld_all = -EXP_M05 * _sigmoid(z)
    a_all = _sigmoid(a0_ref[...] + _mm(xl, a2_ref[...]))
    units = []
    for b in range(NB):
        r = pm_ref[b, :, 0 * W:1 * W]
        k = pm_ref[b, :, 1 * W:2 * W]
        v = pm_ref[b, :, 2 * W:3 * W]
        gate = pm_ref[b, :, 3 * W:4 * W]
        ld = ld_all[b * C:(b + 1) * C]
        a = a_all[b * C:(b + 1) * C]
        G = _cumsum_rows(ld, tril)
        Gx = G - ld
        GC = G[C - 1:C, :]
        for q in range(NQ):
            sl = slice(q * QW, (q + 1) * QW)
            units.append(dict(b=b, ui=b * NQ + q, sl=sl, r=r[:, sl], k=k[:, sl], v=v[:, sl],
                              g=gate[:, sl], a=a[:, sl], G=G[:, sl], Gx=Gx[:, sl], GC=GC[:, sl]))
    yield

    for u in units:
        u["kk"] = u["k"] * kk_ref[:, u["sl"]]
    for u, n2 in zip(units, segsum_all([u["kk"] * u["kk"] for u in units])):
        u["n2"] = n2
    yield
    def unit_chain(u):
        sl = u["sl"]
        kkn = u["kk"] / jnp.maximum(jnp.sqrt(u["n2"]), 1e-12)
        k2 = u["k"] * (1.0 + (u["a"] - 1.0) * ka_ref[:, sl])
        av = -kkn
        bv = kkn * u["a"]
        einv = jnp.exp(-u["G"])
        eC = jnp.exp(u["GC"] - u["G"])
        u["rkk"] = u["r"] * k2 * rk_ref[:, sl]
        BDB = bd(u, "B", bv * einv)
        BDK = bd(u, "K", k2 * einv)
        BKe = jnp.concatenate([bv * eC, k2 * eC], axis=0).astype(BF16)
        BDV = bd(u, "V", u["v"])
        lhs = jnp.concatenate([av * jnp.exp(u["Gx"]), u["r"] * jnp.exp(u["G"])],
                              axis=0).astype(BF16)
        pc_col = jnp.transpose(jnp.broadcast_to(jnp.exp(u["GC"]), (LANE, QW)))
        PCc = jnp.concatenate([pc_col] * (QW // LANE), axis=1)
        yield
        SB = _mm_nt(lhs, BDB)
        yield
        SK = _mm_nt(lhs, BDK)
        yield
        N = jnp.where(strict, SB[:C], 0.0)
        Lrb = jnp.where(incl, SB[C:], 0.0)
        P = eye_q + N
        Np = _mm(N, bd(u, "N", N))
        yield
        akl = jnp.concatenate([jnp.where(strict, SK[:C], 0.0),
                               jnp.where(incl, SK[C:], 0.0)], axis=0)
        both = _mm(akl, BDV)
        yield
        for _ in range(4):
            out = _mm(jnp.concatenate([P, Np], axis=0), bd(u, "N", Np))
            P = P + out[:C]
            Np = out[C:]
            yield
        P = P + _mm(P, bd(u, "N", Np))
        yield
        H = state[u["ui"]]
        zy = _mm(lhs, H)
        yield
        U = _mm(P, bd(u, "Z", zy[:C] + both[:C]))
        yield
        u["Y"] = zy[C:] + _mm(Lrb, bd(u, "U", U)) + both[C:]
        yield
        upd = _mm_tn(BKe, jnp.concatenate([U, u["v"]], axis=0))
        state[u["ui"]] = H * PCc + jnp.where(bd_mask, upd, 0.0)

    chains = [unit_chain(u) for u in units]
    live = list(range(len(chains)))
    rnd = 0
    while live:
        for i in list(live):
            if rnd >= i * UNIT_SKEW and next(chains[i], "done") == "done":
                live.remove(i)
        rnd += 1
        yield

    for u, s in zip(units, segsum_all([u["Y"] for u in units])):
        u["yc"] = u["Y"] - s * (1.0 / RWKV_HEAD)
    yield
    for u, s in zip(units, segsum_all([u["yc"] * u["yc"] for u in units])):
        u["var"] = s * (1.0 / RWKV_HEAD)
    yield
    for u, s in zip(units, segsum_all([u["rkk"] for u in units])):
        u["bonus"] = s
    yield
    for u in units:
        sl = u["sl"]
        yn = u["yc"] * lax.rsqrt(u["var"] + LNX_EPS) * lw_ref[:, sl] + lb_ref[:, sl]
        res = (yn + u["bonus"] * u["v"]) * (u["g"] * _sigmoid(u["g"]))
        o_ref[u["b"], :, sl] = res.astype(o_ref.dtype)


def _hgrn_stages(layer, off, ph_ref, lbp_ref, ng_ref, o_ref, state):
    c = pl.program_id(0)
    C = CHUNK
    NB = ph_ref.shape[0]
    W = ng_ref.shape[1]
    D = HGRN_HEAD
    NH = W // D

    @pl.when(c == 0)
    def _():
        state[...] = jnp.zeros_like(state)

    lp = lbp_ref[...]
    e = jnp.exp(lp - jnp.max(lp, axis=0, keepdims=True))
    lb = jnp.sum(e[:layer + 1], axis=0, keepdims=True) / jnp.sum(e, axis=0, keepdims=True)

    t_i = lax.broadcasted_iota(jnp.int32, (C, C), 0)
    s_i = lax.broadcasted_iota(jnp.int32, (C, C), 1)
    causal = t_i >= s_i
    tril = causal.astype(BF16)

    units = []
    for b in range(NB):
        x = ph_ref[b]
        qv = x[:, off + 0 * W:off + 1 * W]
        fr = x[:, off + 1 * W:off + 2 * W]
        iv = x[:, off + 2 * W:off + 3 * W]
        gate = x[:, off + 3 * W:off + 4 * W]
        f = lb + (1.0 - lb) * _sigmoid(fr)
        kx = 1.0 - f
        G = _cumsum_rows(jnp.log(f), tril)
        GC = G[C - 1:C, :]
        qt = (qv * jnp.exp(G)).astype(BF16)
        kb = (kx * jnp.exp(-G)).astype(BF16)
        kd = (kx * jnp.exp(GC - G)).astype(BF16)
        PC = jnp.exp(GC)
        ivb = iv.astype(BF16)
        for h in range(NH):
            sl = slice(h * D, (h + 1) * D)
            units.append(dict(b=b, h=h, sl=sl, qt=qt[:, sl], kb=kb[:, sl], kd=kd[:, sl],
                              iv=ivb[:, sl], PC=PC[:, sl], g=gate[:, sl]))
        yield

    halves = (units[:len(units) // 2], units[len(units) // 2:])
    for part in halves:
        for u in part:
            u["A"] = _mm_nt(u["qt"], u["kb"])
        yield
    for part in halves:
        for u in part:
            u["S"] = state[u["b"] * NH + u["h"]]
            u["inter"] = _mm_nt(u["qt"], u["S"])
        yield
    for part in halves:
        for u in part:
            u["o"] = _mm(jnp.where(causal, u["A"], 0.0), u["iv"]) + u["inter"]
        yield
    for part in halves:
        for u in part:
            state[u["b"] * NH + u["h"]] = u["S"] * u["PC"] + _mm_tn(u["iv"], u["kd"])
        yield
    for part in halves:
        for u in part:
            o = u["o"]
            g = u["g"]
            ms = jnp.mean(o * o, axis=-1, keepdims=True)
            res = o * lax.rsqrt(ms + NORM_EPS) * ng_ref[:, u["sl"]] * (g * _sigmoid(g))
            o_ref[u["b"], :, u["sl"]] = res.astype(o_ref.dtype)
        yield


HGRN_AFTER_RWKV_STAGE = frozenset(range(6, 18))


def _mixers_kernel(layer, off, pm_ref, pl_ref, ph_ref, w0_ref, w2_ref, a0_ref, a2_ref,
                   kk_ref, ka_ref, rk_ref, lw_ref, lb_ref, ones_ref, lbp_ref, ng_ref,
                   or_ref, oh_ref, r_state, bd_ref, h_state):
    rw = _rwkv_stages(pm_ref, pl_ref, w0_ref, w2_ref, a0_ref, a2_ref, kk_ref, ka_ref, rk_ref,
                      lw_ref, lb_ref, ones_ref, or_ref, r_state, bd_ref)
    hg = _hgrn_stages(layer, off, ph_ref, lbp_ref, ng_ref, oh_ref, h_state)
    for stage, _ in enumerate(rw, start=1):
        if stage in HGRN_AFTER_RWKV_STAGE:
            next(hg, None)
    for _ in hg:
        pass


def _mixers(p3, main_blk, lora_blk, h_blk_w, h_blk, h_off, layer,
            w0, w2p, a0, a2p, k_k, k_a, r_k, lnx_w, lnx_b, ones_bd, lb_param, hgrn_g):
    B, T, _ = p3.shape
    W = w0.shape[1]
    HW = hgrn_g.shape[1]
    C = CHUNK
    NU = B * (W // QW)
    L = lb_param.shape[0]
    full = lambda shape: pl.BlockSpec(shape, lambda c: (0,) * len(shape))
    return pl.pallas_call(
        functools.partial(_mixers_kernel, layer, h_off),
        grid=(T // C,),
        in_specs=[
            pl.BlockSpec((B, C, 4 * W), lambda c: (0, c, main_blk)),
            pl.BlockSpec((B, C, LORA_BLK), lambda c: (0, c, lora_blk)),
            pl.BlockSpec((B, C, h_blk_w), lambda c: (0, c, h_blk)),
            full((1, W)), full((LANE, W)), full((1, W)), full((LORA_BLK, W)),
            full((1, W)), full((1, W)), full((1, W)), full((1, W)), full((1, W)),
            full((QW, QW)), full((L, HW)), full((1, HW)),
        ],
        out_specs=[pl.BlockSpec((B, C, W), lambda c: (0, c, 0)),
                   pl.BlockSpec((B, C, HW), lambda c: (0, c, 0))],
        out_shape=[jax.ShapeDtypeStruct((B, T, W), BF16),
                   jax.ShapeDtypeStruct((B, T, HW), BF16)],
        scratch_shapes=[
            pltpu.VMEM((NU, QW, QW), F32),
            pltpu.VMEM((NU * len(BD_ROLES), QW, QW), BF16),
            pltpu.VMEM((B * (HW // HGRN_HEAD), HGRN_HEAD, HGRN_HEAD), F32),
        ],
        compiler_params=pltpu.CompilerParams(
            dimension_semantics=("arbitrary",), vmem_limit_bytes=VMEM_LIMIT),
        name="mixers",
    )(p3, p3, p3, w0, w2p, a0, a2p, k_k, k_a, r_k, lnx_w, lnx_b, ones_bd, lb_param, hgrn_g)


def _outproj_kernel(final, yr_ref, yh_ref, x_ref, wr_ref, wh_ref, g_ref, o_ref):
    acc = jnp.dot(yr_ref[...], wr_ref[...], preferred_element_type=F32)
    acc += jnp.dot(yh_ref[...], wh_ref[...], preferred_element_type=F32)
    h = x_ref[...] + acc
    if final:
        ms = jnp.mean(h * h, axis=-1, keepdims=True)
        h = h * lax.rsqrt(ms + NORM_EPS) * g_ref[...]
    o_ref[...] = h


def _outproj(yr, yh, x2, wr, wh, g, tm, final):
    m, d = x2.shape
    wr_w = yr.shape[1]
    wh_w = yh.shape[1]
    return pl.pallas_call(
        functools.partial(_outproj_kernel, final),
        grid=(m // tm,),
        in_specs=[
            pl.BlockSpec((tm, wr_w), lambda i: (i, 0)),
            pl.BlockSpec((tm, wh_w), lambda i: (i, 0)),
            pl.BlockSpec((tm, d), lambda i: (i, 0)),
            pl.BlockSpec((wr_w, d), lambda i: (0, 0)),
            pl.BlockSpec((wh_w, d), lambda i: (0, 0)),
            pl.BlockSpec((1, d), lambda i: (0, 0)),
        ],
        out_specs=pl.BlockSpec((tm, d), lambda i: (i, 0)),
        out_shape=jax.ShapeDtypeStruct((m, d), F32),
        compiler_params=pltpu.CompilerParams(
            dimension_semantics=("arbitrary",), vmem_limit_bytes=VMEM_LIMIT),
        name="outproj",
    )(yr, yh, x2, wr, wh, g)


def kernel(x, norm_g, w_in, mu, w0, w2, a0, a2, k_k, k_a, r_k, lnx_w, lnx_b,
           hgrn_norm_g, lb_param, w_out, final_g):
    B, T, D = x.shape
    depth = w_in.shape[0]
    RW = w0.shape[1]
    HW = hgrn_norm_g.shape[1]
    n_r = 4 * RW + 2 * LORA
    tn = 768
    n_p = pl.cdiv(w_in.shape[2], tn) * tn
    h_start = (n_r // LANE) * LANE
    h_off = n_r - h_start
    h_blk_w = n_p - h_start
    assert (4 * RW) % LORA_BLK == 0 and 2 * LORA <= LORA_BLK and w2.shape[1] == LORA <= LANE
    assert h_start % h_blk_w == 0 and h_off + 4 * HW <= h_blk_w

    row = lambda z: z.reshape(1, -1).astype(F32)
    zr = lambda rows: jnp.zeros((rows, RW), F32)
    ones_bd = (jnp.arange(QW)[:, None] // RWKV_HEAD == jnp.arange(QW)[None, :] // RWKV_HEAD).astype(BF16)

    w_in_t = jnp.swapaxes(w_in, 1, 2)
    h = x.reshape(B * T, D)
    for l in range(depth):
        mu_full = jnp.concatenate([mu[l].astype(F32), jnp.zeros((n_p - n_r,), F32)]).reshape(1, -1)
        w2p = jnp.concatenate([w2[l].astype(F32), zr(LANE - LORA)], axis=0).astype(BF16)
        a2p = jnp.concatenate([zr(LORA), a2[l].astype(F32), zr(LORA_BLK - 2 * LORA)],
                              axis=0).astype(BF16)

        p = _inproj(h, row(norm_g[l]), w_in_t, l, mu_full, T, tm=1024, tn=tn, groups=4)
        p3 = p.reshape(B, T, n_p)
        y_r, y_h = _mixers(p3, 0, (4 * RW) // LORA_BLK, h_blk_w, h_start // h_blk_w, h_off, l,
                           row(w0[l]), w2p, row(a0[l]), a2p, row(k_k[l]), row(k_a[l]),
                           row(r_k[l]), row(lnx_w[l]), row(lnx_b[l]), ones_bd,
                           lb_param.astype(F32), row(hgrn_norm_g[l]))
        wo = w_out[l].astype(BF16)
        h = _outproj(y_r.reshape(B * T, RW), y_h.reshape(B * T, HW), h,
                     wo[:RW], wo[RW:], row(final_g), tm=512, final=(l == depth - 1))
    return h.reshape(B, T, D)
```

```python
import functools
import math

import jax
import jax.numpy as jnp
from jax import lax
from jax.experimental import pallas as pl
from jax.experimental.pallas import tpu as pltpu

F32 = jnp.float32
BF16 = jnp.bfloat16

NORM_EPS = 1e-6
LNX_EPS = 64e-5
RWKV_HEAD = 64
HGRN_HEAD = 128
LORA = 96
LORA_BLK = 256
LANE = 128
CHUNK = 64
QUAD = 4
QW = QUAD * RWKV_HEAD
BD_ROLES = ("B", "K", "V", "N", "Z", "U")
BATCH_SKEW = 0
UNIT_SKEW = 0
EXP_M05 = math.exp(-0.5)
VMEM_LIMIT = 48 * 1024 * 1024


def _mm(a, b):
    return jnp.dot(a.astype(BF16), b.astype(BF16), preferred_element_type=F32)


def _mm_nt(a, b):
    return lax.dot_general(a.astype(BF16), b.astype(BF16), (((1,), (1,)), ((), ())),
                           preferred_element_type=F32)


def _mm_tn(a, b):
    return lax.dot_general(a.astype(BF16), b.astype(BF16), (((0,), (0,)), ((), ())),
                           preferred_element_type=F32)


def _cumsum_rows(x, tril_bf16):
    hi = x.astype(BF16)
    lo = (x - hi.astype(F32)).astype(BF16)
    dot = functools.partial(jnp.dot, preferred_element_type=F32)
    return dot(tril_bf16, hi) + dot(tril_bf16, lo)


def _sigmoid(x):
    return 0.5 + 0.5 * jnp.tanh(0.5 * x)


def _inproj_kernel(n_valid, seq_len, x_ref, g_ref, w_ref, mu_ref, o_ref, hn_ref, carry_ref):
    q = pl.program_id(0)
    j = pl.program_id(1)
    i = pl.program_id(2)
    tm, tn = o_ref.shape

    @pl.when((q == 0) & (j == 0) & (i == 0))
    def _():
        carry_ref[...] = jnp.zeros_like(carry_ref)

    @pl.when(j == 0)
    def _():
        x = x_ref[...]
        ms = jnp.mean(x * x, axis=-1, keepdims=True)
        hn_ref[i] = (x * lax.rsqrt(ms + NORM_EPS) * g_ref[...]).astype(BF16)

    w_row = j * tn + lax.broadcasted_iota(jnp.int32, (tn, 1), 0)
    w = jnp.where(w_row < n_valid, w_ref[...], 0.0)
    acc = _mm_nt(hn_ref[i], w)

    first_row = (q * pl.num_programs(2) + i) * tm
    prev_last = jnp.where(first_row % seq_len == 0, 0.0, carry_ref[j])
    carry_ref[j] = acc[tm - 1:tm, :]
    is_row0 = lax.broadcasted_iota(jnp.int32, (tm, 1), 0) == 0
    prev = jnp.where(is_row0, prev_last, pltpu.roll(acc, 1, axis=0))
    o_ref[...] = acc + mu_ref[...] * (prev - acc)


def _inproj(x2, g, wt_all, layer, mu_full, seq_len, tm, tn, groups):
    m, d = x2.shape
    n_valid = wt_all.shape[1]
    n = pl.cdiv(n_valid, tn) * tn
    ni = m // (groups * tm)
    assert seq_len % tm == 0 and mu_full.shape == (1, n)
    x_map = lambda q, j, i: (q * ni + jnp.where(j == 0, i, ni - 1), 0)
    return pl.pallas_call(
        functools.partial(_inproj_kernel, n_valid, seq_len),
        grid=(groups, n // tn, ni),
        in_specs=[
            pl.BlockSpec((tm, d), x_map),
            pl.BlockSpec((1, d), lambda q, j, i: (0, 0)),
            pl.BlockSpec((None, tn, d), lambda q, j, i: (layer, j, 0)),
            pl.BlockSpec((1, tn), lambda q, j, i: (0, j)),
        ],
        out_specs=pl.BlockSpec((tm, tn), lambda q, j, i: (q * ni + i, j)),
        out_shape=jax.ShapeDtypeStruct((m, n), F32),
        scratch_shapes=[pltpu.VMEM((ni, tm, d), BF16), pltpu.VMEM((n // tn, 1, tn), F32)],
        compiler_params=pltpu.CompilerParams(
            dimension_semantics=("arbitrary", "arbitrary", "arbitrary"),
            vmem_limit_bytes=VMEM_LIMIT),
        name="inproj",
    )(x2, g, wt_all, mu_full)


def _rwkv_stages(pm_ref, pl_ref, w0_ref, w2_ref, a0_ref, a2_ref,
                 kk_ref, ka_ref, rk_ref, lw_ref, lb_ref, ones_ref,
                 o_ref, state, bd_ref):
    c = pl.program_id(0)
    C = CHUNK
    NB = pm_ref.shape[0]
    W = w0_ref.shape[1]
    NQ = W // QW

    t_i = lax.broadcasted_iota(jnp.int32, (C, C), 0)
    s_i = lax.broadcasted_iota(jnp.int32, (C, C), 1)
    tril = (t_i >= s_i).astype(BF16)
    ones_bd = ones_ref[...]
    dot = functools.partial(jnp.dot, preferred_element_type=F32)

    lane = lax.broadcasted_iota(jnp.int32, (C, QW), 1)
    lane_s = lane % RWKV_HEAD
    trow = lax.broadcasted_iota(jnp.int32, (C, QW), 0)
    strict = lane_s < trow
    incl = lane_s <= trow
    eye_q = jnp.where(lane_s == trow, 1.0, 0.0).astype(F32)
    bd_mask = (lax.broadcasted_iota(jnp.int32, (QW, QW), 0) // RWKV_HEAD
               == lax.broadcasted_iota(jnp.int32, (QW, QW), 1) // RWKV_HEAD)

    def bd(u, role, x):
        slot = u["ui"] * len(BD_ROLES) + BD_ROLES.index(role)
        xb = x.astype(BF16)
        for h in range(QUAD):
            hs = slice(h * RWKV_HEAD, (h + 1) * RWKV_HEAD)
            bd_ref[slot, h * C:(h + 1) * C, hs] = xb[:, hs]
        return bd_ref[slot]

    def segsum_all(xs):
        out = dot(jnp.concatenate([x.astype(BF16) for x in xs], axis=0), ones_bd)
        return [out[i * C:(i + 1) * C] for i in range(len(xs))]

    xl = jnp.concatenate([pl_ref[b] for b in range(NB)], axis=0)
    z = w0_ref[...] + _mm(jnp.tanh(xl[:, :LANE]), w2_ref[...])
    ld_all = -EXP_M05 * _sigmoid(z)
    a_all = _sigmoid(a0_ref[...] + _mm(xl, a2_ref[...]))
    units = []
    for b in range(NB):
        r = pm_ref[b, :, 0 * W:1 * W]
        k = pm_ref[b, :, 1 * W:2 * W]
        v = pm_ref[b, :, 2 * W:3 * W]
        gate = pm_ref[b, :, 3 * W:4 * W]
        ld = ld_all[b * C:(b + 1) * C]
        a = a_all[b * C:(b + 1) * C]
        G = _cumsum_rows(ld, tril)
        Gx = G - ld
        GC = G[C - 1:C, :]
        for q in range(NQ):
            sl = slice(q * QW, (q + 1) * QW)
            units.append(dict(b=b, ui=b * NQ + q, sl=sl, r=r[:, sl], k=k[:, sl], v=v[:, sl],
                              g=gate[:, sl], a=a[:, sl], G=G[:, sl], Gx=Gx[:, sl], GC=GC[:, sl]))
    yield

    for u in units:
        u["kk"] = u["k"] * kk_ref[:, u["sl"]]
    for u, n2 in zip(units, segsum_all([u["kk"] * u["kk"] for u in units])):
        u["n2"] = n2
    yield
    def unit_chain(u):
        sl = u["sl"]
        kkn = u["kk"] * lax.rsqrt(jnp.maximum(u["n2"], 1e-24))
        k2 = u["k"] * (1.0 + (u["a"] - 1.0) * ka_ref[:, sl])
        av = -kkn
        bv = kkn * u["a"]
        einv = jnp.exp(-u["G"])
        eC = jnp.exp(u["GC"] - u["G"])
        u["rkk"] = u["r"] * k2 * rk_ref[:, sl]
        BDB = bd(u, "B", bv * einv)
        BDK = bd(u, "K", k2 * einv)
        BKe = jnp.concatenate([bv * eC, k2 * eC], axis=0).astype(BF16)
        BDV = bd(u, "V", u["v"])
        lhs = jnp.concatenate([av * jnp.exp(u["Gx"]), u["r"] * jnp.exp(u["G"])],
                              axis=0).astype(BF16)
        pc_col = jnp.transpose(jnp.broadcast_to(jnp.exp(u["GC"]), (LANE, QW)))
        PCc = jnp.concatenate([pc_col] * (QW // LANE), axis=1)
        yield
        SB = _mm_nt(lhs, BDB)
        yield
        SK = _mm_nt(lhs, BDK)
        yield
        N = jnp.where(strict, SB[:C], 0.0)
        Lrb = jnp.where(incl, SB[C:], 0.0)
        P = eye_q + N
        Np = _mm(N, bd(u, "N", N))
        yield
        akl = jnp.concatenate([jnp.where(strict, SK[:C], 0.0),
                               jnp.where(incl, SK[C:], 0.0)], axis=0)
        both = _mm(akl, BDV)
        yield
        for _ in range(4):
            out = _mm(jnp.concatenate([P, Np], axis=0), bd(u, "N", Np))
            P = P + out[:C]
            Np = out[C:]
            yield
        P = P + _mm(P, bd(u, "N", Np))
        yield
        H = state[u["ui"]]
        zy = _mm(lhs, H)
        yield
        U = _mm(P, bd(u, "Z", zy[:C] + both[:C]))
        yield
        u["Y"] = zy[C:] + _mm(Lrb, bd(u, "U", U)) + both[C:]
        yield
        upd = _mm_tn(BKe, jnp.concatenate([U, u["v"]], axis=0))
        state[u["ui"]] = H * PCc + jnp.where(bd_mask, upd, 0.0)

    chains = [unit_chain(u) for u in units]
    live = list(range(len(chains)))
    rnd = 0
    while live:
        for i in list(live):
            start = (i // NQ) * BATCH_SKEW + (i % NQ) * UNIT_SKEW
            if rnd >= start and next(chains[i], "done") == "done":
                live.remove(i)
        rnd += 1
        yield

    for u, s in zip(units, segsum_all([u["Y"] for u in units])):
        u["yc"] = u["Y"] - s * (1.0 / RWKV_HEAD)
    yield
    for u, s in zip(units, segsum_all([u["yc"] * u["yc"] for u in units])):
        u["var"] = s * (1.0 / RWKV_HEAD)
    yield
    for u, s in zip(units, segsum_all([u["rkk"] for u in units])):
        u["bonus"] = s
    yield
    for u in units:
        sl = u["sl"]
        yn = u["yc"] * lax.rsqrt(u["var"] + LNX_EPS) * lw_ref[:, sl] + lb_ref[:, sl]
        res = (yn + u["bonus"] * u["v"]) * (u["g"] * _sigmoid(u["g"]))
        o_ref[u["b"], :, sl] = res.astype(o_ref.dtype)


def _hgrn_stages(layer, off, ph_ref, lbp_ref, ng_ref, o_ref, state):
    c = pl.program_id(0)
    C = CHUNK
    NB = ph_ref.shape[0]
    W = ng_ref.shape[1]
    D = HGRN_HEAD
    NH = W // D

    lp = lbp_ref[...]
    e = jnp.exp(lp - jnp.max(lp, axis=0, keepdims=True))
    lb = jnp.sum(e[:layer + 1], axis=0, keepdims=True) / jnp.sum(e, axis=0, keepdims=True)

    t_i = lax.broadcasted_iota(jnp.int32, (C, C), 0)
    s_i = lax.broadcasted_iota(jnp.int32, (C, C), 1)
    causal = t_i >= s_i
    tril = causal.astype(BF16)

    units = []
    for b in range(NB):
        x = ph_ref[b]
        qv = x[:, off + 0 * W:off + 1 * W]
        fr = x[:, off + 1 * W:off + 2 * W]
        iv = x[:, off + 2 * W:off + 3 * W]
        gate = x[:, off + 3 * W:off + 4 * W]
        f = lb + (1.0 - lb) * _sigmoid(fr)
        kx = 1.0 - f
        G = _cumsum_rows(jnp.log(f), tril)
        GC = G[C - 1:C, :]
        qt = (qv * jnp.exp(G)).astype(BF16)
        kb = (kx * jnp.exp(-G)).astype(BF16)
        kd = (kx * jnp.exp(GC - G)).astype(BF16)
        PC = jnp.exp(GC)
        ivb = iv.astype(BF16)
        for h in range(NH):
            sl = slice(h * D, (h + 1) * D)
            units.append(dict(b=b, h=h, sl=sl, qt=qt[:, sl], kb=kb[:, sl], kd=kd[:, sl],
                              iv=ivb[:, sl], PC=PC[:, sl], g=gate[:, sl]))
        yield

    halves = (units[:len(units) // 2], units[len(units) // 2:])
    for part in halves:
        for u in part:
            u["A"] = _mm_nt(u["qt"], u["kb"])
        yield
    for part in halves:
        for u in part:
            u["S"] = state[u["b"] * NH + u["h"]]
            u["inter"] = _mm_nt(u["qt"], u["S"])
        yield
    for part in halves:
        for u in part:
            u["o"] = _mm(jnp.where(causal, u["A"], 0.0), u["iv"]) + u["inter"]
        yield
    for part in halves:
        for u in part:
            state[u["b"] * NH + u["h"]] = u["S"] * u["PC"] + _mm_tn(u["iv"], u["kd"])
        yield
    for part in halves:
        for u in part:
            o = u["o"]
            g = u["g"]
            ms = jnp.mean(o * o, axis=-1, keepdims=True)
            res = o * lax.rsqrt(ms + NORM_EPS) * ng_ref[:, u["sl"]] * (g * _sigmoid(g))
            o_ref[u["b"], :, u["sl"]] = res.astype(o_ref.dtype)
        yield


HGRN_AFTER_RWKV_STAGE = tuple(range(8, 20))


def _mixers_kernel(layer, off, pm_ref, pl_ref, ph_ref, w0_ref, w2_ref, a0_ref, a2_ref,
                   kk_ref, ka_ref, rk_ref, lw_ref, lb_ref, ones_ref, lbp_ref, ng_ref,
                   or_ref, oh_ref, r_state, bd_ref, h_state):
    @pl.when(pl.program_id(0) == 0)
    def _():
        for ref in (r_state, bd_ref, h_state):
            ref[...] = jnp.zeros_like(ref)

    rw =_rwkv_stages(pm_ref, pl_ref, w0_ref, w2_ref, a0_ref, a2_ref, kk_ref, ka_ref, rk_ref,
                      lw_ref, lb_ref, ones_ref, or_ref, r_state, bd_ref)
    hg = _hgrn_stages(layer, off, ph_ref, lbp_ref, ng_ref, oh_ref, h_state)
    for stage, _ in enumerate(rw, start=1):
        for _ in range(HGRN_AFTER_RWKV_STAGE.count(stage)):
            next(hg, None)
    for _ in hg:
        pass


def _mixers(p3, main_blk, lora_blk, h_blk_w, h_blk, h_off, layer,
            w0, w2p, a0, a2p, k_k, k_a, r_k, lnx_w, lnx_b, ones_bd, lb_param, hgrn_g):
    B, T, _ = p3.shape
    W = w0.shape[1]
    HW = hgrn_g.shape[1]
    C = CHUNK
    NU = B * (W // QW)
    L = lb_param.shape[0]
    full = lambda shape: pl.BlockSpec(shape, lambda c: (0,) * len(shape))
    return pl.pallas_call(
        functools.partial(_mixers_kernel, layer, h_off),
        grid=(T // C,),
        in_specs=[
            pl.BlockSpec((B, C, 4 * W), lambda c: (0, c, main_blk)),
            pl.BlockSpec((B, C, LORA_BLK), lambda c: (0, c, lora_blk)),
            pl.BlockSpec((B, C, h_blk_w), lambda c: (0, c, h_blk)),
            full((1, W)), full((LANE, W)), full((1, W)), full((LORA_BLK, W)),
            full((1, W)), full((1, W)), full((1, W)), full((1, W)), full((1, W)),
            full((QW, QW)), full((L, HW)), full((1, HW)),
        ],
        out_specs=[pl.BlockSpec((B, C, W), lambda c: (0, c, 0)),
                   pl.BlockSpec((B, C, HW), lambda c: (0, c, 0))],
        out_shape=[jax.ShapeDtypeStruct((B, T, W), BF16),
                   jax.ShapeDtypeStruct((B, T, HW), BF16)],
        scratch_shapes=[
            pltpu.VMEM((NU, QW, QW), F32),
            pltpu.VMEM((NU * len(BD_ROLES), QW, QW), BF16),
            pltpu.VMEM((B * (HW // HGRN_HEAD), HGRN_HEAD, HGRN_HEAD), F32),
        ],
        compiler_params=pltpu.CompilerParams(
            dimension_semantics=("arbitrary",), vmem_limit_bytes=VMEM_LIMIT),
        name="mixers",
    )(p3, p3, p3, w0, w2p, a0, a2p, k_k, k_a, r_k, lnx_w, lnx_b, ones_bd, lb_param, hgrn_g)


def _outproj_kernel(final, yr_ref, yh_ref, x_ref, wr_ref, wh_ref, g_ref, o_ref):
    acc = jnp.dot(yr_ref[...], wr_ref[...], preferred_element_type=F32)
    acc += jnp.dot(yh_ref[...], wh_ref[...], preferred_element_type=F32)
    h = x_ref[...] + acc
    if final:
        ms = jnp.mean(h * h, axis=-1, keepdims=True)
        h = h * lax.rsqrt(ms + NORM_EPS) * g_ref[...]
    o_ref[...] = h


def _outproj(yr, yh, x2, wr, wh, g, tm, final):
    m, d = x2.shape
    wr_w = yr.shape[1]
    wh_w = yh.shape[1]
    return pl.pallas_call(
        functools.partial(_outproj_kernel, final),
        grid=(m // tm,),
        in_specs=[
            pl.BlockSpec((tm, wr_w), lambda i: (i, 0)),
            pl.BlockSpec((tm, wh_w), lambda i: (i, 0)),
            pl.BlockSpec((tm, d), lambda i: (i, 0)),
            pl.BlockSpec((wr_w, d), lambda i: (0, 0)),
            pl.BlockSpec((wh_w, d), lambda i: (0, 0)),
            pl.BlockSpec((1, d), lambda i: (0, 0)),
        ],
        out_specs=pl.BlockSpec((tm, d), lambda i: (i, 0)),
        out_shape=jax.ShapeDtypeStruct((m, d), F32),
        compiler_params=pltpu.CompilerParams(
            dimension_semantics=("arbitrary",), vmem_limit_bytes=VMEM_LIMIT),
        name="outproj",
    )(yr, yh, x2, wr, wh, g)


def kernel(x, norm_g, w_in, mu, w0, w2, a0, a2, k_k, k_a, r_k, lnx_w, lnx_b,
           hgrn_norm_g, lb_param, w_out, final_g):
    B, T, D = x.shape
    depth = w_in.shape[0]
    RW = w0.shape[1]
    HW = hgrn_norm_g.shape[1]
    n_r = 4 * RW + 2 * LORA
    tn = 768
    n_p = pl.cdiv(w_in.shape[2], tn) * tn
    h_start = (n_r // LANE) * LANE
    h_off = n_r - h_start
    h_blk_w = n_p - h_start
    assert (4 * RW) % LORA_BLK == 0 and 2 * LORA <= LORA_BLK and w2.shape[1] == LORA <= LANE
    assert h_start % h_blk_w == 0 and h_off + 4 * HW <= h_blk_w

    row = lambda z: z.reshape(1, -1).astype(F32)
    zr = lambda rows: jnp.zeros((rows, RW), F32)
    ones_bd = (jnp.arange(QW)[:, None] // RWKV_HEAD == jnp.arange(QW)[None, :] // RWKV_HEAD).astype(BF16)

    w_in_t = jnp.swapaxes(w_in, 1, 2)
    h = x.reshape(B * T, D)
    for l in range(depth):
        mu_full = jnp.concatenate([mu[l].astype(F32), jnp.zeros((n_p - n_r,), F32)]).reshape(1, -1)
        w2p = jnp.concatenate([w2[l].astype(F32), zr(LANE - LORA)], axis=0).astype(BF16)
        a2p = jnp.concatenate([zr(LORA), a2[l].astype(F32), zr(LORA_BLK - 2 * LORA)],
                              axis=0).astype(BF16)

        p = _inproj(h, row(norm_g[l]), w_in_t, l, mu_full, T, tm=1024, tn=tn, groups=4)
        p3 = p.reshape(B, T, n_p)
        y_r, y_h = _mixers(p3, 0, (4 * RW) // LORA_BLK, h_blk_w, h_start // h_blk_w, h_off, l,
                           row(w0[l]), w2p, row(a0[l]), a2p, row(k_k[l]), row(k_a[l]),
                           row(r_k[l]), row(lnx_w[l]), row(lnx_b[l]), ones_bd,
                           lb_param.astype(F32), row(hgrn_norm_g[l]))
        wo = w_out[l].astype(BF16)
        h = _outproj(y_r.reshape(B * T, RW), y_h.reshape(B * T, HW), h,
                     wo[:RW], wo[RW:], row(final_g), tm=512, final=(l == depth - 1))
    return h.reshape(B, T, D)
```

```python
import functools
import math

import jax
import jax.numpy as jnp
from jax import lax
from jax.experimental import pallas as pl
from jax.experimental.pallas import tpu as pltpu

F32 = jnp.float32
BF16 = jnp.bfloat16

NORM_EPS = 1e-6
LNX_EPS = 64e-5
RWKV_HEAD = 64
HGRN_HEAD = 128
LORA = 96
LORA_BLK = 256
LANE = 128
CHUNK = 64
HGRN_SUB = 16
QUAD = 4
QW = QUAD * RWKV_HEAD
BD_ROLES = ("B", "K", "V", "N", "Z", "U")
HGRN_AFTER_RWKV_STAGE = tuple(range(8, 20))
EXP_M05 = math.exp(-0.5)
VMEM_LIMIT = 48 * 1024 * 1024


def _mm(a, b):
    return jnp.dot(a.astype(BF16), b.astype(BF16), preferred_element_type=F32)


def _mm_nt(a, b):
    return lax.dot_general(a.astype(BF16), b.astype(BF16), (((1,), (1,)), ((), ())),
                           preferred_element_type=F32)


def _mm_tn(a, b):
    return lax.dot_general(a.astype(BF16), b.astype(BF16), (((0,), (0,)), ((), ())),
                           preferred_element_type=F32)


def _cumsum_rows(x, tril_bf16):
    hi = x.astype(BF16)
    lo = (x - hi.astype(F32)).astype(BF16)
    dot = functools.partial(jnp.dot, preferred_element_type=F32)
    return dot(tril_bf16, hi) + dot(tril_bf16, lo)


def _sigmoid(x):
    return 0.5 + 0.5 * jnp.tanh(0.5 * x)


def _inproj_kernel(n_valid, seq_len, x_ref, g_ref, w_ref, mu_ref, o_ref, hn_ref, carry_ref):
    q = pl.program_id(0)
    j = pl.program_id(1)
    i = pl.program_id(2)
    tm, tn = o_ref.shape

    @pl.when((q == 0) & (j == 0) & (i == 0))
    def _():
        carry_ref[...] = jnp.zeros_like(carry_ref)

    @pl.when(j == 0)
    def _():
        x = x_ref[...]
        ms = jnp.mean(x * x, axis=-1, keepdims=True)
        hn_ref[i] = (x * lax.rsqrt(ms + NORM_EPS) * g_ref[...]).astype(BF16)

    w_row = j * tn + lax.broadcasted_iota(jnp.int32, (tn, 1), 0)
    w = jnp.where(w_row < n_valid, w_ref[...], 0.0)
    acc = _mm_nt(hn_ref[i], w)

    first_row = (q * pl.num_programs(2) + i) * tm
    prev_last = jnp.where(first_row % seq_len == 0, 0.0, carry_ref[j])
    carry_ref[j] = acc[tm - 1:tm, :]
    is_row0 = lax.broadcasted_iota(jnp.int32, (tm, 1), 0) == 0
    prev = jnp.where(is_row0, prev_last, pltpu.roll(acc, 1, axis=0))
    o_ref[...] = acc + mu_ref[...] * (prev - acc)


def _inproj(x2, g, wt_all, layer, mu_full, seq_len, tm, tn, groups):
    m, d = x2.shape
    n_valid = wt_all.shape[1]
    n = pl.cdiv(n_valid, tn) * tn
    ni = m // (groups * tm)
    assert seq_len % tm == 0 and mu_full.shape == (1, n)
    x_map = lambda q, j, i: (q * ni + jnp.where(j == 0, i, ni - 1), 0)
    return pl.pallas_call(
        functools.partial(_inproj_kernel, n_valid, seq_len),
        grid=(groups, n // tn, ni),
        in_specs=[
            pl.BlockSpec((tm, d), x_map),
            pl.BlockSpec((1, d), lambda q, j, i: (0, 0)),
            pl.BlockSpec((None, tn, d), lambda q, j, i: (layer, j, 0)),
            pl.BlockSpec((1, tn), lambda q, j, i: (0, j)),
        ],
        out_specs=pl.BlockSpec((tm, tn), lambda q, j, i: (q * ni + i, j)),
        out_shape=jax.ShapeDtypeStruct((m, n), F32),
        scratch_shapes=[pltpu.VMEM((ni, tm, d), BF16), pltpu.VMEM((n // tn, 1, tn), F32)],
        compiler_params=pltpu.CompilerParams(
            dimension_semantics=("arbitrary", "arbitrary", "arbitrary"),
            vmem_limit_bytes=VMEM_LIMIT),
        name="inproj",
    )(x2, g, wt_all, mu_full)


def _rwkv_stages(pm_ref, pl_ref, w0_ref, w2_ref, a0_ref, a2_ref,
                 kk_ref, ka_ref, rk_ref, lw_ref, lb_ref, ones_ref,
                 o_ref, state, bd_ref):
    C = CHUNK
    NB = pm_ref.shape[0]
    W = w0_ref.shape[1]
    NQ = W // QW

    t_i = lax.broadcasted_iota(jnp.int32, (C, C), 0)
    s_i = lax.broadcasted_iota(jnp.int32, (C, C), 1)
    tril = (t_i >= s_i).astype(BF16)
    ones_bd = ones_ref[...]
    dot = functools.partial(jnp.dot, preferred_element_type=F32)

    lane = lax.broadcasted_iota(jnp.int32, (C, QW), 1)
    lane_s = lane % RWKV_HEAD
    trow = lax.broadcasted_iota(jnp.int32, (C, QW), 0)
    strict = lane_s < trow
    incl = lane_s <= trow
    eye_q = jnp.where(lane_s == trow, 1.0, 0.0).astype(F32)
    bd_mask = (lax.broadcasted_iota(jnp.int32, (QW, QW), 0) // RWKV_HEAD
               == lax.broadcasted_iota(jnp.int32, (QW, QW), 1) // RWKV_HEAD)

    def bd(u, role, x):
        slot = u["ui"] * len(BD_ROLES) + BD_ROLES.index(role)
        xb = x.astype(BF16)
        for h in range(QUAD):
            hs = slice(h * RWKV_HEAD, (h + 1) * RWKV_HEAD)
            bd_ref[slot, h * C:(h + 1) * C, hs] = xb[:, hs]
        return bd_ref[slot]

    def segsum_all(xs):
        out = dot(jnp.concatenate([x.astype(BF16) for x in xs], axis=0), ones_bd)
        return [out[i * C:(i + 1) * C] for i in range(len(xs))]

    xl = jnp.concatenate([pl_ref[b] for b in range(NB)], axis=0)
    z = w0_ref[...] + _mm(jnp.tanh(xl[:, :LANE]), w2_ref[...])
    ld_all = -EXP_M05 * _sigmoid(z)
    a_all = _sigmoid(a0_ref[...] + _mm(xl, a2_ref[...]))
    units = []
    for b in range(NB):
        r = pm_ref[b, :, 0 * W:1 * W]
        k = pm_ref[b, :, 1 * W:2 * W]
        v = pm_ref[b, :, 2 * W:3 * W]
        gate = pm_ref[b, :, 3 * W:4 * W]
        ld = ld_all[b * C:(b + 1) * C]
        a = a_all[b * C:(b + 1) * C]
        G = _cumsum_rows(ld, tril)
        Gx = G - ld
        GC = G[C - 1:C, :]
        for q in range(NQ):
            sl = slice(q * QW, (q + 1) * QW)
            units.append(dict(b=b, ui=b * NQ + q, sl=sl, r=r[:, sl], k=k[:, sl], v=v[:, sl],
                              g=gate[:, sl], a=a[:, sl], G=G[:, sl], Gx=Gx[:, sl], GC=GC[:, sl]))
    yield

    for u in units:
        u["kk"] = u["k"] * kk_ref[:, u["sl"]]
    for u, n2 in zip(units, segsum_all([u["kk"] * u["kk"] for u in units])):
        u["n2"] = n2
    yield

    def unit_chain(u):
        sl = u["sl"]
        kkn = u["kk"] * lax.rsqrt(jnp.maximum(u["n2"], 1e-24))
        k2 = u["k"] * (1.0 + (u["a"] - 1.0) * ka_ref[:, sl])
        av = -kkn
        bv = kkn * u["a"]
        einv = jnp.exp(-u["G"])
        eC = jnp.exp(u["GC"] - u["G"])
        u["rkk"] = u["r"] * k2 * rk_ref[:, sl]
        BDB = bd(u, "B", bv * einv)
        BDK = bd(u, "K", k2 * einv)
        BKe = jnp.concatenate([bv * eC, k2 * eC], axis=0).astype(BF16)
        BDV = bd(u, "V", u["v"])
        lhs = jnp.concatenate([av * jnp.exp(u["Gx"]), u["r"] * jnp.exp(u["G"])],
                              axis=0).astype(BF16)
        pc_col = jnp.transpose(jnp.broadcast_to(jnp.exp(u["GC"]), (LANE, QW)))
        PCc = jnp.concatenate([pc_col] * (QW // LANE), axis=1)
        yield
        SB = _mm_nt(lhs, BDB)
        yield
        SK = _mm_nt(lhs, BDK)
        yield
        N = jnp.where(strict, SB[:C], 0.0)
        Lrb = jnp.where(incl, SB[C:], 0.0)
        P = eye_q + N
        Np = _mm(N, bd(u, "N", N))
        yield
        akl = jnp.concatenate([jnp.where(strict, SK[:C], 0.0),
                               jnp.where(incl, SK[C:], 0.0)], axis=0)
        both = _mm(akl, BDV)
        yield
        for _ in range(4):
            out = _mm(jnp.concatenate([P, Np], axis=0), bd(u, "N", Np))
            P = P + out[:C]
            Np = out[C:]
            yield
        P = P + _mm(P, bd(u, "N", Np))
        yield
        H = state[u["ui"]]
        zy = _mm(lhs, H)
        yield
        U = _mm(P, bd(u, "Z", zy[:C] + both[:C]))
        yield
        u["Y"] = zy[C:] + _mm(Lrb, bd(u, "U", U)) + both[C:]
        yield
        upd = _mm_tn(BKe, jnp.concatenate([U, u["v"]], axis=0))
        state[u["ui"]] = H * PCc + jnp.where(bd_mask, upd, 0.0)

    chains = [unit_chain(u) for u in units]
    live = list(range(len(chains)))
    while live:
        for i in list(live):
            if next(chains[i], "done") == "done":
                live.remove(i)
        yield

    for u, s in zip(units, segsum_all([u["Y"] for u in units])):
        u["yc"] = u["Y"] - s * (1.0 / RWKV_HEAD)
    yield
    for u, s in zip(units, segsum_all([u["yc"] * u["yc"] for u in units])):
        u["var"] = s * (1.0 / RWKV_HEAD)
    yield
    for u, s in zip(units, segsum_all([u["rkk"] for u in units])):
        u["bonus"] = s
    yield
    for u in units:
        sl = u["sl"]
        yn = u["yc"] * lax.rsqrt(u["var"] + LNX_EPS) * lw_ref[:, sl] + lb_ref[:, sl]
        res = (yn + u["bonus"] * u["v"]) * (u["g"] * _sigmoid(u["g"]))
        o_ref[u["b"], :, sl] = res.astype(o_ref.dtype)


def _hgrn_stages(layer, off, ph_ref, lbp_ref, ng_ref, o_ref, state):
    C = CHUNK
    NB = ph_ref.shape[0]
    W = ng_ref.shape[1]
    D = HGRN_HEAD
    NH = W // D
    SUB = HGRN_SUB
    NBLK = C // SUB
    GRP = 2 * LANE // D

    lp = lbp_ref[...]
    e = jnp.exp(lp - jnp.max(lp, axis=0, keepdims=True))
    lb = jnp.sum(e[:layer + 1], axis=0, keepdims=True) / jnp.sum(e, axis=0, keepdims=True)

    t_i = lax.broadcasted_iota(jnp.int32, (C, C), 0)
    s_i = lax.broadcasted_iota(jnp.int32, (C, C), 1)
    causal = t_i >= s_i
    tril = causal.astype(BF16)

    units = []
    for b in range(NB):
        x = ph_ref[b]
        qv = x[:, off + 0 * W:off + 1 * W]
        fr = x[:, off + 1 * W:off + 2 * W]
        iv = x[:, off + 2 * W:off + 3 * W]
        gate = x[:, off + 3 * W:off + 4 * W]
        f = lb + (1.0 - lb) * _sigmoid(fr)
        kx = 1.0 - f
        G = _cumsum_rows(jnp.log(f), tril)
        GC = G[C - 1:C, :]
        kd = (kx * jnp.exp(GC - G)).astype(BF16)
        PC = jnp.exp(GC)
        ivb = iv.astype(BF16)
        zero = jnp.zeros((SUB, W), BF16)
        qs, ks, qt_rows, k_rows = [], [], [], []
        for j in range(NBLK):
            lo, hi = j * SUB, (j + 1) * SUB
            qj = qv[lo:hi] * jnp.exp(G[lo:hi] - G[lo - 1:lo, :]) if j else qv[lo:hi] * jnp.exp(G[lo:hi])
            kj = kx[lo:hi] * jnp.exp(G[lo - 1:lo, :] - G[lo:hi]) if j else kx[lo:hi] * jnp.exp(-G[lo:hi])
            if j:
                step = jnp.exp(G[lo - 1:lo, :] - (G[lo - SUB - 1:lo - SUB, :] if j > 1 else 0.0))
                k_rows = [kr * step for kr in k_rows]
                qt_rows.append(qj * jnp.exp(G[lo - 1:lo, :]))
            else:
                qt_rows.append(qj)
            k_rows.append(kj)
            g0 = (j // GRP) * GRP
            qs.append(jnp.concatenate([zero] * (j - g0) + [qj.astype(BF16)] + [zero] * (g0 + GRP - 1 - j),
                                      axis=0))
            ks.append(jnp.concatenate([kr.astype(BF16) for kr in k_rows] + [zero] * (g0 + GRP - 1 - j),
                                      axis=0))
        qt = jnp.concatenate(qt_rows, axis=0).astype(BF16)
        for h in range(NH):
            sl = slice(h * D, (h + 1) * D)
            units.append(dict(b=b, h=h, sl=sl, qt=qt[:, sl], kd=kd[:, sl], iv=ivb[:, sl],
                              PC=PC[:, sl], g=gate[:, sl],
                              qcat=[jnp.concatenate([q[:, sl] for q in qs[g:g + GRP]], axis=1)
                                    for g in range(0, NBLK, GRP)],
                              kcat=[jnp.concatenate([k[:, sl] for k in ks[g:g + GRP]], axis=1)
                                    for g in range(0, NBLK, GRP)]))
        yield

    halves = (units[:len(units) // 2], units[len(units) // 2:])
    for part in halves:
        for u in part:
            u["A"] = [_mm_nt(q, k) for q, k in zip(u["qcat"], u["kcat"])]
        yield
    for part in halves:
        for u in part:
            u["S"] = state[u["b"] * NH + u["h"]]
            u["inter"] = _mm_nt(u["qt"], u["S"])
        yield
    for part in halves:
        for u in part:
            rows = []
            for g, A in enumerate(u["A"]):
                hi = (g + 1) * GRP * SUB
                mask = causal[hi - GRP * SUB:hi, :hi]
                rows.append(_mm(jnp.where(mask, A, 0.0), u["iv"][:hi]))
            u["o"] = jnp.concatenate(rows, axis=0) + u["inter"]
        yield
    for part in halves:
        for u in part:
            state[u["b"] * NH + u["h"]] = u["S"] * u["PC"] + _mm_tn(u["iv"], u["kd"])
        yield
    for part in halves:
        for u in part:
            o = u["o"]
            g = u["g"]
            ms = jnp.mean(o * o, axis=-1, keepdims=True)
            res = o * lax.rsqrt(ms + NORM_EPS) * ng_ref[:, u["sl"]] * (g * _sigmoid(g))
            o_ref[u["b"], :, u["sl"]] = res.astype(o_ref.dtype)
        yield


def _mixers_kernel(layer, off, pm_ref, pl_ref, ph_ref, w0_ref, w2_ref, a0_ref, a2_ref,
                   kk_ref, ka_ref, rk_ref, lw_ref, lb_ref, ones_ref, lbp_ref, ng_ref,
                   or_ref, oh_ref, r_state, bd_ref, h_state):
    @pl.when(pl.program_id(0) == 0)
    def _():
        for ref in (r_state, bd_ref, h_state):
            ref[...] = jnp.zeros_like(ref)

    rw = _rwkv_stages(pm_ref, pl_ref, w0_ref, w2_ref, a0_ref, a2_ref, kk_ref, ka_ref, rk_ref,
                      lw_ref, lb_ref, ones_ref, or_ref, r_state, bd_ref)
    hg = _hgrn_stages(layer, off, ph_ref, lbp_ref, ng_ref, oh_ref, h_state)
    for stage, _ in enumerate(rw, start=1):
        for _ in range(HGRN_AFTER_RWKV_STAGE.count(stage)):
            next(hg, None)
    for _ in hg:
        pass


def _mixers(p3, main_blk, lora_blk, h_blk_w, h_blk, h_off, layer,
            w0, w2p, a0, a2p, k_k, k_a, r_k, lnx_w, lnx_b, ones_bd, lb_param, hgrn_g):
    B, T, _ = p3.shape
    W = w0.shape[1]
    HW = hgrn_g.shape[1]
    C = CHUNK
    NU = B * (W // QW)
    L = lb_param.shape[0]
    full = lambda shape: pl.BlockSpec(shape, lambda c: (0,) * len(shape))
    return pl.pallas_call(
        functools.partial(_mixers_kernel, layer, h_off),
        grid=(T // C,),
        in_specs=[
            pl.BlockSpec((B, C, 4 * W), lambda c: (0, c, main_blk)),
            pl.BlockSpec((B, C, LORA_BLK), lambda c: (0, c, lora_blk)),
            pl.BlockSpec((B, C, h_blk_w), lambda c: (0, c, h_blk)),
            full((1, W)), full((LANE, W)), full((1, W)), full((LORA_BLK, W)),
            full((1, W)), full((1, W)), full((1, W)), full((1, W)), full((1, W)),
            full((QW, QW)), full((L, HW)), full((1, HW)),
        ],
        out_specs=[pl.BlockSpec((B, C, W), lambda c: (0, c, 0)),
                   pl.BlockSpec((B, C, HW), lambda c: (0, c, 0))],
        out_shape=[jax.ShapeDtypeStruct((B, T, W), BF16),
                   jax.ShapeDtypeStruct((B, T, HW), BF16)],
        scratch_shapes=[
            pltpu.VMEM((NU, QW, QW), F32),
            pltpu.VMEM((NU * len(BD_ROLES), QW, QW), BF16),
            pltpu.VMEM((B * (HW // HGRN_HEAD), HGRN_HEAD, HGRN_HEAD), F32),
        ],
        compiler_params=pltpu.CompilerParams(
            dimension_semantics=("arbitrary",), vmem_limit_bytes=VMEM_LIMIT),
        name="mixers",
    )(p3, p3, p3, w0, w2p, a0, a2p, k_k, k_a, r_k, lnx_w, lnx_b, ones_bd, lb_param, hgrn_g)


def _outproj_kernel(final, yr_ref, yh_ref, x_ref, wr_ref, wh_ref, g_ref, o_ref):
    acc = jnp.dot(yr_ref[...], wr_ref[...], preferred_element_type=F32)
    acc += jnp.dot(yh_ref[...], wh_ref[...], preferred_element_type=F32)
    h = x_ref[...] + acc
    if final:
        ms = jnp.mean(h * h, axis=-1, keepdims=True)
        h = h * lax.rsqrt(ms + NORM_EPS) * g_ref[...]
    o_ref[...] = h


def _outproj(yr, yh, x2, wr, wh, g, tm, final):
    m, d = x2.shape
    wr_w = yr.shape[1]
    wh_w = yh.shape[1]
    return pl.pallas_call(
        functools.partial(_outproj_kernel, final),
        grid=(m // tm,),
        in_specs=[
            pl.BlockSpec((tm, wr_w), lambda i: (i, 0)),
            pl.BlockSpec((tm, wh_w), lambda i: (i, 0)),
            pl.BlockSpec((tm, d), lambda i: (i, 0)),
            pl.BlockSpec((wr_w, d), lambda i: (0, 0)),
            pl.BlockSpec((wh_w, d), lambda i: (0, 0)),
            pl.BlockSpec((1, d), lambda i: (0, 0)),
        ],
        out_specs=pl.BlockSpec((tm, d), lambda i: (i, 0)),
        out_shape=jax.ShapeDtypeStruct((m, d), F32),
        compiler_params=pltpu.CompilerParams(
            dimension_semantics=("arbitrary",), vmem_limit_bytes=VMEM_LIMIT),
        name="outproj",
    )(yr, yh, x2, wr, wh, g)


def kernel(x, norm_g, w_in, mu, w0, w2, a0, a2, k_k, k_a, r_k, lnx_w, lnx_b,
           hgrn_norm_g, lb_param, w_out, final_g):
    B, T, D = x.shape
    depth = w_in.shape[0]
    RW = w0.shape[1]
    HW = hgrn_norm_g.shape[1]
    n_r = 4 * RW + 2 * LORA
    tn = 768
    n_p = pl.cdiv(w_in.shape[2], tn) * tn
    h_start = (n_r // LANE) * LANE
    h_off = n_r - h_start
    h_blk_w = n_p - h_start
    assert (4 * RW) % LORA_BLK == 0 and 2 * LORA <= LORA_BLK and w2.shape[1] == LORA <= LANE
    assert h_start % h_blk_w == 0 and h_off + 4 * HW <= h_blk_w

    row = lambda z: z.reshape(1, -1).astype(F32)
    zr = lambda rows: jnp.zeros((rows, RW), F32)
    ones_bd = (jnp.arange(QW)[:, None] // RWKV_HEAD == jnp.arange(QW)[None, :] // RWKV_HEAD).astype(BF16)

    w_in_t = jnp.swapaxes(w_in, 1, 2)
    h = x.reshape(B * T, D)
    for l in range(depth):
        mu_full = jnp.concatenate([mu[l].astype(F32), jnp.zeros((n_p - n_r,), F32)]).reshape(1, -1)
        w2p = jnp.concatenate([w2[l].astype(F32), zr(LANE - LORA)], axis=0).astype(BF16)
        a2p = jnp.concatenate([zr(LORA), a2[l].astype(F32), zr(LORA_BLK - 2 * LORA)],
                              axis=0).astype(BF16)

        p = _inproj(h, row(norm_g[l]), w_in_t, l, mu_full, T, tm=1024, tn=tn, groups=4)
        p3 = p.reshape(B, T, n_p)
        y_r, y_h = _mixers(p3, 0, (4 * RW) // LORA_BLK, h_blk_w, h_start // h_blk_w, h_off, l,
                           row(w0[l]), w2p, row(a0[l]), a2p, row(k_k[l]), row(k_a[l]),
                           row(r_k[l]), row(lnx_w[l]), row(lnx_b[l]), ones_bd,
                           lb_param.astype(F32), row(hgrn_norm_g[l]))
        wo = w_out[l].astype(BF16)
        h = _outproj(y_r.reshape(B * T, RW), y_h.reshape(B * T, HW), h,
                     wo[:RW], wo[RW:], row(final_g), tm=512, final=(l == depth - 1))
    return h.reshape(B, T, D)
```

```python
import functools
import math

import jax
import jax.numpy as jnp
from jax import lax
from jax.experimental import pallas as pl
from jax.experimental.pallas import tpu as pltpu

F32 = jnp.float32
BF16 = jnp.bfloat16

NORM_EPS = 1e-6
LNX_EPS = 64e-5
RWKV_HEAD = 64
HGRN_HEAD = 128
LORA = 96
LORA_BLK = 256
LANE = 128
CHUNK = 64
HGRN_SUB = 16
QUAD = 4
QW = QUAD * RWKV_HEAD
BD_ROLES = ("B", "K", "V", "N", "Z", "U", "Bt", "Kt")
HGRN_AFTER_RWKV_STAGE = tuple(range(8, 20))
EXP_M05 = math.exp(-0.5)
VMEM_LIMIT = 48 * 1024 * 1024


def _mm(a, b):
    return jnp.dot(a.astype(BF16), b.astype(BF16), preferred_element_type=F32)


def _mm_nt(a, b):
    return lax.dot_general(a.astype(BF16), b.astype(BF16), (((1,), (1,)), ((), ())),
                           preferred_element_type=F32)


def _mm_tn(a, b):
    return lax.dot_general(a.astype(BF16), b.astype(BF16), (((0,), (0,)), ((), ())),
                           preferred_element_type=F32)


def _cumsum_rows(x, tril_bf16):
    hi = x.astype(BF16)
    lo = (x - hi.astype(F32)).astype(BF16)
    dot = functools.partial(jnp.dot, preferred_element_type=F32)
    return dot(tril_bf16, hi) + dot(tril_bf16, lo)


def _sigmoid(x):
    return 0.5 + 0.5 * jnp.tanh(0.5 * x)


def _inproj_kernel(n_valid, seq_len, x_ref, g_ref, w_ref, mu_ref, o_ref, hn_ref, carry_ref):
    q = pl.program_id(0)
    j = pl.program_id(1)
    i = pl.program_id(2)
    tm, tn = o_ref.shape

    @pl.when((q == 0) & (j == 0) & (i == 0))
    def _():
        carry_ref[...] = jnp.zeros_like(carry_ref)

    @pl.when(j == 0)
    def _():
        x = x_ref[...]
        ms = jnp.mean(x * x, axis=-1, keepdims=True)
        hn_ref[i] = (x * lax.rsqrt(ms + NORM_EPS) * g_ref[...]).astype(BF16)

    w_row = j * tn + lax.broadcasted_iota(jnp.int32, (tn, 1), 0)
    w = jnp.where(w_row < n_valid, w_ref[...], 0.0)
    acc = _mm_nt(hn_ref[i], w)

    first_row = (q * pl.num_programs(2) + i) * tm
    prev_last = jnp.where(first_row % seq_len == 0, 0.0, carry_ref[j])
    carry_ref[j] = acc[tm - 1:tm, :]
    is_row0 = lax.broadcasted_iota(jnp.int32, (tm, 1), 0) == 0
    prev = jnp.where(is_row0, prev_last, pltpu.roll(acc, 1, axis=0))
    o_ref[...] = acc + mu_ref[...] * (prev - acc)


def _inproj(x2, g, wt_all, layer, mu_full, seq_len, tm, tn, groups):
    m, d = x2.shape
    n_valid = wt_all.shape[1]
    n = pl.cdiv(n_valid, tn) * tn
    ni = m // (groups * tm)
    assert seq_len % tm == 0 and mu_full.shape == (1, n)
    x_map = lambda q, j, i: (q * ni + jnp.where(j == 0, i, ni - 1), 0)
    return pl.pallas_call(
        functools.partial(_inproj_kernel, n_valid, seq_len),
        grid=(groups, n // tn, ni),
        in_specs=[
            pl.BlockSpec((tm, d), x_map),
            pl.BlockSpec((1, d), lambda q, j, i: (0, 0)),
            pl.BlockSpec((None, tn, d), lambda q, j, i: (layer, j, 0)),
            pl.BlockSpec((1, tn), lambda q, j, i: (0, j)),
        ],
        out_specs=pl.BlockSpec((tm, tn), lambda q, j, i: (q * ni + i, j)),
        out_shape=jax.ShapeDtypeStruct((m, n), F32),
        scratch_shapes=[pltpu.VMEM((ni, tm, d), BF16), pltpu.VMEM((n // tn, 1, tn), F32)],
        compiler_params=pltpu.CompilerParams(
            dimension_semantics=("arbitrary", "arbitrary", "arbitrary"),
            vmem_limit_bytes=VMEM_LIMIT),
        name="inproj",
    )(x2, g, wt_all, mu_full)


def _rwkv_stages(pm_ref, pl_ref, w0_ref, w2_ref, a0_ref, a2_ref,
                 kk_ref, ka_ref, rk_ref, lw_ref, lb_ref, ones_ref,
                 o_ref, state, bd_ref):
    C = CHUNK
    NB = pm_ref.shape[0]
    W = w0_ref.shape[1]
    NQ = W // QW

    t_i = lax.broadcasted_iota(jnp.int32, (C, C), 0)
    s_i = lax.broadcasted_iota(jnp.int32, (C, C), 1)
    tril = (t_i >= s_i).astype(BF16)
    ones_bd = ones_ref[...]
    dot = functools.partial(jnp.dot, preferred_element_type=F32)

    lane = lax.broadcasted_iota(jnp.int32, (C, QW), 1)
    lane_s = lane % RWKV_HEAD
    trow = lax.broadcasted_iota(jnp.int32, (C, QW), 0)
    strict = lane_s < trow
    incl = lane_s <= trow
    eye_q = jnp.where(lane_s == trow, 1.0, 0.0).astype(F32)
    bd_mask = (lax.broadcasted_iota(jnp.int32, (QW, QW), 0) // RWKV_HEAD
               == lax.broadcasted_iota(jnp.int32, (QW, QW), 1) // RWKV_HEAD)

    def bd(u, role, x):
        slot = u["ui"] * len(BD_ROLES) + BD_ROLES.index(role)
        xb = x.astype(BF16)
        for h in range(QUAD):
            hs = slice(h * RWKV_HEAD, (h + 1) * RWKV_HEAD)
            bd_ref[slot, h * C:(h + 1) * C, hs] = xb[:, hs]
        return bd_ref[slot]

    def bd_t(u, role, x):
        slot = u["ui"] * len(BD_ROLES) + BD_ROLES.index(role)
        for p in range(QW // LANE):
            ps = slice(p * LANE, (p + 1) * LANE)
            bd_ref[slot, ps, ps] = jnp.transpose(x[ps, ps])
        return bd_ref[slot]

    def segsum_all(xs):
        out = dot(jnp.concatenate([x.astype(BF16) for x in xs], axis=0), ones_bd)
        return [out[i * C:(i + 1) * C] for i in range(len(xs))]

    xl = jnp.concatenate([pl_ref[b] for b in range(NB)], axis=0)
    z = w0_ref[...] + _mm(jnp.tanh(xl[:, :LANE]), w2_ref[...])
    ld_all = -EXP_M05 * _sigmoid(z)
    a_all = _sigmoid(a0_ref[...] + _mm(xl, a2_ref[...]))
    units = []
    for b in range(NB):
        r = pm_ref[b, :, 0 * W:1 * W]
        k = pm_ref[b, :, 1 * W:2 * W]
        v = pm_ref[b, :, 2 * W:3 * W]
        gate = pm_ref[b, :, 3 * W:4 * W]
        ld = ld_all[b * C:(b + 1) * C]
        a = a_all[b * C:(b + 1) * C]
        G = _cumsum_rows(ld, tril)
        Gx = G - ld
        GC = G[C - 1:C, :]
        for q in range(NQ):
            sl = slice(q * QW, (q + 1) * QW)
            units.append(dict(b=b, ui=b * NQ + q, sl=sl, r=r[:, sl], k=k[:, sl], v=v[:, sl],
                              g=gate[:, sl], a=a[:, sl], G=G[:, sl], Gx=Gx[:, sl], GC=GC[:, sl]))
    yield

    for u in units:
        u["kk"] = u["k"] * kk_ref[:, u["sl"]]
    for u, n2 in zip(units, segsum_all([u["kk"] * u["kk"] for u in units])):
        u["n2"] = n2
    yield

    def unit_chain(u):
        sl = u["sl"]
        kkn = u["kk"] * lax.rsqrt(jnp.maximum(u["n2"], 1e-24))
        k2 = u["k"] * (1.0 + (u["a"] - 1.0) * ka_ref[:, sl])
        av = -kkn
        bv = kkn * u["a"]
        einv = jnp.exp(-u["G"])
        eC = jnp.exp(u["GC"] - u["G"])
        u["rkk"] = u["r"] * k2 * rk_ref[:, sl]
        BDB = bd_t(u, "Bt", bd(u, "B", bv * einv))
        BDK = bd_t(u, "Kt", bd(u, "K", k2 * einv))
        BKe = jnp.concatenate([bv * eC, k2 * eC], axis=0).astype(BF16)
        BDV = bd(u, "V", u["v"])
        lhs = jnp.concatenate([av * jnp.exp(u["Gx"]), u["r"] * jnp.exp(u["G"])],
                              axis=0).astype(BF16)
        pc_col = jnp.transpose(jnp.broadcast_to(jnp.exp(u["GC"]), (LANE, QW)))
        PCc = jnp.concatenate([pc_col] * (QW // LANE), axis=1)
        yield
        SB = _mm(lhs, BDB)
        yield
        SK = _mm(lhs, BDK)
        yield
        N = jnp.where(strict, SB[:C], 0.0)
        Lrb = jnp.where(incl, SB[C:], 0.0)
        P = eye_q + N
        Np = _mm(N, bd(u, "N", N))
        yield
        akl = jnp.concatenate([jnp.where(strict, SK[:C], 0.0),
                               jnp.where(incl, SK[C:], 0.0)], axis=0)
        both = _mm(akl, BDV)
        yield
        for _ in range(4):
            out = _mm(jnp.concatenate([P, Np], axis=0), bd(u, "N", Np))
            P = P + out[:C]
            Np = out[C:]
            yield
        P = P + _mm(P, bd(u, "N", Np))
        yield
        H = state[u["ui"]]
        zy = _mm(lhs, H)
        yield
        U = _mm(P, bd(u, "Z", zy[:C] + both[:C]))
        yield
        u["Y"] = zy[C:] + _mm(Lrb, bd(u, "U", U)) + both[C:]
        yield
        upd = _mm_tn(BKe, jnp.concatenate([U, u["v"]], axis=0))
        state[u["ui"]] = H * PCc + jnp.where(bd_mask, upd, 0.0)

    chains = [unit_chain(u) for u in units]
    live = list(range(len(chains)))
    while live:
        for i in list(live):
            if next(chains[i], "done") == "done":
                live.remove(i)
        yield

    for u, s in zip(units, segsum_all([u["Y"] for u in units])):
        u["yc"] = u["Y"] - s * (1.0 / RWKV_HEAD)
    yield
    for u, s in zip(units, segsum_all([u["yc"] * u["yc"] for u in units])):
        u["var"] = s * (1.0 / RWKV_HEAD)
    yield
    for u, s in zip(units, segsum_all([u["rkk"] for u in units])):
        u["bonus"] = s
    yield
    for u in units:
        sl = u["sl"]
        yn = u["yc"] * lax.rsqrt(u["var"] + LNX_EPS) * lw_ref[:, sl] + lb_ref[:, sl]
        res = (yn + u["bonus"] * u["v"]) * (u["g"] * _sigmoid(u["g"]))
        o_ref[u["b"], :, sl] = res.astype(o_ref.dtype)


def _hgrn_stages(layer, off, ph_ref, lbp_ref, ng_ref, o_ref, state):
    C = CHUNK
    NB = ph_ref.shape[0]
    W = ng_ref.shape[1]
    D = HGRN_HEAD
    NH = W // D
    SUB = HGRN_SUB
    NBLK = C // SUB
    GRP = 2 * LANE // D

    lp = lbp_ref[...]
    e = jnp.exp(lp - jnp.max(lp, axis=0, keepdims=True))
    lb = jnp.sum(e[:layer + 1], axis=0, keepdims=True) / jnp.sum(e, axis=0, keepdims=True)

    t_i = lax.broadcasted_iota(jnp.int32, (C, C), 0)
    s_i = lax.broadcasted_iota(jnp.int32, (C, C), 1)
    causal = t_i >= s_i
    tril = causal.astype(BF16)

    units = []
    for b in range(NB):
        x = ph_ref[b]
        qv = x[:, off + 0 * W:off + 1 * W]
        fr = x[:, off + 1 * W:off + 2 * W]
        iv = x[:, off + 2 * W:off + 3 * W]
        gate = x[:, off + 3 * W:off + 4 * W]
        f = lb + (1.0 - lb) * _sigmoid(fr)
        kx = 1.0 - f
        G = _cumsum_rows(jnp.log(f), tril)
        GC = G[C - 1:C, :]
        kd = (kx * jnp.exp(GC - G)).astype(BF16)
        PC = jnp.exp(GC)
        ivb = iv.astype(BF16)
        zero = jnp.zeros((SUB, W), BF16)
        qs, ks, qt_rows, k_rows = [], [], [], []
        for j in range(NBLK):
            lo, hi = j * SUB, (j + 1) * SUB
            qj = qv[lo:hi] * jnp.exp(G[lo:hi] - G[lo - 1:lo, :]) if j else qv[lo:hi] * jnp.exp(G[lo:hi])
            kj = kx[lo:hi] * jnp.exp(G[lo - 1:lo, :] - G[lo:hi]) if j else kx[lo:hi] * jnp.exp(-G[lo:hi])
            if j:
                step = jnp.exp(G[lo - 1:lo, :] - (G[lo - SUB - 1:lo - SUB, :] if j > 1 else 0.0))
                k_rows = [kr * step for kr in k_rows]
                qt_rows.append(qj * jnp.exp(G[lo - 1:lo, :]))
            else:
                qt_rows.append(qj)
            k_rows.append(kj)
            g0 = (j // GRP) * GRP
            qs.append(jnp.concatenate([zero] * (j - g0) + [qj.astype(BF16)] + [zero] * (g0 + GRP - 1 - j),
                                      axis=0))
            ks.append(jnp.concatenate([kr.astype(BF16) for kr in k_rows] + [zero] * (g0 + GRP - 1 - j),
                                      axis=0))
        qt = jnp.concatenate(qt_rows, axis=0).astype(BF16)
        for h in range(NH):
            sl = slice(h * D, (h + 1) * D)
            units.append(dict(b=b, h=h, sl=sl, qt=qt[:, sl], kd=kd[:, sl], iv=ivb[:, sl],
                              PC=PC[:, sl], g=gate[:, sl],
                              qcat=[jnp.concatenate([q[:, sl] for q in qs[g:g + GRP]], axis=1)
                                    for g in range(0, NBLK, GRP)],
                              kcat=[jnp.concatenate([k[:, sl] for k in ks[g:g + GRP]], axis=1)
                                    for g in range(0, NBLK, GRP)]))
        yield

    halves = (units[:len(units) // 2], units[len(units) // 2:])
    for part in halves:
        for u in part:
            u["A"] = [_mm_nt(q, k) for q, k in zip(u["qcat"], u["kcat"])]
        yield
    for part in halves:
        for u in part:
            u["S"] = state[u["b"] * NH + u["h"]]
            u["inter"] = _mm_nt(u["qt"], u["S"])
        yield
    for part in halves:
        for u in part:
            rows = []
            for g, A in enumerate(u["A"]):
                hi = (g + 1) * GRP * SUB
                mask = causal[hi - GRP * SUB:hi, :hi]
                rows.append(_mm(jnp.where(mask, A, 0.0), u["iv"][:hi]))
            u["o"] = jnp.concatenate(rows, axis=0) + u["inter"]
        yield
    for part in halves:
        for u in part:
            state[u["b"] * NH + u["h"]] = u["S"] * u["PC"] + _mm_tn(u["iv"], u["kd"])
        yield
    for part in halves:
        for u in part:
            o = u["o"]
            g = u["g"]
            ms = jnp.mean(o * o, axis=-1, keepdims=True)
            res = o * lax.rsqrt(ms + NORM_EPS) * ng_ref[:, u["sl"]] * (g * _sigmoid(g))
            o_ref[u["b"], :, u["sl"]] = res.astype(o_ref.dtype)
        yield


def _mixers_kernel(layer, off, pm_ref, pl_ref, ph_ref, w0_ref, w2_ref, a0_ref, a2_ref,
                   kk_ref, ka_ref, rk_ref, lw_ref, lb_ref, ones_ref, lbp_ref, ng_ref,
                   or_ref, oh_ref, r_state, bd_ref, h_state):
    @pl.when(pl.program_id(0) == 0)
    def _():
        for ref in (r_state, bd_ref, h_state):
            ref[...] = jnp.zeros_like(ref)

    rw = _rwkv_stages(pm_ref, pl_ref, w0_ref, w2_ref, a0_ref, a2_ref, kk_ref, ka_ref, rk_ref,
                      lw_ref, lb_ref, ones_ref, or_ref, r_state, bd_ref)
    hg = _hgrn_stages(layer, off, ph_ref, lbp_ref, ng_ref, oh_ref, h_state)
    for stage, _ in enumerate(rw, start=1):
        for _ in range(HGRN_AFTER_RWKV_STAGE.count(stage)):
            next(hg, None)
    for _ in hg:
        pass


def _mixers(p3, main_blk, lora_blk, h_blk_w, h_blk, h_off, layer,
            w0, w2p, a0, a2p, k_k, k_a, r_k, lnx_w, lnx_b, ones_bd, lb_param, hgrn_g):
    B, T, _ = p3.shape
    W = w0.shape[1]
    HW = hgrn_g.shape[1]
    C = CHUNK
    NU = B * (W // QW)
    L = lb_param.shape[0]
    full = lambda shape: pl.BlockSpec(shape, lambda c: (0,) * len(shape))
    return pl.pallas_call(
        functools.partial(_mixers_kernel, layer, h_off),
        grid=(T // C,),
        in_specs=[
            pl.BlockSpec((B, C, 4 * W), lambda c: (0, c, main_blk)),
            pl.BlockSpec((B, C, LORA_BLK), lambda c: (0, c, lora_blk)),
            pl.BlockSpec((B, C, h_blk_w), lambda c: (0, c, h_blk)),
            full((1, W)), full((LANE, W)), full((1, W)), full((LORA_BLK, W)),
            full((1, W)), full((1, W)), full((1, W)), full((1, W)), full((1, W)),
            full((QW, QW)), full((L, HW)), full((1, HW)),
        ],
        out_specs=[pl.BlockSpec((B, C, W), lambda c: (0, c, 0)),
                   pl.BlockSpec((B, C, HW), lambda c: (0, c, 0))],
        out_shape=[jax.ShapeDtypeStruct((B, T, W), BF16),
                   jax.ShapeDtypeStruct((B, T, HW), BF16)],
        scratch_shapes=[
            pltpu.VMEM((NU, QW, QW), F32),
            pltpu.VMEM((NU * len(BD_ROLES), QW, QW), BF16),
            pltpu.VMEM((B * (HW // HGRN_HEAD), HGRN_HEAD, HGRN_HEAD), F32),
        ],
        compiler_params=pltpu.CompilerParams(
            dimension_semantics=("arbitrary",), vmem_limit_bytes=VMEM_LIMIT),
        name="mixers",
    )(p3, p3, p3, w0, w2p, a0, a2p, k_k, k_a, r_k, lnx_w, lnx_b, ones_bd, lb_param, hgrn_g)


def _outproj_kernel(final, yr_ref, yh_ref, x_ref, wr_ref, wh_ref, g_ref, o_ref):
    acc = jnp.dot(yr_ref[...], wr_ref[...], preferred_element_type=F32)
    acc += jnp.dot(yh_ref[...], wh_ref[...], preferred_element_type=F32)
    h = x_ref[...] + acc
    if final:
        ms = jnp.mean(h * h, axis=-1, keepdims=True)
        h = h * lax.rsqrt(ms + NORM_EPS) * g_ref[...]
    o_ref[...] = h


def _outproj(yr, yh, x2, wr, wh, g, tm, final):
    m, d = x2.shape
    wr_w = yr.shape[1]
    wh_w = yh.shape[1]
    return pl.pallas_call(
        functools.partial(_outproj_kernel, final),
        grid=(m // tm,),
        in_specs=[
            pl.BlockSpec((tm, wr_w), lambda i: (i, 0)),
            pl.BlockSpec((tm, wh_w), lambda i: (i, 0)),
            pl.BlockSpec((tm, d), lambda i: (i, 0)),
            pl.BlockSpec((wr_w, d), lambda i: (0, 0)),
            pl.BlockSpec((wh_w, d), lambda i: (0, 0)),
            pl.BlockSpec((1, d), lambda i: (0, 0)),
        ],
        out_specs=pl.BlockSpec((tm, d), lambda i: (i, 0)),
        out_shape=jax.ShapeDtypeStruct((m, d), F32),
        compiler_params=pltpu.CompilerParams(
            dimension_semantics=("arbitrary",), vmem_limit_bytes=VMEM_LIMIT),
        name="outproj",
    )(yr, yh, x2, wr, wh, g)


def kernel(x, norm_g, w_in, mu, w0, w2, a0, a2, k_k, k_a, r_k, lnx_w, lnx_b,
           hgrn_norm_g, lb_param, w_out, final_g):
    B, T, D = x.shape
    depth = w_in.shape[0]
    RW = w0.shape[1]
    HW = hgrn_norm_g.shape[1]
    n_r = 4 * RW + 2 * LORA
    tn = 768
    n_p = pl.cdiv(w_in.shape[2], tn) * tn
    h_start = (n_r // LANE) * LANE
    h_off = n_r - h_start
    h_blk_w = n_p - h_start
    assert (4 * RW) % LORA_BLK == 0 and 2 * LORA <= LORA_BLK and w2.shape[1] == LORA <= LANE
    assert h_start % h_blk_w == 0 and h_off + 4 * HW <= h_blk_w

    row = lambda z: z.reshape(1, -1).astype(F32)
    zr = lambda rows: jnp.zeros((rows, RW), F32)
    ones_bd = (jnp.arange(QW)[:, None] // RWKV_HEAD == jnp.arange(QW)[None, :] // RWKV_HEAD).astype(BF16)

    w_in_t = jnp.swapaxes(w_in, 1, 2)
    h = x.reshape(B * T, D)
    for l in range(depth):
        mu_full = jnp.concatenate([mu[l].astype(F32), jnp.zeros((n_p - n_r,), F32)]).reshape(1, -1)
        w2p = jnp.concatenate([w2[l].astype(F32), zr(LANE - LORA)], axis=0).astype(BF16)
        a2p = jnp.concatenate([zr(LORA), a2[l].astype(F32), zr(LORA_BLK - 2 * LORA)],
                              axis=0).astype(BF16)

        p = _inproj(h, row(norm_g[l]), w_in_t, l, mu_full, T, tm=1024, tn=tn, groups=4)
        p3 = p.reshape(B, T, n_p)
        y_r, y_h = _mixers(p3, 0, (4 * RW) // LORA_BLK, h_blk_w, h_start // h_blk_w, h_off, l,
                           row(w0[l]), w2p, row(a0[l]), a2p, row(k_k[l]), row(k_a[l]),
                           row(r_k[l]), row(lnx_w[l]), row(lnx_b[l]), ones_bd,
                           lb_param.astype(F32), row(hgrn_norm_g[l]))
        wo = w_out[l].astype(BF16)
        h = _outproj(y_r.reshape(B * T, RW), y_h.reshape(B * T, HW), h,
                     wo[:RW], wo[RW:], row(final_g), tm=512, final=(l == depth - 1))
    return h.reshape(B, T, D)
```

```python
import functools
import math

import jax
import jax.numpy as jnp
from jax import lax
from jax.experimental import pallas as pl
from jax.experimental.pallas import tpu as pltpu

F32 = jnp.float32
BF16 = jnp.bfloat16

NORM_EPS = 1e-6
LNX_EPS = 64e-5
RWKV_HEAD = 64
HGRN_HEAD = 128
LORA = 96
LORA_BLK = 256
LANE = 128
CHUNK = 64
HGRN_SUB = 16
QUAD = 4
QW = QUAD * RWKV_HEAD
BD_ROLES = ("B", "K", "V", "N", "Z", "U")
HGRN_AFTER_RWKV_STAGE = tuple(range(6, 18))
EXP_M05 = math.exp(-0.5)
VMEM_LIMIT = 48 * 1024 * 1024


def _mm(a, b):
    return jnp.dot(a.astype(BF16), b.astype(BF16), preferred_element_type=F32)


def _mm_nt(a, b):
    return lax.dot_general(a.astype(BF16), b.astype(BF16), (((1,), (1,)), ((), ())),
                           preferred_element_type=F32)


def _mm_tn(a, b):
    return lax.dot_general(a.astype(BF16), b.astype(BF16), (((0,), (0,)), ((), ())),
                           preferred_element_type=F32)


def _cumsum_rows(x, tril_bf16):
    hi = x.astype(BF16)
    lo = (x - hi.astype(F32)).astype(BF16)
    dot = functools.partial(jnp.dot, preferred_element_type=F32)
    return dot(tril_bf16, hi) + dot(tril_bf16, lo)


def _sigmoid(x):
    return 0.5 + 0.5 * jnp.tanh(0.5 * x)


def _inproj_kernel(n_valid, seq_len, x_ref, g_ref, w_ref, mu_ref, o_ref, hn_ref, carry_ref):
    q = pl.program_id(0)
    j = pl.program_id(1)
    i = pl.program_id(2)
    tm, tn = o_ref.shape

    @pl.when((q == 0) & (j == 0) & (i == 0))
    def _():
        carry_ref[...] = jnp.zeros_like(carry_ref)

    @pl.when(j == 0)
    def _():
        x = x_ref[...]
        ms = jnp.mean(x * x, axis=-1, keepdims=True)
        hn_ref[i] = (x * lax.rsqrt(ms + NORM_EPS) * g_ref[...]).astype(BF16)

    w_row = j * tn + lax.broadcasted_iota(jnp.int32, (tn, 1), 0)
    w = jnp.where(w_row < n_valid, w_ref[...], 0.0)
    acc = _mm_nt(hn_ref[i], w)

    first_row = (q * pl.num_programs(2) + i) * tm
    prev_last = jnp.where(first_row % seq_len == 0, 0.0, carry_ref[j])
    carry_ref[j] = acc[tm - 1:tm, :]
    is_row0 = lax.broadcasted_iota(jnp.int32, (tm, 1), 0) == 0
    prev = jnp.where(is_row0, prev_last, pltpu.roll(acc, 1, axis=0))
    o_ref[...] = acc + mu_ref[...] * (prev - acc)


def _inproj(x2, g, wt_all, layer, mu_full, seq_len, tm, tn, groups):
    m, d = x2.shape
    n_valid = wt_all.shape[1]
    n = pl.cdiv(n_valid, tn) * tn
    ni = m // (groups * tm)
    assert seq_len % tm == 0 and mu_full.shape == (1, n)
    x_map = lambda q, j, i: (q * ni + jnp.where(j == 0, i, ni - 1), 0)
    return pl.pallas_call(
        functools.partial(_inproj_kernel, n_valid, seq_len),
        grid=(groups, n // tn, ni),
        in_specs=[
            pl.BlockSpec((tm, d), x_map),
            pl.BlockSpec((1, d), lambda q, j, i: (0, 0)),
            pl.BlockSpec((None, tn, d), lambda q, j, i: (layer, j, 0)),
            pl.BlockSpec((1, tn), lambda q, j, i: (0, j)),
        ],
        out_specs=pl.BlockSpec((tm, tn), lambda q, j, i: (q * ni + i, j)),
        out_shape=jax.ShapeDtypeStruct((m, n), F32),
        scratch_shapes=[pltpu.VMEM((ni, tm, d), BF16), pltpu.VMEM((n // tn, 1, tn), F32)],
        compiler_params=pltpu.CompilerParams(
            dimension_semantics=("arbitrary", "arbitrary", "arbitrary"),
            vmem_limit_bytes=VMEM_LIMIT),
        name="inproj",
    )(x2, g, wt_all, mu_full)


def _rwkv_stages(pm_ref, pl_ref, w0_ref, w2_ref, a0_ref, a2_ref,
                 kk_ref, ka_ref, rk_ref, lw_ref, lb_ref, ones_ref,
                 o_ref, state, bd_ref):
    C = CHUNK
    NB = pm_ref.shape[0]
    W = w0_ref.shape[1]
    NQ = W // QW

    t_i = lax.broadcasted_iota(jnp.int32, (C, C), 0)
    s_i = lax.broadcasted_iota(jnp.int32, (C, C), 1)
    tril = (t_i >= s_i).astype(BF16)
    ones_bd = ones_ref[...]
    dot = functools.partial(jnp.dot, preferred_element_type=F32)

    lane = lax.broadcasted_iota(jnp.int32, (C, QW), 1)
    lane_s = lane % RWKV_HEAD
    trow = lax.broadcasted_iota(jnp.int32, (C, QW), 0)
    strict = lane_s < trow
    incl = lane_s <= trow
    eye_q = jnp.where(lane_s == trow, 1.0, 0.0).astype(F32)
    bd_mask = (lax.broadcasted_iota(jnp.int32, (QW, QW), 0) // RWKV_HEAD
               == lax.broadcasted_iota(jnp.int32, (QW, QW), 1) // RWKV_HEAD)

    def bd(u, role, x):
        slot = u["ui"] * len(BD_ROLES) + BD_ROLES.index(role)
        xb = x.astype(BF16)
        for h in range(QUAD):
            hs = slice(h * RWKV_HEAD, (h + 1) * RWKV_HEAD)
            bd_ref[slot, h * C:(h + 1) * C, hs] = xb[:, hs]
        return bd_ref[slot]

    def segsum_all(xs):
        out = dot(jnp.concatenate([x.astype(BF16) for x in xs], axis=0), ones_bd)
        return [out[i * C:(i + 1) * C] for i in range(len(xs))]

    xl = jnp.concatenate([pl_ref[b] for b in range(NB)], axis=0)
    z = w0_ref[...] + _mm(jnp.tanh(xl[:, :LANE]), w2_ref[...])
    ld_all = -EXP_M05 * _sigmoid(z)
    a_all = _sigmoid(a0_ref[...] + _mm(xl, a2_ref[...]))
    units = []
    for b in range(NB):
        r = pm_ref[b, :, 0 * W:1 * W]
        k = pm_ref[b, :, 1 * W:2 * W]
        v = pm_ref[b, :, 2 * W:3 * W]
        gate = pm_ref[b, :, 3 * W:4 * W]
        ld = ld_all[b * C:(b + 1) * C]
        a = a_all[b * C:(b + 1) * C]
        G = _cumsum_rows(ld, tril)
        Gx = G - ld
        GC = G[C - 1:C, :]
        for q in range(NQ):
            sl = slice(q * QW, (q + 1) * QW)
            units.append(dict(b=b, ui=b * NQ + q, sl=sl, r=r[:, sl], k=k[:, sl], v=v[:, sl],
                              g=gate[:, sl], a=a[:, sl], G=G[:, sl], Gx=Gx[:, sl], GC=GC[:, sl]))
    yield

    for u in units:
        u["kk"] = u["k"] * kk_ref[:, u["sl"]]
    for u, n2 in zip(units, segsum_all([u["kk"] * u["kk"] for u in units])):
        u["n2"] = n2
    yield

    def unit_chain(u):
        sl = u["sl"]
        kkn = u["kk"] * lax.rsqrt(jnp.maximum(u["n2"], 1e-24))
        k2 = u["k"] * (1.0 + (u["a"] - 1.0) * ka_ref[:, sl])
        av = -kkn
        bv = kkn * u["a"]
        einv = jnp.exp(-u["G"])
        eC = jnp.exp(u["GC"] - u["G"])
        u["rkk"] = u["r"] * k2 * rk_ref[:, sl]
        BDB = bd(u, "B", bv * einv)
        BDK = bd(u, "K", k2 * einv)
        BKe = jnp.concatenate([bv * eC, k2 * eC], axis=0).astype(BF16)
        BDV = bd(u, "V", u["v"])
        lhs = jnp.concatenate([av * jnp.exp(u["Gx"]), u["r"] * jnp.exp(u["G"])],
                              axis=0).astype(BF16)
        pc_col = jnp.transpose(jnp.broadcast_to(jnp.exp(u["GC"]), (LANE, QW)))
        PCc = jnp.concatenate([pc_col] * (QW // LANE), axis=1)
        yield
        SB = _mm_nt(lhs, BDB)
        yield
        SK = _mm_nt(lhs, BDK)
        yield
        N = jnp.where(strict, SB[:C], 0.0)
        Lrb = jnp.where(incl, SB[C:], 0.0)
        P = eye_q + N
        Np = _mm(N, bd(u, "N", N))
        yield
        akl = jnp.concatenate([jnp.where(strict, SK[:C], 0.0),
                               jnp.where(incl, SK[C:], 0.0)], axis=0)
        both = _mm(akl, BDV)
        yield
        for _ in range(4):
            out = _mm(jnp.concatenate([P, Np], axis=0), bd(u, "N", Np))
            P = P + out[:C]
            Np = out[C:]
            yield
        P = P + _mm(P, bd(u, "N", Np))
        yield
        H = state[u["ui"]]
        zy = _mm(lhs, H)
        yield
        U = _mm(P, bd(u, "Z", zy[:C] + both[:C]))
        yield
        u["Y"] = zy[C:] + _mm(Lrb, bd(u, "U", U)) + both[C:]
        yield
        upd = _mm_tn(BKe, jnp.concatenate([U, u["v"]], axis=0))
        state[u["ui"]] = H * PCc + jnp.where(bd_mask, upd, 0.0)

    chains = [unit_chain(u) for u in units]
    live = list(range(len(chains)))
    while live:
        for i in list(live):
            if next(chains[i], "done") == "done":
                live.remove(i)
        yield

    for u, s in zip(units, segsum_all([u["Y"] for u in units])):
        u["yc"] = u["Y"] - s * (1.0 / RWKV_HEAD)
    yield
    for u, s in zip(units, segsum_all([u["yc"] * u["yc"] for u in units])):
        u["var"] = s * (1.0 / RWKV_HEAD)
    yield
    for u, s in zip(units, segsum_all([u["rkk"] for u in units])):
        u["bonus"] = s
    yield
    for u in units:
        sl = u["sl"]
        yn = u["yc"] * lax.rsqrt(u["var"] + LNX_EPS) * lw_ref[:, sl] + lb_ref[:, sl]
        res = (yn + u["bonus"] * u["v"]) * (u["g"] * _sigmoid(u["g"]))
        o_ref[u["b"], :, sl] = res.astype(o_ref.dtype)


def _hgrn_stages(layer, off, ph_ref, lbp_ref, ng_ref, o_ref, state):
    C = CHUNK
    NB = ph_ref.shape[0]
    W = ng_ref.shape[1]
    D = HGRN_HEAD
    NH = W // D
    SUB = HGRN_SUB
    NBLK = C // SUB
    GRP = 2 * LANE // D

    lp = lbp_ref[...]
    e = jnp.exp(lp - jnp.max(lp, axis=0, keepdims=True))
    lb = jnp.sum(e[:layer + 1], axis=0, keepdims=True) / jnp.sum(e, axis=0, keepdims=True)

    t_i = lax.broadcasted_iota(jnp.int32, (C, C), 0)
    s_i = lax.broadcasted_iota(jnp.int32, (C, C), 1)
    causal = t_i >= s_i
    tril = causal.astype(BF16)

    units = []
    for b in range(NB):
        x = ph_ref[b]
        qv = x[:, off + 0 * W:off + 1 * W]
        fr = x[:, off + 1 * W:off + 2 * W]
        iv = x[:, off + 2 * W:off + 3 * W]
        gate = x[:, off + 3 * W:off + 4 * W]
        f = lb + (1.0 - lb) * _sigmoid(fr)
        kx = 1.0 - f
        G = _cumsum_rows(jnp.log(f), tril)
        GC = G[C - 1:C, :]
        kd = (kx * jnp.exp(GC - G)).astype(BF16)
        PC = jnp.exp(GC)
        ivb = iv.astype(BF16)
        zero = jnp.zeros((SUB, W), BF16)
        qs, ks, qt_rows, k_rows = [], [], [], []
        for j in range(NBLK):
            lo, hi = j * SUB, (j + 1) * SUB
            qj = qv[lo:hi] * jnp.exp(G[lo:hi] - G[lo - 1:lo, :]) if j else qv[lo:hi] * jnp.exp(G[lo:hi])
            kj = kx[lo:hi] * jnp.exp(G[lo - 1:lo, :] - G[lo:hi]) if j else kx[lo:hi] * jnp.exp(-G[lo:hi])
            if j:
                step = jnp.exp(G[lo - 1:lo, :] - (G[lo - SUB - 1:lo - SUB, :] if j > 1 else 0.0))
                k_rows = [kr * step for kr in k_rows]
                qt_rows.append(qj * jnp.exp(G[lo - 1:lo, :]))
            else:
                qt_rows.append(qj)
            k_rows.append(kj)
            g0 = (j // GRP) * GRP
            qs.append(jnp.concatenate([zero] * (j - g0) + [qj.astype(BF16)] + [zero] * (g0 + GRP - 1 - j),
                                      axis=0))
            ks.append(jnp.concatenate([kr.astype(BF16) for kr in k_rows] + [zero] * (g0 + GRP - 1 - j),
                                      axis=0))
        qt = jnp.concatenate(qt_rows, axis=0).astype(BF16)
        for h in range(NH):
            sl = slice(h * D, (h + 1) * D)
            units.append(dict(b=b, h=h, sl=sl, qt=qt[:, sl], kd=kd[:, sl], iv=ivb[:, sl],
                              PC=PC[:, sl], g=gate[:, sl],
                              qcat=[jnp.concatenate([q[:, sl] for q in qs[g:g + GRP]], axis=1)
                                    for g in range(0, NBLK, GRP)],
                              kcat=[jnp.concatenate([k[:, sl] for k in ks[g:g + GRP]], axis=1)
                                    for g in range(0, NBLK, GRP)]))
        yield

    halves = (units[:len(units) // 2], units[len(units) // 2:])
    for part in halves:
        for u in part:
            u["A"] = [_mm_nt(q, k) for q, k in zip(u["qcat"], u["kcat"])]
        yield
    for part in halves:
        for u in part:
            u["S"] = state[u["b"] * NH + u["h"]]
            u["inter"] = _mm_nt(u["qt"], u["S"])
        yield
    for part in halves:
        for u in part:
            rows = []
            for g, A in enumerate(u["A"]):
                hi = (g + 1) * GRP * SUB
                mask = causal[hi - GRP * SUB:hi, :hi]
                rows.append(_mm(jnp.where(mask, A, 0.0), u["iv"][:hi]))
            u["o"] = jnp.concatenate(rows, axis=0) + u["inter"]
        yield
    for part in halves:
        for u in part:
            state[u["b"] * NH + u["h"]] = u["S"] * u["PC"] + _mm_tn(u["iv"], u["kd"])
        yield
    for part in halves:
        for u in part:
            o = u["o"]
            g = u["g"]
            ms = jnp.mean(o * o, axis=-1, keepdims=True)
            res = o * lax.rsqrt(ms + NORM_EPS) * ng_ref[:, u["sl"]] * (g * _sigmoid(g))
            o_ref[u["b"], :, u["sl"]] = res.astype(o_ref.dtype)
        yield


def _mixers_kernel(layer, off, pm_ref, pl_ref, ph_ref, w0_ref, w2_ref, a0_ref, a2_ref,
                   kk_ref, ka_ref, rk_ref, lw_ref, lb_ref, ones_ref, lbp_ref, ng_ref,
                   or_ref, oh_ref, r_state, bd_ref, h_state):
    @pl.when(pl.program_id(0) == 0)
    def _():
        for ref in (r_state, bd_ref, h_state):
            ref[...] = jnp.zeros_like(ref)

    rw = _rwkv_stages(pm_ref, pl_ref, w0_ref, w2_ref, a0_ref, a2_ref, kk_ref, ka_ref, rk_ref,
                      lw_ref, lb_ref, ones_ref, or_ref, r_state, bd_ref)
    hg = _hgrn_stages(layer, off, ph_ref, lbp_ref, ng_ref, oh_ref, h_state)
    for stage, _ in enumerate(rw, start=1):
        for _ in range(HGRN_AFTER_RWKV_STAGE.count(stage)):
            next(hg, None)
    for _ in hg:
        pass


def _mixers(p3, main_blk, lora_blk, h_blk_w, h_blk, h_off, layer,
            w0, w2p, a0, a2p, k_k, k_a, r_k, lnx_w, lnx_b, ones_bd, lb_param, hgrn_g):
    B, T, _ = p3.shape
    W = w0.shape[1]
    HW = hgrn_g.shape[1]
    C = CHUNK
    NU = B * (W // QW)
    L = lb_param.shape[0]
    full = lambda shape: pl.BlockSpec(shape, lambda c: (0,) * len(shape))
    return pl.pallas_call(
        functools.partial(_mixers_kernel, layer, h_off),
        grid=(T // C,),
        in_specs=[
            pl.BlockSpec((B, C, 4 * W), lambda c: (0, c, main_blk)),
            pl.BlockSpec((B, C, LORA_BLK), lambda c: (0, c, lora_blk)),
            pl.BlockSpec((B, C, h_blk_w), lambda c: (0, c, h_blk)),
            full((1, W)), full((LANE, W)), full((1, W)), full((LORA_BLK, W)),
            full((1, W)), full((1, W)), full((1, W)), full((1, W)), full((1, W)),
            full((QW, QW)), full((L, HW)), full((1, HW)),
        ],
        out_specs=[pl.BlockSpec((B, C, W), lambda c: (0, c, 0)),
                   pl.BlockSpec((B, C, HW), lambda c: (0, c, 0))],
        out_shape=[jax.ShapeDtypeStruct((B, T, W), BF16),
                   jax.ShapeDtypeStruct((B, T, HW), BF16)],
        scratch_shapes=[
            pltpu.VMEM((NU, QW, QW), F32),
            pltpu.VMEM((NU * len(BD_ROLES), QW, QW), BF16),
            pltpu.VMEM((B * (HW // HGRN_HEAD), HGRN_HEAD, HGRN_HEAD), F32),
        ],
        compiler_params=pltpu.CompilerParams(
            dimension_semantics=("arbitrary",), vmem_limit_bytes=VMEM_LIMIT),
        name="mixers",
    )(p3, p3, p3, w0, w2p, a0, a2p, k_k, k_a, r_k, lnx_w, lnx_b, ones_bd, lb_param, hgrn_g)


def _outproj_kernel(final, yr_ref, yh_ref, x_ref, wr_ref, wh_ref, g_ref, o_ref):
    acc = jnp.dot(yr_ref[...], wr_ref[...], preferred_element_type=F32)
    acc += jnp.dot(yh_ref[...], wh_ref[...], preferred_element_type=F32)
    h = x_ref[...] + acc
    if final:
        ms = jnp.mean(h * h, axis=-1, keepdims=True)
        h = h * lax.rsqrt(ms + NORM_EPS) * g_ref[...]
    o_ref[...] = h


def _outproj(yr, yh, x2, wr, wh, g, tm, final):
    m, d = x2.shape
    wr_w = yr.shape[1]
    wh_w = yh.shape[1]
    return pl.pallas_call(
        functools.partial(_outproj_kernel, final),
        grid=(m // tm,),
        in_specs=[
            pl.BlockSpec((tm, wr_w), lambda i: (i, 0)),
            pl.BlockSpec((tm, wh_w), lambda i: (i, 0)),
            pl.BlockSpec((tm, d), lambda i: (i, 0)),
            pl.BlockSpec((wr_w, d), lambda i: (0, 0)),
            pl.BlockSpec((wh_w, d), lambda i: (0, 0)),
            pl.BlockSpec((1, d), lambda i: (0, 0)),
        ],
        out_specs=pl.BlockSpec((tm, d), lambda i: (i, 0)),
        out_shape=jax.ShapeDtypeStruct((m, d), F32),
        compiler_params=pltpu.CompilerParams(
            dimension_semantics=("arbitrary",), vmem_limit_bytes=VMEM_LIMIT),
        name="outproj",
    )(yr, yh, x2, wr, wh, g)


def kernel(x, norm_g, w_in, mu, w0, w2, a0, a2, k_k, k_a, r_k, lnx_w, lnx_b,
           hgrn_norm_g, lb_param, w_out, final_g):
    B, T, D = x.shape
    depth = w_in.shape[0]
    RW = w0.shape[1]
    HW = hgrn_norm_g.shape[1]
    n_r = 4 * RW + 2 * LORA
    tn = 768
    n_p = pl.cdiv(w_in.shape[2], tn) * tn
    h_start = (n_r // LANE) * LANE
    h_off = n_r - h_start
    h_blk_w = n_p - h_start
    assert (4 * RW) % LORA_BLK == 0 and 2 * LORA <= LORA_BLK and w2.shape[1] == LORA <= LANE
    assert h_start % h_blk_w == 0 and h_off + 4 * HW <= h_blk_w

    row = lambda z: z.reshape(1, -1).astype(F32)
    zr = lambda rows: jnp.zeros((rows, RW), F32)
    ones_bd = (jnp.arange(QW)[:, None] // RWKV_HEAD == jnp.arange(QW)[None, :] // RWKV_HEAD).astype(BF16)

    w_in_t = jnp.swapaxes(w_in, 1, 2)
    h = x.reshape(B * T, D)
    for l in range(depth):
        mu_full = jnp.concatenate([mu[l].astype(F32), jnp.zeros((n_p - n_r,), F32)]).reshape(1, -1)
        w2p = jnp.concatenate([w2[l].astype(F32), zr(LANE - LORA)], axis=0).astype(BF16)
        a2p = jnp.concatenate([zr(LORA), a2[l].astype(F32), zr(LORA_BLK - 2 * LORA)],
                              axis=0).astype(BF16)

        p = _inproj(h, row(norm_g[l]), w_in_t, l, mu_full, T, tm=1024, tn=tn, groups=4)
        p3 = p.reshape(B, T, n_p)
        y_r, y_h = _mixers(p3, 0, (4 * RW) // LORA_BLK, h_blk_w, h_start // h_blk_w, h_off, l,
                           row(w0[l]), w2p, row(a0[l]), a2p, row(k_k[l]), row(k_a[l]),
                           row(r_k[l]), row(lnx_w[l]), row(lnx_b[l]), ones_bd,
                           lb_param.astype(F32), row(hgrn_norm_g[l]))
        wo = w_out[l].astype(BF16)
        h = _outproj(y_r.reshape(B * T, RW), y_h.reshape(B * T, HW), h,
                     wo[:RW], wo[RW:], row(final_g), tm=512, final=(l == depth - 1))
    return h.reshape(B, T, D)
```

```python
import functools
import math

import jax
import jax.numpy as jnp
from jax import lax
from jax.experimental import pallas as pl
from jax.experimental.pallas import tpu as pltpu

F32 = jnp.float32
BF16 = jnp.bfloat16

NORM_EPS = 1e-6
LNX_EPS = 64e-5
RWKV_HEAD = 64
HGRN_HEAD = 128
LORA = 96
LORA_BLK = 256
LANE = 128
CHUNK = 64
HGRN_SUB = 16
QUAD = 4
QW = QUAD * RWKV_HEAD
BD_ROLES = ("B", "K", "V", "N", "Z", "U")
HGRN_AFTER_RWKV_STAGE = tuple(range(6, 18))
EXP_M05 = math.exp(-0.5)
VMEM_LIMIT = 48 * 1024 * 1024


def _mm(a, b):
    return jnp.dot(a.astype(BF16), b.astype(BF16), preferred_element_type=F32)


def _mm_nt(a, b):
    return lax.dot_general(a.astype(BF16), b.astype(BF16), (((1,), (1,)), ((), ())),
                           preferred_element_type=F32)


def _mm_tn(a, b):
    return lax.dot_general(a.astype(BF16), b.astype(BF16), (((0,), (0,)), ((), ())),
                           preferred_element_type=F32)


def _cumsum_rows(x, tril_bf16):
    hi = x.astype(BF16)
    lo = (x - hi.astype(F32)).astype(BF16)
    dot = functools.partial(jnp.dot, preferred_element_type=F32)
    return dot(tril_bf16, hi) + dot(tril_bf16, lo)


def _sigmoid(x):
    return 0.5 + 0.5 * jnp.tanh(0.5 * x)


def _inproj_kernel(n_valid, seq_len, x_ref, g_ref, w_ref, mu_ref, o_ref, hn_ref, carry_ref):
    q = pl.program_id(0)
    j = pl.program_id(1)
    i = pl.program_id(2)
    tm, tn = o_ref.shape

    @pl.when((q == 0) & (j == 0) & (i == 0))
    def _():
        carry_ref[...] = jnp.zeros_like(carry_ref)

    @pl.when(j == 0)
    def _():
        x = x_ref[...]
        ms = jnp.mean(x * x, axis=-1, keepdims=True)
        hn_ref[i] = (x * lax.rsqrt(ms + NORM_EPS) * g_ref[...]).astype(BF16)

    w_row = j * tn + lax.broadcasted_iota(jnp.int32, (tn, 1), 0)
    w = jnp.where(w_row < n_valid, w_ref[...], 0.0)
    acc = _mm_nt(hn_ref[i], w)

    first_row = (q * pl.num_programs(2) + i) * tm
    prev_last = jnp.where(first_row % seq_len == 0, 0.0, carry_ref[j])
    carry_ref[j] = acc[tm - 1:tm, :]
    is_row0 = lax.broadcasted_iota(jnp.int32, (tm, 1), 0) == 0
    prev = jnp.where(is_row0, prev_last, pltpu.roll(acc, 1, axis=0))
    o_ref[...] = acc + mu_ref[...] * (prev - acc)


def _inproj(x2, g, wt_all, layer, mu_full, seq_len, tm, tn, groups):
    m, d = x2.shape
    n_valid = wt_all.shape[1]
    n = pl.cdiv(n_valid, tn) * tn
    ni = m // (groups * tm)
    assert seq_len % tm == 0 and mu_full.shape == (1, n)
    x_map = lambda q, j, i: (q * ni + jnp.where(j == 0, i, ni - 1), 0)
    return pl.pallas_call(
        functools.partial(_inproj_kernel, n_valid, seq_len),
        grid=(groups, n // tn, ni),
        in_specs=[
            pl.BlockSpec((tm, d), x_map),
            pl.BlockSpec((1, d), lambda q, j, i: (0, 0)),
            pl.BlockSpec((None, tn, d), lambda q, j, i: (layer, j, 0)),
            pl.BlockSpec((1, tn), lambda q, j, i: (0, j)),
        ],
        out_specs=pl.BlockSpec((tm, tn), lambda q, j, i: (q * ni + i, j)),
        out_shape=jax.ShapeDtypeStruct((m, n), F32),
        scratch_shapes=[pltpu.VMEM((ni, tm, d), BF16), pltpu.VMEM((n // tn, 1, tn), F32)],
        compiler_params=pltpu.CompilerParams(
            dimension_semantics=("arbitrary", "arbitrary", "arbitrary"),
            vmem_limit_bytes=VMEM_LIMIT),
        name="inproj",
    )(x2, g, wt_all, mu_full)


def _rwkv_stages(pm_ref, pl_ref, w0_ref, w2_ref, a0_ref, a2_ref,
                 kk_ref, ka_ref, rk_ref, lw_ref, lb_ref, ones_ref,
                 o_ref, state, bd_ref):
    C = CHUNK
    NB = pm_ref.shape[0]
    W = w0_ref.shape[1]
    NQ = W // QW

    t_i = lax.broadcasted_iota(jnp.int32, (C, C), 0)
    s_i = lax.broadcasted_iota(jnp.int32, (C, C), 1)
    tril = (t_i >= s_i).astype(BF16)
    ones_bd = ones_ref[...]
    dot = functools.partial(jnp.dot, preferred_element_type=F32)

    lane = lax.broadcasted_iota(jnp.int32, (C, QW), 1)
    lane_s = lane % RWKV_HEAD
    trow = lax.broadcasted_iota(jnp.int32, (C, QW), 0)
    strict = lane_s < trow
    incl = lane_s <= trow
    eye_q = jnp.where(lane_s == trow, 1.0, 0.0).astype(F32)
    bd_mask = (lax.broadcasted_iota(jnp.int32, (QW, QW), 0) // RWKV_HEAD
               == lax.broadcasted_iota(jnp.int32, (QW, QW), 1) // RWKV_HEAD)

    def bd(u, role, x):
        slot = u["ui"] * len(BD_ROLES) + BD_ROLES.index(role)
        xb = x.astype(BF16)
        for h in range(QUAD):
            hs = slice(h * RWKV_HEAD, (h + 1) * RWKV_HEAD)
            bd_ref[slot, h * C:(h + 1) * C, hs] = xb[:, hs]
        return bd_ref[slot]

    def segsum_all(xs):
        out = dot(jnp.concatenate([x.astype(BF16) for x in xs], axis=0), ones_bd)
        return [out[i * C:(i + 1) * C] for i in range(len(xs))]

    xl = jnp.concatenate([pl_ref[b] for b in range(NB)], axis=0)
    z = w0_ref[...] + _mm(jnp.tanh(xl[:, :LANE]), w2_ref[...])
    ld_all = -EXP_M05 * _sigmoid(z)
    a_all = _sigmoid(a0_ref[...] + _mm(xl, a2_ref[...]))
    units = []
    for b in range(NB):
        r = pm_ref[b, :, 0 * W:1 * W]
        k = pm_ref[b, :, 1 * W:2 * W]
        v = pm_ref[b, :, 2 * W:3 * W]
        gate = pm_ref[b, :, 3 * W:4 * W]
        ld = ld_all[b * C:(b + 1) * C]
        a = a_all[b * C:(b + 1) * C]
        G = _cumsum_rows(ld, tril)
        Gx = G - ld
        GC = G[C - 1:C, :]
        for q in range(NQ):
            sl = slice(q * QW, (q + 1) * QW)
            units.append(dict(b=b, ui=b * NQ + q, sl=sl, r=r[:, sl], k=k[:, sl], v=v[:, sl],
                              g=gate[:, sl], a=a[:, sl], G=G[:, sl], Gx=Gx[:, sl], GC=GC[:, sl]))
    yield

    for u in units:
        u["kk"] = u["k"] * kk_ref[:, u["sl"]]
    for u, n2 in zip(units, segsum_all([u["kk"] * u["kk"] for u in units])):
        u["n2"] = n2
    yield

    def unit_chain(u):
        sl = u["sl"]
        kkn = u["kk"] * lax.rsqrt(jnp.maximum(u["n2"], 1e-24))
        k2 = u["k"] * (1.0 + (u["a"] - 1.0) * ka_ref[:, sl])
        av = -kkn
        bv = kkn * u["a"]
        einv = jnp.exp(-u["G"])
        eC = jnp.exp(u["GC"]) * einv
        u["rkk"] = u["r"] * k2 * rk_ref[:, sl]
        BDB = bd(u, "B", bv * einv)
        BDK = bd(u, "K", k2 * einv)
        BKe = jnp.concatenate([bv * eC, k2 * eC], axis=0).astype(BF16)
        BDV = bd(u, "V", u["v"])
        lhs = jnp.concatenate([av * jnp.exp(u["Gx"]), u["r"] * jnp.exp(u["G"])],
                              axis=0).astype(BF16)
        pc_col = jnp.transpose(jnp.broadcast_to(jnp.exp(u["GC"]), (LANE, QW)))
        PCc = jnp.concatenate([pc_col] * (QW // LANE), axis=1)
        yield
        SB = _mm_nt(lhs, BDB)
        yield
        SK = _mm_nt(lhs, BDK)
        yield
        N = jnp.where(strict, SB[:C], 0.0)
        Lrb = jnp.where(incl, SB[C:], 0.0)
        P = eye_q + N
        Np = _mm(N, bd(u, "N", N))
        yield
        akl = jnp.concatenate([jnp.where(strict, SK[:C], 0.0),
                               jnp.where(incl, SK[C:], 0.0)], axis=0)
        both = _mm(akl, BDV)
        yield
        for _ in range(4):
            out = _mm(jnp.concatenate([P, Np], axis=0), bd(u, "N", Np))
            P = P + out[:C]
            Np = out[C:]
            yield
        P = P + _mm(P, bd(u, "N", Np))
        yield
        H = state[u["ui"]]
        zy = _mm(lhs, H)
        yield
        U = _mm(P, bd(u, "Z", zy[:C] + both[:C]))
        yield
        u["Y"] = zy[C:] + _mm(Lrb, bd(u, "U", U)) + both[C:]
        yield
        upd = _mm_tn(BKe, jnp.concatenate([U, u["v"]], axis=0))
        state[u["ui"]] = H * PCc + jnp.where(bd_mask, upd, 0.0)

    chains = [unit_chain(u) for u in units]
    live = list(range(len(chains)))
    while live:
        for i in list(live):
            if next(chains[i], "done") == "done":
                live.remove(i)
        yield

    for u, s in zip(units, segsum_all([u["Y"] for u in units])):
        u["yc"] = u["Y"] - s * (1.0 / RWKV_HEAD)
    yield
    for u, s in zip(units, segsum_all([u["yc"] * u["yc"] for u in units])):
        u["var"] = s * (1.0 / RWKV_HEAD)
    yield
    for u, s in zip(units, segsum_all([u["rkk"] for u in units])):
        u["bonus"] = s
    yield
    for u in units:
        sl = u["sl"]
        yn = u["yc"] * lax.rsqrt(u["var"] + LNX_EPS) * lw_ref[:, sl] + lb_ref[:, sl]
        res = (yn + u["bonus"] * u["v"]) * (u["g"] * _sigmoid(u["g"]))
        o_ref[u["b"], :, sl] = res.astype(o_ref.dtype)


def _hgrn_stages(layer, off, ph_ref, lbp_ref, ng_ref, o_ref, state):
    C = CHUNK
    NB = ph_ref.shape[0]
    W = ng_ref.shape[1]
    D = HGRN_HEAD
    NH = W // D
    SUB = HGRN_SUB
    NBLK = C // SUB
    GRP = 2 * LANE // D

    lp = lbp_ref[...]
    e = jnp.exp(lp - jnp.max(lp, axis=0, keepdims=True))
    lb = jnp.sum(e[:layer + 1], axis=0, keepdims=True) / jnp.sum(e, axis=0, keepdims=True)

    t_i = lax.broadcasted_iota(jnp.int32, (C, C), 0)
    s_i = lax.broadcasted_iota(jnp.int32, (C, C), 1)
    causal = t_i >= s_i
    tril = causal.astype(BF16)

    units = []
    for b in range(NB):
        x = ph_ref[b]
        qv = x[:, off + 0 * W:off + 1 * W]
        fr = x[:, off + 1 * W:off + 2 * W]
        iv = x[:, off + 2 * W:off + 3 * W]
        gate = x[:, off + 3 * W:off + 4 * W]
        f = lb + (1.0 - lb) * _sigmoid(fr)
        kx = 1.0 - f
        G = _cumsum_rows(jnp.log(f), tril)
        GC = G[C - 1:C, :]
        PC = jnp.exp(GC)
        ivb = iv.astype(BF16)
        zero = jnp.zeros((SUB, W), BF16)
        qs, ks, qt_rows, k_rows = [], [], [], []
        for j in range(NBLK):
            lo, hi = j * SUB, (j + 1) * SUB
            qj = qv[lo:hi] * jnp.exp(G[lo:hi] - G[lo - 1:lo, :]) if j else qv[lo:hi] * jnp.exp(G[lo:hi])
            kj = kx[lo:hi] * jnp.exp(G[lo - 1:lo, :] - G[lo:hi]) if j else kx[lo:hi] * jnp.exp(-G[lo:hi])
            if j:
                step = jnp.exp(G[lo - 1:lo, :] - (G[lo - SUB - 1:lo - SUB, :] if j > 1 else 0.0))
                k_rows = [kr * step for kr in k_rows]
                qt_rows.append(qj * jnp.exp(G[lo - 1:lo, :]))
            else:
                qt_rows.append(qj)
            k_rows.append(kj)
            g0 = (j // GRP) * GRP
            qs.append(jnp.concatenate([zero] * (j - g0) + [qj.astype(BF16)] + [zero] * (g0 + GRP - 1 - j),
                                      axis=0))
            ks.append(jnp.concatenate([kr.astype(BF16) for kr in k_rows] + [zero] * (g0 + GRP - 1 - j),
                                      axis=0))
        qt = jnp.concatenate(qt_rows, axis=0).astype(BF16)
        to_end = jnp.exp(GC - G[C - SUB - 1:C - SUB, :])
        kd = jnp.concatenate([(kr * to_end).astype(BF16) for kr in k_rows], axis=0)
        for h in range(NH):
            sl = slice(h * D, (h + 1) * D)
            units.append(dict(b=b, h=h, sl=sl, qt=qt[:, sl], kd=kd[:, sl], iv=ivb[:, sl],
                              PC=PC[:, sl], g=gate[:, sl],
                              qcat=[jnp.concatenate([q[:, sl] for q in qs[g:g + GRP]], axis=1)
                                    for g in range(0, NBLK, GRP)],
                              kcat=[jnp.concatenate([k[:, sl] for k in ks[g:g + GRP]], axis=1)
                                    for g in range(0, NBLK, GRP)]))
        yield

    halves = (units[:len(units) // 2], units[len(units) // 2:])
    for part in halves:
        for u in part:
            u["A"] = [_mm_nt(q, k) for q, k in zip(u["qcat"], u["kcat"])]
        yield
    for part in halves:
        for u in part:
            u["S"] = state[u["b"] * NH + u["h"]]
            u["inter"] = _mm_nt(u["qt"], u["S"])
        yield
    for part in halves:
        for u in part:
            rows = []
            for g, A in enumerate(u["A"]):
                hi = (g + 1) * GRP * SUB
                mask = causal[hi - GRP * SUB:hi, :hi]
                rows.append(_mm(jnp.where(mask, A, 0.0), u["iv"][:hi]))
            u["o"] = jnp.concatenate(rows, axis=0) + u["inter"]
        yield
    for part in halves:
        for u in part:
            state[u["b"] * NH + u["h"]] = u["S"] * u["PC"] + _mm_tn(u["iv"], u["kd"])
        yield
    for part in halves:
        for u in part:
            o = u["o"]
            g = u["g"]
            ms = jnp.mean(o * o, axis=-1, keepdims=True)
            res = o * lax.rsqrt(ms + NORM_EPS) * ng_ref[:, u["sl"]] * (g * _sigmoid(g))
            o_ref[u["b"], :, u["sl"]] = res.astype(o_ref.dtype)
        yield


def _mixers_kernel(layer, off, pm_ref, pl_ref, ph_ref, w0_ref, w2_ref, a0_ref, a2_ref,
                   kk_ref, ka_ref, rk_ref, lw_ref, lb_ref, ones_ref, lbp_ref, ng_ref,
                   or_ref, oh_ref, r_state, bd_ref, h_state):
    @pl.when(pl.program_id(0) == 0)
    def _():
        for ref in (r_state, bd_ref, h_state):
            ref[...] = jnp.zeros_like(ref)

    rw = _rwkv_stages(pm_ref, pl_ref, w0_ref, w2_ref, a0_ref, a2_ref, kk_ref, ka_ref, rk_ref,
                      lw_ref, lb_ref, ones_ref, or_ref, r_state, bd_ref)
    hg = _hgrn_stages(layer, off, ph_ref, lbp_ref, ng_ref, oh_ref, h_state)
    for stage, _ in enumerate(rw, start=1):
        for _ in range(HGRN_AFTER_RWKV_STAGE.count(stage)):
            next(hg, None)
    for _ in hg:
        pass


def _mixers(p3, main_blk, lora_blk, h_blk_w, h_blk, h_off, layer,
            w0, w2p, a0, a2p, k_k, k_a, r_k, lnx_w, lnx_b, ones_bd, lb_param, hgrn_g):
    B, T, _ = p3.shape
    W = w0.shape[1]
    HW = hgrn_g.shape[1]
    C = CHUNK
    NU = B * (W // QW)
    L = lb_param.shape[0]
    full = lambda shape: pl.BlockSpec(shape, lambda c: (0,) * len(shape))
    return pl.pallas_call(
        functools.partial(_mixers_kernel, layer, h_off),
        grid=(T // C,),
        in_specs=[
            pl.BlockSpec((B, C, 4 * W), lambda c: (0, c, main_blk)),
            pl.BlockSpec((B, C, LORA_BLK), lambda c: (0, c, lora_blk)),
            pl.BlockSpec((B, C, h_blk_w), lambda c: (0, c, h_blk)),
            full((1, W)), full((LANE, W)), full((1, W)), full((LORA_BLK, W)),
            full((1, W)), full((1, W)), full((1, W)), full((1, W)), full((1, W)),
            full((QW, QW)), full((L, HW)), full((1, HW)),
        ],
        out_specs=[pl.BlockSpec((B, C, W), lambda c: (0, c, 0)),
                   pl.BlockSpec((B, C, HW), lambda c: (0, c, 0))],
        out_shape=[jax.ShapeDtypeStruct((B, T, W), BF16),
                   jax.ShapeDtypeStruct((B, T, HW), BF16)],
        scratch_shapes=[
            pltpu.VMEM((NU, QW, QW), F32),
            pltpu.VMEM((NU * len(BD_ROLES), QW, QW), BF16),
            pltpu.VMEM((B * (HW // HGRN_HEAD), HGRN_HEAD, HGRN_HEAD), F32),
        ],
        compiler_params=pltpu.CompilerParams(
            dimension_semantics=("arbitrary",), vmem_limit_bytes=VMEM_LIMIT),
        name="mixers",
    )(p3, p3, p3, w0, w2p, a0, a2p, k_k, k_a, r_k, lnx_w, lnx_b, ones_bd, lb_param, hgrn_g)


def _outproj_kernel(final, yr_ref, yh_ref, x_ref, wr_ref, wh_ref, g_ref, o_ref):
    acc = jnp.dot(yr_ref[...], wr_ref[...], preferred_element_type=F32)
    acc += jnp.dot(yh_ref[...], wh_ref[...], preferred_element_type=F32)
    h = x_ref[...] + acc
    if final:
        ms = jnp.mean(h * h, axis=-1, keepdims=True)
        h = h * lax.rsqrt(ms + NORM_EPS) * g_ref[...]
    o_ref[...] = h


def _outproj(yr, yh, x2, wr, wh, g, tm, final):
    m, d = x2.shape
    wr_w = yr.shape[1]
    wh_w = yh.shape[1]
    return pl.pallas_call(
        functools.partial(_outproj_kernel, final),
        grid=(m // tm,),
        in_specs=[
            pl.BlockSpec((tm, wr_w), lambda i: (i, 0)),
            pl.BlockSpec((tm, wh_w), lambda i: (i, 0)),
            pl.BlockSpec((tm, d), lambda i: (i, 0)),
            pl.BlockSpec((wr_w, d), lambda i: (0, 0)),
            pl.BlockSpec((wh_w, d), lambda i: (0, 0)),
            pl.BlockSpec((1, d), lambda i: (0, 0)),
        ],
        out_specs=pl.BlockSpec((tm, d), lambda i: (i, 0)),
        out_shape=jax.ShapeDtypeStruct((m, d), F32),
        compiler_params=pltpu.CompilerParams(
            dimension_semantics=("arbitrary",), vmem_limit_bytes=VMEM_LIMIT),
        name="outproj",
    )(yr, yh, x2, wr, wh, g)


def kernel(x, norm_g, w_in, mu, w0, w2, a0, a2, k_k, k_a, r_k, lnx_w, lnx_b,
           hgrn_norm_g, lb_param, w_out, final_g):
    B, T, D = x.shape
    depth = w_in.shape[0]
    RW = w0.shape[1]
    HW = hgrn_norm_g.shape[1]
    n_r = 4 * RW + 2 * LORA
    tn = 768
    n_p = pl.cdiv(w_in.shape[2], tn) * tn
    h_start = (n_r // LANE) * LANE
    h_off = n_r - h_start
    h_blk_w = n_p - h_start
    assert (4 * RW) % LORA_BLK == 0 and 2 * LORA <= LORA_BLK and w2.shape[1] == LORA <= LANE
    assert h_start % h_blk_w == 0 and h_off + 4 * HW <= h_blk_w

    row = lambda z: z.reshape(1, -1).astype(F32)
    zr = lambda rows: jnp.zeros((rows, RW), F32)
    ones_bd = (jnp.arange(QW)[:, None] // RWKV_HEAD == jnp.arange(QW)[None, :] // RWKV_HEAD).astype(BF16)

    w_in_t = jnp.swapaxes(w_in, 1, 2)
    h = x.reshape(B * T, D)
    for l in range(depth):
        mu_full = jnp.concatenate([mu[l].astype(F32), jnp.zeros((n_p - n_r,), F32)]).reshape(1, -1)
        w2p = jnp.concatenate([w2[l].astype(F32), zr(LANE - LORA)], axis=0).astype(BF16)
        a2p = jnp.concatenate([zr(LORA), a2[l].astype(F32), zr(LORA_BLK - 2 * LORA)],
                              axis=0).astype(BF16)

        p = _inproj(h, row(norm_g[l]), w_in_t, l, mu_full, T, tm=1024, tn=tn, groups=4)
        p3 = p.reshape(B, T, n_p)
        y_r, y_h = _mixers(p3, 0, (4 * RW) // LORA_BLK, h_blk_w, h_start // h_blk_w, h_off, l,
                           row(w0[l]), w2p, row(a0[l]), a2p, row(k_k[l]), row(k_a[l]),
                           row(r_k[l]), row(lnx_w[l]), row(lnx_b[l]), ones_bd,
                           lb_param.astype(F32), row(hgrn_norm_g[l]))
        wo = w_out[l].astype(BF16)
        h = _outproj(y_r.reshape(B * T, RW), y_h.reshape(B * T, HW), h,
                     wo[:RW], wo[RW:], row(final_g), tm=512, final=(l == depth - 1))
    return h.reshape(B, T, D)
```

```python
import functools
import math

import jax
import jax.numpy as jnp
from jax import lax
from jax.experimental import pallas as pl
from jax.experimental.pallas import tpu as pltpu

F32 = jnp.float32
BF16 = jnp.bfloat16

NORM_EPS = 1e-6
LNX_EPS = 64e-5
RWKV_HEAD = 64
HGRN_HEAD = 128
LORA = 96
LORA_BLK = 256
LANE = 128
CHUNK = 64
HGRN_SUB = 16
QUAD = 4
QW = QUAD * RWKV_HEAD
BD_ROLES = ("B", "K", "V", "N", "Z", "U")
HGRN_AFTER_RWKV_STAGE = tuple(range(6, 18))
EXP_M05 = math.exp(-0.5)
VMEM_LIMIT = 48 * 1024 * 1024
VMEM_MARGIN = 6 * 1024 * 1024


def _mm(a, b):
    return jnp.dot(a.astype(BF16), b.astype(BF16), preferred_element_type=F32)


def _mm_nt(a, b):
    return lax.dot_general(a.astype(BF16), b.astype(BF16), (((1,), (1,)), ((), ())),
                           preferred_element_type=F32)


def _mm_tn(a, b):
    return lax.dot_general(a.astype(BF16), b.astype(BF16), (((0,), (0,)), ((), ())),
                           preferred_element_type=F32)


def _cumsum_rows(x, tril_bf16):
    hi = x.astype(BF16)
    lo = (x - hi.astype(F32)).astype(BF16)
    dot = functools.partial(jnp.dot, preferred_element_type=F32)
    return dot(tril_bf16, hi) + dot(tril_bf16, lo)


def _sigmoid(x):
    return 0.5 + 0.5 * jnp.tanh(0.5 * x)


def _inproj_kernel(n_valid, seq_len, x_ref, g_ref, w_ref, mu_ref, o_ref, hn_ref, carry_ref):
    q = pl.program_id(0)
    j = pl.program_id(1)
    i = pl.program_id(2)
    tm, tn = o_ref.shape

    @pl.when((q == 0) & (j == 0) & (i == 0))
    def _():
        carry_ref[...] = jnp.zeros_like(carry_ref)

    @pl.when(j == 0)
    def _():
        x = x_ref[...]
        ms = jnp.mean(x * x, axis=-1, keepdims=True)
        hn_ref[i] = (x * lax.rsqrt(ms + NORM_EPS) * g_ref[...]).astype(BF16)

    w_row = j * tn + lax.broadcasted_iota(jnp.int32, (tn, 1), 0)
    w = jnp.where(w_row < n_valid, w_ref[...], 0.0)
    acc = _mm_nt(hn_ref[i], w)

    first_row = (q * pl.num_programs(2) + i) * tm
    prev_last = jnp.where(first_row % seq_len == 0, 0.0, carry_ref[j])
    carry_ref[j] = acc[tm - 1:tm, :]
    is_row0 = lax.broadcasted_iota(jnp.int32, (tm, 1), 0) == 0
    prev = jnp.where(is_row0, prev_last, pltpu.roll(acc, 1, axis=0))
    o_ref[...] = acc + mu_ref[...] * (prev - acc)


def _inproj(x2, g, wt_all, layer, mu_full, seq_len, tm, tn, groups):
    m, d = x2.shape
    n_valid = wt_all.shape[1]
    n = pl.cdiv(n_valid, tn) * tn
    ni = m // (groups * tm)
    assert seq_len % tm == 0 and mu_full.shape == (1, n)
    vmem = ni * tm * d * 2 + 2 * 4 * (tm * d + tn * d + tm * tn) + VMEM_MARGIN
    x_map = lambda q, j, i: (q * ni + jnp.where(j == 0, i, ni - 1), 0)
    return pl.pallas_call(
        functools.partial(_inproj_kernel, n_valid, seq_len),
        grid=(groups, n // tn, ni),
        in_specs=[
            pl.BlockSpec((tm, d), x_map),
            pl.BlockSpec((1, d), lambda q, j, i: (0, 0)),
            pl.BlockSpec((None, tn, d), lambda q, j, i: (layer, j, 0)),
            pl.BlockSpec((1, tn), lambda q, j, i: (0, j)),
        ],
        out_specs=pl.BlockSpec((tm, tn), lambda q, j, i: (q * ni + i, j)),
        out_shape=jax.ShapeDtypeStruct((m, n), F32),
        scratch_shapes=[pltpu.VMEM((ni, tm, d), BF16), pltpu.VMEM((n // tn, 1, tn), F32)],
        compiler_params=pltpu.CompilerParams(
            dimension_semantics=("arbitrary", "arbitrary", "arbitrary"),
            vmem_limit_bytes=vmem),
        name="inproj",
    )(x2, g, wt_all, mu_full)


def _rwkv_stages(pm_ref, pl_ref, w0_ref, w2_ref, a0_ref, a2_ref,
                 kk_ref, ka_ref, rk_ref, lw_ref, lb_ref, ones_ref,
                 o_ref, state, bd_ref):
    C = CHUNK
    NB = pm_ref.shape[0]
    W = w0_ref.shape[1]
    NQ = W // QW

    t_i = lax.broadcasted_iota(jnp.int32, (C, C), 0)
    s_i = lax.broadcasted_iota(jnp.int32, (C, C), 1)
    tril = (t_i >= s_i).astype(BF16)
    ones_bd = ones_ref[...]
    dot = functools.partial(jnp.dot, preferred_element_type=F32)

    lane = lax.broadcasted_iota(jnp.int32, (C, QW), 1)
    lane_s = lane % RWKV_HEAD
    trow = lax.broadcasted_iota(jnp.int32, (C, QW), 0)
    strict = lane_s < trow
    incl = lane_s <= trow
    eye_q = jnp.where(lane_s == trow, 1.0, 0.0).astype(F32)
    bd_mask = (lax.broadcasted_iota(jnp.int32, (QW, QW), 0) // RWKV_HEAD
               == lax.broadcasted_iota(jnp.int32, (QW, QW), 1) // RWKV_HEAD)

    def bd(u, role, x):
        slot = u["ui"] * len(BD_ROLES) + BD_ROLES.index(role)
        xb = x.astype(BF16)
        for h in range(QUAD):
            hs = slice(h * RWKV_HEAD, (h + 1) * RWKV_HEAD)
            bd_ref[slot, h * C:(h + 1) * C, hs] = xb[:, hs]
        return bd_ref[slot]

    def segsum_all(xs):
        out = dot(jnp.concatenate([x.astype(BF16) for x in xs], axis=0), ones_bd)
        return [out[i * C:(i + 1) * C] for i in range(len(xs))]

    xl = jnp.concatenate([pl_ref[b] for b in range(NB)], axis=0)
    z = w0_ref[...] + _mm(jnp.tanh(xl[:, :LANE]), w2_ref[...])
    ld_all = -EXP_M05 * _sigmoid(z)
    a_all = _sigmoid(a0_ref[...] + _mm(xl, a2_ref[...]))
    units = []
    for b in range(NB):
        r = pm_ref[b, :, 0 * W:1 * W]
        k = pm_ref[b, :, 1 * W:2 * W]
        v = pm_ref[b, :, 2 * W:3 * W]
        gate = pm_ref[b, :, 3 * W:4 * W]
        ld = ld_all[b * C:(b + 1) * C]
        a = a_all[b * C:(b + 1) * C]
        G = _cumsum_rows(ld, tril)
        Gx = G - ld
        GC = G[C - 1:C, :]
        for q in range(NQ):
            sl = slice(q * QW, (q + 1) * QW)
            units.append(dict(b=b, ui=b * NQ + q, sl=sl, r=r[:, sl], k=k[:, sl], v=v[:, sl],
                              g=gate[:, sl], a=a[:, sl], G=G[:, sl], Gx=Gx[:, sl], GC=GC[:, sl]))
    yield

    for u in units:
        u["kk"] = u["k"] * kk_ref[:, u["sl"]]
    for u, n2 in zip(units, segsum_all([u["kk"] * u["kk"] for u in units])):
        u["n2"] = n2
    yield

    def unit_chain(u):
        sl = u["sl"]
        kkn = u["kk"] * lax.rsqrt(jnp.maximum(u["n2"], 1e-24))
        k2 = u["k"] * (1.0 + (u["a"] - 1.0) * ka_ref[:, sl])
        av = -kkn
        bv = kkn * u["a"]
        einv = jnp.exp(-u["G"])
        eC = jnp.exp(u["GC"]) * einv
        u["rkk"] = u["r"] * k2 * rk_ref[:, sl]
        BDB = bd(u, "B", bv * einv)
        BDK = bd(u, "K", k2 * einv)
        BKe = jnp.concatenate([bv * eC, k2 * eC], axis=0).astype(BF16)
        BDV = bd(u, "V", u["v"])
        lhs = jnp.concatenate([av * jnp.exp(u["Gx"]), u["r"] * jnp.exp(u["G"])],
                              axis=0).astype(BF16)
        pc_col = jnp.transpose(jnp.broadcast_to(jnp.exp(u["GC"]), (LANE, QW)))
        PCc = jnp.concatenate([pc_col] * (QW // LANE), axis=1)
        yield
        SB = _mm_nt(lhs, BDB)
        yield
        SK = _mm_nt(lhs, BDK)
        yield
        N = jnp.where(strict, SB[:C], 0.0)
        Lrb = jnp.where(incl, SB[C:], 0.0)
        P = eye_q + N
        Np = _mm(N, bd(u, "N", N))
        yield
        akl = jnp.concatenate([jnp.where(strict, SK[:C], 0.0),
                               jnp.where(incl, SK[C:], 0.0)], axis=0)
        both = _mm(akl, BDV)
        yield
        for _ in range(4):
            out = _mm(jnp.concatenate([P, Np], axis=0), bd(u, "N", Np))
            P = P + out[:C]
            Np = out[C:]
            yield
        P = P + _mm(P, bd(u, "N", Np))
        yield
        H = state[u["ui"]]
        zy = _mm(lhs, H)
        yield
        U = _mm(P, bd(u, "Z", zy[:C] + both[:C]))
        yield
        u["Y"] = zy[C:] + _mm(Lrb, bd(u, "U", U)) + both[C:]
        yield
        upd = _mm_tn(BKe, jnp.concatenate([U, u["v"]], axis=0))
        state[u["ui"]] = H * PCc + jnp.where(bd_mask, upd, 0.0)

    chains = [unit_chain(u) for u in units]
    live = list(range(len(chains)))
    while live:
        for i in list(live):
            if next(chains[i], "done") == "done":
                live.remove(i)
        yield

    for u, s in zip(units, segsum_all([u["Y"] for u in units])):
        u["yc"] = u["Y"] - s * (1.0 / RWKV_HEAD)
    yield
    for u, s in zip(units, segsum_all([u["yc"] * u["yc"] for u in units])):
        u["var"] = s * (1.0 / RWKV_HEAD)
    yield
    for u, s in zip(units, segsum_all([u["rkk"] for u in units])):
        u["bonus"] = s
    yield
    for u in units:
        sl = u["sl"]
        yn = u["yc"] * lax.rsqrt(u["var"] + LNX_EPS) * lw_ref[:, sl] + lb_ref[:, sl]
        res = (yn + u["bonus"] * u["v"]) * (u["g"] * _sigmoid(u["g"]))
        o_ref[u["b"], :, sl] = res.astype(o_ref.dtype)


def _hgrn_stages(layer, off, ph_ref, lbp_ref, ng_ref, o_ref, state):
    C = CHUNK
    NB = ph_ref.shape[0]
    W = ng_ref.shape[1]
    D = HGRN_HEAD
    NH = W // D
    SUB = HGRN_SUB
    NBLK = C // SUB
    GRP = 2 * LANE // D

    lp = lbp_ref[...]
    e = jnp.exp(lp - jnp.max(lp, axis=0, keepdims=True))
    lb = jnp.sum(e[:layer + 1], axis=0, keepdims=True) / jnp.sum(e, axis=0, keepdims=True)

    t_i = lax.broadcasted_iota(jnp.int32, (C, C), 0)
    s_i = lax.broadcasted_iota(jnp.int32, (C, C), 1)
    causal = t_i >= s_i
    tril = causal.astype(BF16)

    units = []
    for b in range(NB):
        x = ph_ref[b]
        qv = x[:, off + 0 * W:off + 1 * W]
        fr = x[:, off + 1 * W:off + 2 * W]
        iv = x[:, off + 2 * W:off + 3 * W]
        gate = x[:, off + 3 * W:off + 4 * W]
        f = lb + (1.0 - lb) * _sigmoid(fr)
        kx = 1.0 - f
        G = _cumsum_rows(jnp.log(f), tril)
        GC = G[C - 1:C, :]
        PC = jnp.exp(GC)
        ivb = iv.astype(BF16)
        zero = jnp.zeros((SUB, W), BF16)
        qs, ks, qt_rows, k_rows = [], [], [], []
        for j in range(NBLK):
            lo, hi = j * SUB, (j + 1) * SUB
            qj = qv[lo:hi] * jnp.exp(G[lo:hi] - G[lo - 1:lo, :]) if j else qv[lo:hi] * jnp.exp(G[lo:hi])
            kj = kx[lo:hi] * jnp.exp(G[lo - 1:lo, :] - G[lo:hi]) if j else kx[lo:hi] * jnp.exp(-G[lo:hi])
            if j:
                step = jnp.exp(G[lo - 1:lo, :] - (G[lo - SUB - 1:lo - SUB, :] if j > 1 else 0.0))
                k_rows = [kr * step for kr in k_rows]
                qt_rows.append(qj * jnp.exp(G[lo - 1:lo, :]))
            else:
                qt_rows.append(qj)
            k_rows.append(kj)
            g0 = (j // GRP) * GRP
            qs.append(jnp.concatenate([zero] * (j - g0) + [qj.astype(BF16)] + [zero] * (g0 + GRP - 1 - j),
                                      axis=0))
            ks.append(jnp.concatenate([kr.astype(BF16) for kr in k_rows] + [zero] * (g0 + GRP - 1 - j),
                                      axis=0))
        qt = jnp.concatenate(qt_rows, axis=0).astype(BF16)
        to_end = jnp.exp(GC - G[C - SUB - 1:C - SUB, :])
        kd = jnp.concatenate([(kr * to_end).astype(BF16) for kr in k_rows], axis=0)
        for h in range(NH):
            sl = slice(h * D, (h + 1) * D)
            units.append(dict(b=b, h=h, sl=sl, qt=qt[:, sl], kd=kd[:, sl], iv=ivb[:, sl],
                              PC=PC[:, sl], g=gate[:, sl],
                              qcat=[jnp.concatenate([q[:, sl] for q in qs[g:g + GRP]], axis=1)
                                    for g in range(0, NBLK, GRP)],
                              kcat=[jnp.concatenate([k[:, sl] for k in ks[g:g + GRP]], axis=1)
                                    for g in range(0, NBLK, GRP)]))
        yield

    halves = (units[:len(units) // 2], units[len(units) // 2:])
    for part in halves:
        for u in part:
            u["A"] = [_mm_nt(q, k) for q, k in zip(u["qcat"], u["kcat"])]
        yield
    for part in halves:
        for u in part:
            u["S"] = state[u["b"] * NH + u["h"]]
            u["inter"] = _mm_nt(u["qt"], u["S"])
        yield
    for part in halves:
        for u in part:
            rows = []
            for g, A in enumerate(u["A"]):
                hi = (g + 1) * GRP * SUB
                mask = causal[hi - GRP * SUB:hi, :hi]
                rows.append(_mm(jnp.where(mask, A, 0.0), u["iv"][:hi]))
            u["o"] = jnp.concatenate(rows, axis=0) + u["inter"]
        yield
    for part in halves:
        for u in part:
            state[u["b"] * NH + u["h"]] = u["S"] * u["PC"] + _mm_tn(u["iv"], u["kd"])
        yield
    for part in halves:
        for u in part:
            o = u["o"]
            g = u["g"]
            ms = jnp.mean(o * o, axis=-1, keepdims=True)
            res = o * lax.rsqrt(ms + NORM_EPS) * ng_ref[:, u["sl"]] * (g * _sigmoid(g))
            o_ref[u["b"], :, u["sl"]] = res.astype(o_ref.dtype)
        yield


def _mixers_kernel(layer, off, pm_ref, pl_ref, ph_ref, w0_ref, w2_ref, a0_ref, a2_ref,
                   kk_ref, ka_ref, rk_ref, lw_ref, lb_ref, ones_ref, lbp_ref, ng_ref,
                   or_ref, oh_ref, r_state, bd_ref, h_state):
    @pl.when(pl.program_id(0) == 0)
    def _():
        for ref in (r_state, bd_ref, h_state):
            ref[...] = jnp.zeros_like(ref)

    rw = _rwkv_stages(pm_ref, pl_ref, w0_ref, w2_ref, a0_ref, a2_ref, kk_ref, ka_ref, rk_ref,
                      lw_ref, lb_ref, ones_ref, or_ref, r_state, bd_ref)
    hg = _hgrn_stages(layer, off, ph_ref, lbp_ref, ng_ref, oh_ref, h_state)
    for stage, _ in enumerate(rw, start=1):
        for _ in range(HGRN_AFTER_RWKV_STAGE.count(stage)):
            next(hg, None)
    for _ in hg:
        pass


def _mixers(p3, main_blk, lora_blk, h_blk_w, h_blk, h_off, layer,
            w0, w2p, a0, a2p, k_k, k_a, r_k, lnx_w, lnx_b, ones_bd, lb_param, hgrn_g):
    B, T, _ = p3.shape
    W = w0.shape[1]
    HW = hgrn_g.shape[1]
    C = CHUNK
    NU = B * (W // QW)
    L = lb_param.shape[0]
    full = lambda shape: pl.BlockSpec(shape, lambda c: (0,) * len(shape))
    return pl.pallas_call(
        functools.partial(_mixers_kernel, layer, h_off),
        grid=(T // C,),
        in_specs=[
            pl.BlockSpec((B, C, 4 * W), lambda c: (0, c, main_blk)),
            pl.BlockSpec((B, C, LORA_BLK), lambda c: (0, c, lora_blk)),
            pl.BlockSpec((B, C, h_blk_w), lambda c: (0, c, h_blk)),
            full((1, W)), full((LANE, W)), full((1, W)), full((LORA_BLK, W)),
            full((1, W)), full((1, W)), full((1, W)), full((1, W)), full((1, W)),
            full((QW, QW)), full((L, HW)), full((1, HW)),
        ],
        out_specs=[pl.BlockSpec((B, C, W), lambda c: (0, c, 0)),
                   pl.BlockSpec((B, C, HW), lambda c: (0, c, 0))],
        out_shape=[jax.ShapeDtypeStruct((B, T, W), BF16),
                   jax.ShapeDtypeStruct((B, T, HW), BF16)],
        scratch_shapes=[
            pltpu.VMEM((NU, QW, QW), F32),
            pltpu.VMEM((NU * len(BD_ROLES), QW, QW), BF16),
            pltpu.VMEM((B * (HW // HGRN_HEAD), HGRN_HEAD, HGRN_HEAD), F32),
        ],
        compiler_params=pltpu.CompilerParams(
            dimension_semantics=("arbitrary",), vmem_limit_bytes=VMEM_LIMIT),
        name="mixers",
    )(p3, p3, p3, w0, w2p, a0, a2p, k_k, k_a, r_k, lnx_w, lnx_b, ones_bd, lb_param, hgrn_g)


def _outproj_kernel(final, yr_ref, yh_ref, x_ref, wr_ref, wh_ref, g_ref, o_ref):
    acc = jnp.dot(yr_ref[...], wr_ref[...], preferred_element_type=F32)
    acc += jnp.dot(yh_ref[...], wh_ref[...], preferred_element_type=F32)
    h = x_ref[...] + acc
    if final:
        ms = jnp.mean(h * h, axis=-1, keepdims=True)
        h = h * lax.rsqrt(ms + NORM_EPS) * g_ref[...]
    o_ref[...] = h


def _outproj(yr, yh, x2, w, g, tm, final):
    m, d = x2.shape
    wr_w = yr.shape[1]
    wh_w = yh.shape[1]
    assert wr_w == wh_w and w.shape == (wr_w + wh_w, d)
    return pl.pallas_call(
        functools.partial(_outproj_kernel, final),
        grid=(m // tm,),
        in_specs=[
            pl.BlockSpec((tm, wr_w), lambda i: (i, 0)),
            pl.BlockSpec((tm, wh_w), lambda i: (i, 0)),
            pl.BlockSpec((tm, d), lambda i: (i, 0)),
            pl.BlockSpec((wr_w, d), lambda i: (0, 0)),
            pl.BlockSpec((wh_w, d), lambda i: (1, 0)),
            pl.BlockSpec((1, d), lambda i: (0, 0)),
        ],
        out_specs=pl.BlockSpec((tm, d), lambda i: (i, 0)),
        out_shape=jax.ShapeDtypeStruct((m, d), F32),
        compiler_params=pltpu.CompilerParams(
            dimension_semantics=("arbitrary",), vmem_limit_bytes=VMEM_LIMIT),
        name="outproj",
    )(yr, yh, x2, w, w, g)


def kernel(x, norm_g, w_in, mu, w0, w2, a0, a2, k_k, k_a, r_k, lnx_w, lnx_b,
           hgrn_norm_g, lb_param, w_out, final_g):
    B, T, D = x.shape
    depth = w_in.shape[0]
    RW = w0.shape[1]
    HW = hgrn_norm_g.shape[1]
    n_r = 4 * RW + 2 * LORA
    tn = 768
    n_p = pl.cdiv(w_in.shape[2], tn) * tn
    h_start = (n_r // LANE) * LANE
    h_off = n_r - h_start
    h_blk_w = n_p - h_start
    assert (4 * RW) % LORA_BLK == 0 and 2 * LORA <= LORA_BLK and w2.shape[1] == LORA <= LANE
    assert h_start % h_blk_w == 0 and h_off + 4 * HW <= h_blk_w

    row = lambda z: z.reshape(1, -1).astype(F32)
    zr = lambda rows: jnp.zeros((rows, RW), F32)
    ones_bd = (jnp.arange(QW)[:, None] // RWKV_HEAD == jnp.arange(QW)[None, :] // RWKV_HEAD).astype(BF16)

    w_in_t = jnp.swapaxes(w_in, 1, 2)
    h = x.reshape(B * T, D)
    for l in range(depth):
        mu_full = jnp.concatenate([mu[l].astype(F32), jnp.zeros((n_p - n_r,), F32)]).reshape(1, -1)
        w2p = jnp.concatenate([w2[l].astype(F32), zr(LANE - LORA)], axis=0).astype(BF16)
        a2p = jnp.concatenate([zr(LORA), a2[l].astype(F32), zr(LORA_BLK - 2 * LORA)],
                              axis=0).astype(BF16)

        p = _inproj(h, row(norm_g[l]), w_in_t, l, mu_full, T, tm=1024, tn=tn, groups=4)
        p3 = p.reshape(B, T, n_p)
        y_r, y_h = _mixers(p3, 0, (4 * RW) // LORA_BLK, h_blk_w, h_start // h_blk_w, h_off, l,
                           row(w0[l]), w2p, row(a0[l]), a2p, row(k_k[l]), row(k_a[l]),
                           row(r_k[l]), row(lnx_w[l]), row(lnx_b[l]), ones_bd,
                           lb_param.astype(F32), row(hgrn_norm_g[l]))
        wo = w_out[l].astype(BF16)
        h = _outproj(y_r.reshape(B * T, RW), y_h.reshape(B * T, HW), h,
                     wo, row(final_g), tm=512, final=(l == depth - 1))
    return h.reshape(B, T, D)
```

```python
import functools
import math

import jax
import jax.numpy as jnp
from jax import lax
from jax.experimental import pallas as pl
from jax.experimental.pallas import tpu as pltpu

F32 = jnp.float32
BF16 = jnp.bfloat16

NORM_EPS = 1e-6
LNX_EPS = 64e-5
RWKV_HEAD = 64
HGRN_HEAD = 128
LORA = 96
LORA_BLK = 256
LANE = 128
CHUNK = 64
HGRN_SUB = 16
QUAD = 4
QW = QUAD * RWKV_HEAD
BD_ROLES = ("B", "K", "V", "N", "Z", "U")
HGRN_AFTER_RWKV_STAGE = tuple(range(6, 18))
EXP_M05 = math.exp(-0.5)
VMEM_LIMIT = 48 * 1024 * 1024
VMEM_MARGIN = 6 * 1024 * 1024


def _mm(a, b):
    return jnp.dot(a.astype(BF16), b.astype(BF16), preferred_element_type=F32)


def _mm_nt(a, b):
    return lax.dot_general(a.astype(BF16), b.astype(BF16), (((1,), (1,)), ((), ())),
                           preferred_element_type=F32)


def _mm_tn(a, b):
    return lax.dot_general(a.astype(BF16), b.astype(BF16), (((0,), (0,)), ((), ())),
                           preferred_element_type=F32)


def _cumsum_rows(x, tril_bf16):
    hi = x.astype(BF16)
    lo = (x - hi.astype(F32)).astype(BF16)
    dot = functools.partial(jnp.dot, preferred_element_type=F32)
    return dot(tril_bf16, hi) + dot(tril_bf16, lo)


def _sigmoid(x):
    return 0.5 + 0.5 * jnp.tanh(0.5 * x)


def _inproj_kernel(gap_tile, gap_at, gap, seq_len, x_ref, g_ref, w_ref, mu_ref, o_ref, hn_ref, carry_ref):
    q = pl.program_id(0)
    j = pl.program_id(1)
    i = pl.program_id(2)
    tm, tn = o_ref.shape

    @pl.when((q == 0) & (j == 0) & (i == 0))
    def _():
        carry_ref[...] = jnp.zeros_like(carry_ref)

    @pl.when(j == 0)
    def _():
        x = x_ref[...]
        ms = jnp.mean(x * x, axis=-1, keepdims=True)
        hn_ref[i] = (x * lax.rsqrt(ms + NORM_EPS) * g_ref[...]).astype(BF16)

    w = w_ref[0]
    if gap:
        w_gap = jnp.concatenate([w[:gap_at], jnp.zeros((gap, w.shape[1]), w.dtype),
                                 w[gap_at:tn - gap]], axis=0)
        w = jnp.where(j == gap_tile, w_gap, w)
    acc = _mm_nt(hn_ref[i], w)

    first_row = (q * pl.num_programs(2) + i) * tm
    prev_last = jnp.where(first_row % seq_len == 0, 0.0, carry_ref[j])
    carry_ref[j] = acc[tm - 1:tm, :]
    is_row0 = lax.broadcasted_iota(jnp.int32, (tm, 1), 0) == 0
    prev = jnp.where(is_row0, prev_last, pltpu.roll(acc, 1, axis=0))
    o_ref[...] = acc + mu_ref[...] * (prev - acc)


def _inproj(x2, g, wt_all, layer, mu_full, gap_col, seq_len, tm, tn, groups):
    m, d = x2.shape
    n = mu_full.shape[1]
    gap = n - wt_all.shape[1]
    gap_tile, gap_at = divmod(gap_col, tn)
    assert n % tn == 0 and 0 <= gap and gap_at + gap <= tn and gap % 8 == 0 and gap_at % 8 == 0
    w_map = lambda q, j, i: (
        layer, pl.multiple_of(jnp.where(j <= gap_tile, j * tn, j * tn - gap), 8), 0)
    ni = m // (groups * tm)
    assert seq_len % tm == 0 and mu_full.shape == (1, n)
    vmem = ni * tm * d * 2 + 2 * 4 * (tm * d + tn * d + tm * tn) + VMEM_MARGIN
    x_map = lambda q, j, i: (q * ni + jnp.where(j == 0, i, ni - 1), 0)
    return pl.pallas_call(
        functools.partial(_inproj_kernel, gap_tile, gap_at, gap, seq_len),
        grid=(groups, n // tn, ni),
        in_specs=[
            pl.BlockSpec((tm, d), x_map),
            pl.BlockSpec((1, d), lambda q, j, i: (0, 0)),
            pl.BlockSpec((pl.Element(1), pl.Element(tn), pl.Element(d)), w_map),
            pl.BlockSpec((1, tn), lambda q, j, i: (0, j)),
        ],
        out_specs=pl.BlockSpec((tm, tn), lambda q, j, i: (q * ni + i, j)),
        out_shape=jax.ShapeDtypeStruct((m, n), F32),
        scratch_shapes=[pltpu.VMEM((ni, tm, d), BF16), pltpu.VMEM((n // tn, 1, tn), F32)],
        compiler_params=pltpu.CompilerParams(
            dimension_semantics=("arbitrary", "arbitrary", "arbitrary"),
            vmem_limit_bytes=vmem),
        name="inproj",
    )(x2, g, wt_all, mu_full)


def _rwkv_stages(pm_ref, pl_ref, w0_ref, w2_ref, a0_ref, a2_ref,
                 kk_ref, ka_ref, rk_ref, lw_ref, lb_ref, ones_ref,
                 o_ref, state, bd_ref):
    C = CHUNK
    NB = pm_ref.shape[0]
    W = w0_ref.shape[1]
    NQ = W // QW

    t_i = lax.broadcasted_iota(jnp.int32, (C, C), 0)
    s_i = lax.broadcasted_iota(jnp.int32, (C, C), 1)
    tril = (t_i >= s_i).astype(BF16)
    ones_bd = ones_ref[...]
    dot = functools.partial(jnp.dot, preferred_element_type=F32)

    lane = lax.broadcasted_iota(jnp.int32, (C, QW), 1)
    lane_s = lane % RWKV_HEAD
    trow = lax.broadcasted_iota(jnp.int32, (C, QW), 0)
    strict = lane_s < trow
    incl = lane_s <= trow
    eye_q = jnp.where(lane_s == trow, 1.0, 0.0).astype(F32)
    bd_mask = (lax.broadcasted_iota(jnp.int32, (QW, QW), 0) // RWKV_HEAD
               == lax.broadcasted_iota(jnp.int32, (QW, QW), 1) // RWKV_HEAD)

    def bd(u, role, x):
        slot = u["ui"] * len(BD_ROLES) + BD_ROLES.index(role)
        xb = x.astype(BF16)
        for h in range(QUAD):
            hs = slice(h * RWKV_HEAD, (h + 1) * RWKV_HEAD)
            bd_ref[slot, h * C:(h + 1) * C, hs] = xb[:, hs]
        return bd_ref[slot]

    def segsum_all(xs):
        out = dot(jnp.concatenate([x.astype(BF16) for x in xs], axis=0), ones_bd)
        return [out[i * C:(i + 1) * C] for i in range(len(xs))]

    xl = jnp.concatenate([pl_ref[b] for b in range(NB)], axis=0)
    z = w0_ref[...] + _mm(jnp.tanh(xl[:, :LANE]), w2_ref[...])
    ld_all = -EXP_M05 * _sigmoid(z)
    a_all = _sigmoid(a0_ref[...] + _mm(xl, a2_ref[...]))
    units = []
    for b in range(NB):
        r = pm_ref[b, :, 0 * W:1 * W]
        k = pm_ref[b, :, 1 * W:2 * W]
        v = pm_ref[b, :, 2 * W:3 * W]
        gate = pm_ref[b, :, 3 * W:4 * W]
        ld = ld_all[b * C:(b + 1) * C]
        a = a_all[b * C:(b + 1) * C]
        G = _cumsum_rows(ld, tril)
        Gx = G - ld
        GC = G[C - 1:C, :]
        for q in range(NQ):
            sl = slice(q * QW, (q + 1) * QW)
            units.append(dict(b=b, ui=b * NQ + q, sl=sl, r=r[:, sl], k=k[:, sl], v=v[:, sl],
                              g=gate[:, sl], a=a[:, sl], G=G[:, sl], Gx=Gx[:, sl], GC=GC[:, sl]))
    yield

    for u in units:
        u["kk"] = u["k"] * kk_ref[:, u["sl"]]
    for u, n2 in zip(units, segsum_all([u["kk"] * u["kk"] for u in units])):
        u["n2"] = n2
    yield

    def unit_chain(u):
        sl = u["sl"]
        kkn = u["kk"] * lax.rsqrt(jnp.maximum(u["n2"], 1e-24))
        k2 = u["k"] * (1.0 + (u["a"] - 1.0) * ka_ref[:, sl])
        av = -kkn
        bv = kkn * u["a"]
        einv = jnp.exp(-u["G"])
        eC = jnp.exp(u["GC"]) * einv
        u["rkk"] = u["r"] * k2 * rk_ref[:, sl]
        BDB = bd(u, "B", bv * einv)
        BDK = bd(u, "K", k2 * einv)
        BKe = jnp.concatenate([bv * eC, k2 * eC], axis=0).astype(BF16)
        BDV = bd(u, "V", u["v"])
        lhs = jnp.concatenate([av * jnp.exp(u["Gx"]), u["r"] * jnp.exp(u["G"])],
                              axis=0).astype(BF16)
        pc_col = jnp.transpose(jnp.broadcast_to(jnp.exp(u["GC"]), (LANE, QW)))
        PCc = jnp.concatenate([pc_col] * (QW // LANE), axis=1)
        yield
        SB = _mm_nt(lhs, BDB)
        yield
        SK = _mm_nt(lhs, BDK)
        yield
        N = jnp.where(strict, SB[:C], 0.0)
        Lrb = jnp.where(incl, SB[C:], 0.0)
        P = eye_q + N
        Np = _mm(N, bd(u, "N", N))
        yield
        akl = jnp.concatenate([jnp.where(strict, SK[:C], 0.0),
                               jnp.where(incl, SK[C:], 0.0)], axis=0)
        both = _mm(akl, BDV)
        yield
        for _ in range(4):
            out = _mm(jnp.concatenate([P, Np], axis=0), bd(u, "N", Np))
            P = P + out[:C]
            Np = out[C:]
            yield
        P = P + _mm(P, bd(u, "N", Np))
        yield
        H = state[u["ui"]]
        zy = _mm(lhs, H)
        yield
        U = _mm(P, bd(u, "Z", zy[:C] + both[:C]))
        yield
        u["Y"] = zy[C:] + _mm(Lrb, bd(u, "U", U)) + both[C:]
        yield
        upd = _mm_tn(BKe, jnp.concatenate([U, u["v"]], axis=0))
        state[u["ui"]] = H * PCc + jnp.where(bd_mask, upd, 0.0)

    chains = [unit_chain(u) for u in units]
    live = list(range(len(chains)))
    while live:
        for i in list(live):
            if next(chains[i], "done") == "done":
                live.remove(i)
        yield

    for u, s in zip(units, segsum_all([u["Y"] for u in units])):
        u["yc"] = u["Y"] - s * (1.0 / RWKV_HEAD)
    yield
    for u, s in zip(units, segsum_all([u["yc"] * u["yc"] for u in units])):
        u["var"] = s * (1.0 / RWKV_HEAD)
    yield
    for u, s in zip(units, segsum_all([u["rkk"] for u in units])):
        u["bonus"] = s
    yield
    for u in units:
        sl = u["sl"]
        yn = u["yc"] * lax.rsqrt(u["var"] + LNX_EPS) * lw_ref[:, sl] + lb_ref[:, sl]
        res = (yn + u["bonus"] * u["v"]) * (u["g"] * _sigmoid(u["g"]))
        o_ref[u["b"], :, sl] = res.astype(o_ref.dtype)


def _hgrn_stages(layer, off, ph_ref, lbp_ref, ng_ref, o_ref, state):
    C = CHUNK
    NB = ph_ref.shape[0]
    W = ng_ref.shape[1]
    D = HGRN_HEAD
    NH = W // D
    SUB = HGRN_SUB
    NBLK = C // SUB
    GRP = 2 * LANE // D

    lp = lbp_ref[...]
    e = jnp.exp(lp - jnp.max(lp, axis=0, keepdims=True))
    lb = jnp.sum(e[:layer + 1], axis=0, keepdims=True) / jnp.sum(e, axis=0, keepdims=True)

    t_i = lax.broadcasted_iota(jnp.int32, (C, C), 0)
    s_i = lax.broadcasted_iota(jnp.int32, (C, C), 1)
    causal = t_i >= s_i
    tril = causal.astype(BF16)

    units = []
    for b in range(NB):
        x = ph_ref[b]
        qv = x[:, off + 0 * W:off + 1 * W]
        fr = x[:, off + 1 * W:off + 2 * W]
        iv = x[:, off + 2 * W:off + 3 * W]
        gate = x[:, off + 3 * W:off + 4 * W]
        f = lb + (1.0 - lb) * _sigmoid(fr)
        kx = 1.0 - f
        G = _cumsum_rows(jnp.log(f), tril)
        GC = G[C - 1:C, :]
        PC = jnp.exp(GC)
        ivb = iv.astype(BF16)
        zero = jnp.zeros((SUB, W), BF16)
        qs, ks, qt_rows, k_rows = [], [], [], []
        for j in range(NBLK):
            lo, hi = j * SUB, (j + 1) * SUB
            qj = qv[lo:hi] * jnp.exp(G[lo:hi] - G[lo - 1:lo, :]) if j else qv[lo:hi] * jnp.exp(G[lo:hi])
            kj = kx[lo:hi] * jnp.exp(G[lo - 1:lo, :] - G[lo:hi]) if j else kx[lo:hi] * jnp.exp(-G[lo:hi])
            if j:
                step = jnp.exp(G[lo - 1:lo, :] - (G[lo - SUB - 1:lo - SUB, :] if j > 1 else 0.0))
                k_rows = [kr * step for kr in k_rows]
                qt_rows.append(qj * jnp.exp(G[lo - 1:lo, :]))
            else:
                qt_rows.append(qj)
            k_rows.append(kj)
            g0 = (j // GRP) * GRP
            qs.append(jnp.concatenate([zero] * (j - g0) + [qj.astype(BF16)] + [zero] * (g0 + GRP - 1 - j),
                                      axis=0))
            ks.append(jnp.concatenate([kr.astype(BF16) for kr in k_rows] + [zero] * (g0 + GRP - 1 - j),
                                      axis=0))
        qt = jnp.concatenate(qt_rows, axis=0).astype(BF16)
        to_end = jnp.exp(GC - G[C - SUB - 1:C - SUB, :])
        kd = jnp.concatenate([(kr * to_end).astype(BF16) for kr in k_rows], axis=0)
        for h in range(NH):
            sl = slice(h * D, (h + 1) * D)
            units.append(dict(b=b, h=h, sl=sl, qt=qt[:, sl], kd=kd[:, sl], iv=ivb[:, sl],
                              PC=PC[:, sl], g=gate[:, sl],
                              qcat=[jnp.concatenate([q[:, sl] for q in qs[g:g + GRP]], axis=1)
                                    for g in range(0, NBLK, GRP)],
                              kcat=[jnp.concatenate([k[:, sl] for k in ks[g:g + GRP]], axis=1)
                                    for g in range(0, NBLK, GRP)]))
        yield

    halves = (units[:len(units) // 2], units[len(units) // 2:])
    for part in halves:
        for u in part:
            u["A"] = [_mm_nt(q, k) for q, k in zip(u["qcat"], u["kcat"])]
        yield
    for part in halves:
        for u in part:
            u["S"] = state[u["b"] * NH + u["h"]]
            u["inter"] = _mm_nt(u["qt"], u["S"])
        yield
    for part in halves:
        for u in part:
            rows = []
            for g, A in enumerate(u["A"]):
                hi = (g + 1) * GRP * SUB
                mask = causal[hi - GRP * SUB:hi, :hi]
                rows.append(_mm(jnp.where(mask, A, 0.0), u["iv"][:hi]))
            u["o"] = jnp.concatenate(rows, axis=0) + u["inter"]
        yield
    for part in halves:
        for u in part:
            state[u["b"] * NH + u["h"]] = u["S"] * u["PC"] + _mm_tn(u["iv"], u["kd"])
        yield
    for part in halves:
        for u in part:
            o = u["o"]
            g = u["g"]
            ms = jnp.mean(o * o, axis=-1, keepdims=True)
            res = o * lax.rsqrt(ms + NORM_EPS) * ng_ref[:, u["sl"]] * (g * _sigmoid(g))
            o_ref[u["b"], :, u["sl"]] = res.astype(o_ref.dtype)
        yield


def _mixers_kernel(layer, off, pm_ref, pl_ref, ph_ref, w0_ref, w2_ref, a0_ref, a2_ref,
                   kk_ref, ka_ref, rk_ref, lw_ref, lb_ref, ones_ref, lbp_ref, ng_ref,
                   or_ref, oh_ref, r_state, bd_ref, h_state):
    @pl.when(pl.program_id(0) == 0)
    def _():
        for ref in (r_state, bd_ref, h_state):
            ref[...] = jnp.zeros_like(ref)

    rw = _rwkv_stages(pm_ref, pl_ref, w0_ref, w2_ref, a0_ref, a2_ref, kk_ref, ka_ref, rk_ref,
                      lw_ref, lb_ref, ones_ref, or_ref, r_state, bd_ref)
    hg = _hgrn_stages(layer, off, ph_ref, lbp_ref, ng_ref, oh_ref, h_state)
    for stage, _ in enumerate(rw, start=1):
        for _ in range(HGRN_AFTER_RWKV_STAGE.count(stage)):
            next(hg, None)
    for _ in hg:
        pass


def _mixers(p3, main_blk, lora_blk, h_blk_w, h_blk, h_off, layer,
            w0, w2p, a0, a2p, k_k, k_a, r_k, lnx_w, lnx_b, ones_bd, lb_param, hgrn_g):
    B, T, _ = p3.shape
    W = w0.shape[1]
    HW = hgrn_g.shape[1]
    C = CHUNK
    NU = B * (W // QW)
    L = lb_param.shape[0]
    full = lambda shape: pl.BlockSpec(shape, lambda c: (0,) * len(shape))
    return pl.pallas_call(
        functools.partial(_mixers_kernel, layer, h_off),
        grid=(T // C,),
        in_specs=[
            pl.BlockSpec((B, C, 4 * W), lambda c: (0, c, main_blk)),
            pl.BlockSpec((B, C, LORA_BLK), lambda c: (0, c, lora_blk)),
            pl.BlockSpec((B, C, h_blk_w), lambda c: (0, c, h_blk)),
            full((1, W)), full((LANE, W)), full((1, W)), full((LORA_BLK, W)),
            full((1, W)), full((1, W)), full((1, W)), full((1, W)), full((1, W)),
            full((QW, QW)), full((L, HW)), full((1, HW)),
        ],
        out_specs=[pl.BlockSpec((B, C, W), lambda c: (0, c, 0)),
                   pl.BlockSpec((B, C, HW), lambda c: (0, c, 0))],
        out_shape=[jax.ShapeDtypeStruct((B, T, W), BF16),
                   jax.ShapeDtypeStruct((B, T, HW), BF16)],
        scratch_shapes=[
            pltpu.VMEM((NU, QW, QW), F32),
            pltpu.VMEM((NU * len(BD_ROLES), QW, QW), BF16),
            pltpu.VMEM((B * (HW // HGRN_HEAD), HGRN_HEAD, HGRN_HEAD), F32),
        ],
        compiler_params=pltpu.CompilerParams(
            dimension_semantics=("arbitrary",), vmem_limit_bytes=VMEM_LIMIT),
        name="mixers",
    )(p3, p3, p3, w0, w2p, a0, a2p, k_k, k_a, r_k, lnx_w, lnx_b, ones_bd, lb_param, hgrn_g)


def _outproj_kernel(final, yr_ref, yh_ref, x_ref, wr_ref, wh_ref, g_ref, o_ref):
    acc = jnp.dot(yr_ref[...], wr_ref[...], preferred_element_type=F32)
    acc += jnp.dot(yh_ref[...], wh_ref[...], preferred_element_type=F32)
    h = x_ref[...] + acc
    if final:
        ms = jnp.mean(h * h, axis=-1, keepdims=True)
        h = h * lax.rsqrt(ms + NORM_EPS) * g_ref[...]
    o_ref[...] = h


def _outproj(yr, yh, x2, w, g, tm, final):
    m, d = x2.shape
    wr_w = yr.shape[1]
    wh_w = yh.shape[1]
    assert wr_w == wh_w and w.shape == (wr_w + wh_w, d)
    return pl.pallas_call(
        functools.partial(_outproj_kernel, final),
        grid=(m // tm,),
        in_specs=[
            pl.BlockSpec((tm, wr_w), lambda i: (i, 0)),
            pl.BlockSpec((tm, wh_w), lambda i: (i, 0)),
            pl.BlockSpec((tm, d), lambda i: (i, 0)),
            pl.BlockSpec((wr_w, d), lambda i: (0, 0)),
            pl.BlockSpec((wh_w, d), lambda i: (1, 0)),
            pl.BlockSpec((1, d), lambda i: (0, 0)),
        ],
        out_specs=pl.BlockSpec((tm, d), lambda i: (i, 0)),
        out_shape=jax.ShapeDtypeStruct((m, d), F32),
        compiler_params=pltpu.CompilerParams(
            dimension_semantics=("arbitrary",), vmem_limit_bytes=VMEM_LIMIT),
        name="outproj",
    )(yr, yh, x2, w, w, g)


def kernel(x, norm_g, w_in, mu, w0, w2, a0, a2, k_k, k_a, r_k, lnx_w, lnx_b,
           hgrn_norm_g, lb_param, w_out, final_g):
    B, T, D = x.shape
    depth = w_in.shape[0]
    RW = w0.shape[1]
    HW = hgrn_norm_g.shape[1]
    n_r = 4 * RW + 2 * LORA
    tn = 768
    gap = -n_r % LANE
    n_p = w_in.shape[2] + gap
    h_start = (n_r // LANE) * LANE
    h_off = n_r + gap - h_start
    h_blk_w = n_p - h_start
    assert (4 * RW) % LORA_BLK == 0 and 2 * LORA <= LORA_BLK and w2.shape[1] == LORA <= LANE
    assert h_start % h_blk_w == 0 and h_off + 4 * HW <= h_blk_w

    row = lambda z: z.reshape(1, -1).astype(F32)
    zr = lambda rows: jnp.zeros((rows, RW), F32)
    ones_bd = (jnp.arange(QW)[:, None] // RWKV_HEAD == jnp.arange(QW)[None, :] // RWKV_HEAD).astype(BF16)

    w_in_t = jnp.swapaxes(w_in, 1, 2)
    h = x.reshape(B * T, D)
    for l in range(depth):
        mu_full = jnp.concatenate([mu[l].astype(F32), jnp.zeros((n_p - n_r,), F32)]).reshape(1, -1)
        w2p = jnp.concatenate([w2[l].astype(F32), zr(LANE - LORA)], axis=0).astype(BF16)
        a2p = jnp.concatenate([zr(LORA), a2[l].astype(F32), zr(LORA_BLK - 2 * LORA)],
                              axis=0).astype(BF16)

        p = _inproj(h, row(norm_g[l]), w_in_t, l, mu_full, n_r, T, tm=1024, tn=tn, groups=4)
        p3 = p.reshape(B, T, n_p)
        y_r, y_h = _mixers(p3, 0, (4 * RW) // LORA_BLK, h_blk_w, h_start // h_blk_w, h_off, l,
                           row(w0[l]), w2p, row(a0[l]), a2p, row(k_k[l]), row(k_a[l]),
                           row(r_k[l]), row(lnx_w[l]), row(lnx_b[l]), ones_bd,
                           lb_param.astype(F32), row(hgrn_norm_g[l]))
        wo = w_out[l].astype(BF16)
        h = _outproj(y_r.reshape(B * T, RW), y_h.reshape(B * T, HW), h,
                     wo, row(final_g), tm=512, final=(l == depth - 1))
    return h.reshape(B, T, D)
```

```python
import functools
import math

import jax
import jax.numpy as jnp
from jax import lax
from jax.experimental import pallas as pl
from jax.experimental.pallas import tpu as pltpu

F32 = jnp.float32
BF16 = jnp.bfloat16

NORM_EPS = 1e-6
LNX_EPS = 64e-5
RWKV_HEAD = 64
HGRN_HEAD = 128
LORA = 96
LORA_BLK = 256
LANE = 128
CHUNK = 64
HGRN_SUB = 16
QUAD = 4
QW = QUAD * RWKV_HEAD
BD_ROLES = ("B", "K", "V", "N", "Z", "U")
HGRN_AFTER_RWKV_STAGE = tuple(range(6, 18))
EXP_M05 = math.exp(-0.5)
VMEM_LIMIT = 48 * 1024 * 1024
VMEM_MARGIN = 6 * 1024 * 1024


def _mm(a, b):
    return jnp.dot(a.astype(BF16), b.astype(BF16), preferred_element_type=F32)


def _mm_nt(a, b):
    return lax.dot_general(a.astype(BF16), b.astype(BF16), (((1,), (1,)), ((), ())),
                           preferred_element_type=F32)


def _mm_tn(a, b):
    return lax.dot_general(a.astype(BF16), b.astype(BF16), (((0,), (0,)), ((), ())),
                           preferred_element_type=F32)


def _cumsum_rows(x, tril_bf16):
    hi = x.astype(BF16)
    lo = (x - hi.astype(F32)).astype(BF16)
    dot = functools.partial(jnp.dot, preferred_element_type=F32)
    return dot(tril_bf16, hi) + dot(tril_bf16, lo)


def _sigmoid(x):
    return 0.5 + 0.5 * jnp.tanh(0.5 * x)


def _inproj_kernel(gap_tile, gap_at, gap, seq_len, x_ref, g_ref, w_ref, mu_ref, o_ref, hn_ref, carry_ref):
    q = pl.program_id(0)
    j = pl.program_id(1)
    i = pl.program_id(2)
    tm, tn = o_ref.shape

    @pl.when((q == 0) & (j == 0) & (i == 0))
    def _():
        carry_ref[...] = jnp.zeros_like(carry_ref)

    @pl.when(j == 0)
    def _():
        x = x_ref[...]
        ms = jnp.mean(x * x, axis=-1, keepdims=True)
        hn_ref[i] = (x * lax.rsqrt(ms + NORM_EPS) * g_ref[...]).astype(BF16)

    w = w_ref[0]
    if gap:
        w_gap = jnp.concatenate([w[:gap_at], jnp.zeros((gap, w.shape[1]), w.dtype),
                                 w[gap_at:tn - gap]], axis=0)
        w = jnp.where(j == gap_tile, w_gap, w)
    acc = _mm_nt(hn_ref[i], w)

    first_row = (q * pl.num_programs(2) + i) * tm
    prev_last = jnp.where(first_row % seq_len == 0, 0.0, carry_ref[j])
    carry_ref[j] = acc[tm - 1:tm, :]
    is_row0 = lax.broadcasted_iota(jnp.int32, (tm, 1), 0) == 0
    prev = jnp.where(is_row0, prev_last, pltpu.roll(acc, 1, axis=0))
    o_ref[...] = acc + mu_ref[...] * (prev - acc)


def _inproj(x2, g, wt_all, layer, mu_full, gap_col, seq_len, tm, tn, groups):
    m, d = x2.shape
    n = mu_full.shape[1]
    gap = n - wt_all.shape[1]
    gap_tile, gap_at = divmod(gap_col, tn)
    assert n % tn == 0 and 0 <= gap and gap_at + gap <= tn and gap % 8 == 0 and gap_at % 8 == 0
    w_map = lambda q, j, i: (
        layer, pl.multiple_of(jnp.where(j <= gap_tile, j * tn, j * tn - gap), 8), 0)
    ni = m // (groups * tm)
    assert seq_len % tm == 0 and mu_full.shape == (1, n)
    vmem = ni * tm * d * 2 + 2 * 4 * (tm * d + tn * d + tm * tn) + VMEM_MARGIN
    x_map = lambda q, j, i: (q * ni + jnp.where(j == 0, i, ni - 1), 0)
    return pl.pallas_call(
        functools.partial(_inproj_kernel, gap_tile, gap_at, gap, seq_len),
        grid=(groups, n // tn, ni),
        in_specs=[
            pl.BlockSpec((tm, d), x_map),
            pl.BlockSpec((1, d), lambda q, j, i: (0, 0)),
            pl.BlockSpec((pl.Element(1), pl.Element(tn), pl.Element(d)), w_map),
            pl.BlockSpec((1, tn), lambda q, j, i: (0, j)),
        ],
        out_specs=pl.BlockSpec((tm, tn), lambda q, j, i: (q * ni + i, j)),
        out_shape=jax.ShapeDtypeStruct((m, n), F32),
        scratch_shapes=[pltpu.VMEM((ni, tm, d), BF16), pltpu.VMEM((n // tn, 1, tn), F32)],
        compiler_params=pltpu.CompilerParams(
            dimension_semantics=("arbitrary", "arbitrary", "arbitrary"),
            vmem_limit_bytes=vmem),
        name="inproj",
    )(x2, g, wt_all, mu_full)


def _rwkv_stages(pm_ref, pl_ref, w0_ref, w2_ref, a0_ref, a2_ref,
                 kk_ref, ka_ref, rk_ref, lw_ref, lb_ref, ones_ref,
                 o_ref, state, bd_ref):
    C = CHUNK
    NB = pm_ref.shape[0]
    W = w0_ref.shape[1]
    NQ = W // QW

    t_i = lax.broadcasted_iota(jnp.int32, (C, C), 0)
    s_i = lax.broadcasted_iota(jnp.int32, (C, C), 1)
    tril = (t_i >= s_i).astype(BF16)
    ones_bd = ones_ref[...]
    dot = functools.partial(jnp.dot, preferred_element_type=F32)

    lane = lax.broadcasted_iota(jnp.int32, (C, QW), 1)
    lane_s = lane % RWKV_HEAD
    trow = lax.broadcasted_iota(jnp.int32, (C, QW), 0)
    strict = lane_s < trow
    incl = lane_s <= trow
    eye_q = jnp.where(lane_s == trow, 1.0, 0.0).astype(F32)
    bd_mask = (lax.broadcasted_iota(jnp.int32, (QW, QW), 0) // RWKV_HEAD
               == lax.broadcasted_iota(jnp.int32, (QW, QW), 1) // RWKV_HEAD)

    def bd(u, role, x):
        slot = u["ui"] * len(BD_ROLES) + BD_ROLES.index(role)
        xb = x.astype(BF16)
        for h in range(QUAD):
            hs = slice(h * RWKV_HEAD, (h + 1) * RWKV_HEAD)
            bd_ref[slot, h * C:(h + 1) * C, hs] = xb[:, hs]
        return bd_ref[slot]

    def segsum_all(xs):
        out = dot(jnp.concatenate([x.astype(BF16) for x in xs], axis=0), ones_bd)
        return [out[i * C:(i + 1) * C] for i in range(len(xs))]

    xl = jnp.concatenate([pl_ref[b] for b in range(NB)], axis=0)
    z = w0_ref[...] + _mm(jnp.tanh(xl[:, :LANE]), w2_ref[...])
    ld_all = -EXP_M05 * _sigmoid(z)
    a_all = _sigmoid(a0_ref[...] + _mm(xl, a2_ref[...]))
    units = []
    for b in range(NB):
        r = pm_ref[b, :, 0 * W:1 * W]
        k = pm_ref[b, :, 1 * W:2 * W]
        v = pm_ref[b, :, 2 * W:3 * W]
        gate = pm_ref[b, :, 3 * W:4 * W]
        ld = ld_all[b * C:(b + 1) * C]
        a = a_all[b * C:(b + 1) * C]
        G = _cumsum_rows(ld, tril)
        Gx = G - ld
        GC = G[C - 1:C, :]
        for q in range(NQ):
            sl = slice(q * QW, (q + 1) * QW)
            units.append(dict(b=b, ui=b * NQ + q, sl=sl, r=r[:, sl], k=k[:, sl], v=v[:, sl],
                              g=gate[:, sl], a=a[:, sl], G=G[:, sl], Gx=Gx[:, sl], GC=GC[:, sl]))
    yield

    for u in units:
        u["kk"] = u["k"] * kk_ref[:, u["sl"]]
    for u, n2 in zip(units, segsum_all([u["kk"] * u["kk"] for u in units])):
        u["n2"] = n2
    yield

    def unit_chain(u):
        sl = u["sl"]
        kkn = u["kk"] * lax.rsqrt(jnp.maximum(u["n2"], 1e-24))
        k2 = u["k"] * (1.0 + (u["a"] - 1.0) * ka_ref[:, sl])
        av = -kkn
        bv = kkn * u["a"]
        einv = jnp.exp(-u["G"])
        eC = jnp.exp(u["GC"]) * einv
        u["rkk"] = u["r"] * k2 * rk_ref[:, sl]
        BDB = bd(u, "B", bv * einv)
        BDK = bd(u, "K", k2 * einv)
        BKe = jnp.concatenate([bv * eC, k2 * eC], axis=0).astype(BF16)
        BDV = bd(u, "V", u["v"])
        lhs = jnp.concatenate([av * jnp.exp(u["Gx"]), u["r"] * jnp.exp(u["G"])],
                              axis=0).astype(BF16)
        pc_col = jnp.transpose(jnp.broadcast_to(jnp.exp(u["GC"]), (LANE, QW)))
        PCc = jnp.concatenate([pc_col] * (QW // LANE), axis=1)
        yield
        SB = _mm_nt(lhs, BDB)
        yield
        SK = _mm_nt(lhs, BDK)
        yield
        N = jnp.where(strict, SB[:C], 0.0)
        Lrb = jnp.where(incl, SB[C:], 0.0)
        P = eye_q + N
        Np = _mm(N, bd(u, "N", N))
        yield
        akl = jnp.concatenate([jnp.where(strict, SK[:C], 0.0),
                               jnp.where(incl, SK[C:], 0.0)], axis=0)
        both = _mm(akl, BDV)
        yield
        for _ in range(4):
            out = _mm(jnp.concatenate([P, Np], axis=0), bd(u, "N", Np))
            P = P + out[:C]
            Np = out[C:]
            yield
        P = P + _mm(P, bd(u, "N", Np))
        yield
        H = state[u["ui"]]
        zy = _mm(lhs, H)
        yield
        U = _mm(P, bd(u, "Z", zy[:C] + both[:C]))
        yield
        u["Y"] = zy[C:] + _mm(Lrb, bd(u, "U", U)) + both[C:]
        yield
        upd = _mm_tn(BKe, jnp.concatenate([U, u["v"]], axis=0))
        state[u["ui"]] = H * PCc + jnp.where(bd_mask, upd, 0.0)

    chains = [unit_chain(u) for u in units]
    live = list(range(len(chains)))
    while live:
        for i in list(live):
            if next(chains[i], "done") == "done":
                live.remove(i)
        yield

    for u, s in zip(units, segsum_all([u["Y"] for u in units])):
        u["yc"] = u["Y"] - s * (1.0 / RWKV_HEAD)
    yield
    for u, s in zip(units, segsum_all([u["yc"] * u["yc"] for u in units])):
        u["var"] = s * (1.0 / RWKV_HEAD)
    yield
    for u, s in zip(units, segsum_all([u["rkk"] for u in units])):
        u["bonus"] = s
    yield
    for u in units:
        sl = u["sl"]
        yn = u["yc"] * lax.rsqrt(u["var"] + LNX_EPS) * lw_ref[:, sl] + lb_ref[:, sl]
        res = (yn + u["bonus"] * u["v"]) * (u["g"] * _sigmoid(u["g"]))
        o_ref[u["b"], :, sl] = res.astype(o_ref.dtype)


def _hgrn_stages(layer, off, ph_ref, lbp_ref, ng_ref, o_ref, state):
    C = CHUNK
    NB = ph_ref.shape[0]
    W = ng_ref.shape[1]
    D = HGRN_HEAD
    NH = W // D
    SUB = HGRN_SUB
    NBLK = C // SUB
    GRP = 2 * LANE // D

    lp = lbp_ref[...]
    e = jnp.exp(lp - jnp.max(lp, axis=0, keepdims=True))
    lb = jnp.sum(e[:layer + 1], axis=0, keepdims=True) / jnp.sum(e, axis=0, keepdims=True)

    t_i = lax.broadcasted_iota(jnp.int32, (C, C), 0)
    s_i = lax.broadcasted_iota(jnp.int32, (C, C), 1)
    causal = t_i >= s_i
    tril = causal.astype(BF16)

    units = []
    for b in range(NB):
        x = ph_ref[b]
        qv = x[:, off + 0 * W:off + 1 * W]
        fr = x[:, off + 1 * W:off + 2 * W]
        iv = x[:, off + 2 * W:off + 3 * W]
        gate = x[:, off + 3 * W:off + 4 * W]
        f = lb + (1.0 - lb) * _sigmoid(fr)
        kx = 1.0 - f
        G = _cumsum_rows(jnp.log(f), tril)
        GC = G[C - 1:C, :]
        PC = jnp.exp(GC)
        ivb = iv.astype(BF16)
        zero = jnp.zeros((SUB, W), BF16)
        qs, ks, qt_rows, k_rows = [], [], [], []
        for j in range(NBLK):
            lo, hi = j * SUB, (j + 1) * SUB
            qj = qv[lo:hi] * jnp.exp(G[lo:hi] - G[lo - 1:lo, :]) if j else qv[lo:hi] * jnp.exp(G[lo:hi])
            kj = kx[lo:hi] * jnp.exp(G[lo - 1:lo, :] - G[lo:hi]) if j else kx[lo:hi] * jnp.exp(-G[lo:hi])
            if j:
                step = jnp.exp(G[lo - 1:lo, :] - (G[lo - SUB - 1:lo - SUB, :] if j > 1 else 0.0))
                k_rows = [kr * step for kr in k_rows]
                qt_rows.append(qj * jnp.exp(G[lo - 1:lo, :]))
            else:
                qt_rows.append(qj)
            k_rows.append(kj)
            g0 = (j // GRP) * GRP
            qs.append(jnp.concatenate([zero] * (j - g0) + [qj.astype(BF16)] + [zero] * (g0 + GRP - 1 - j),
                                      axis=0))
            ks.append(jnp.concatenate([kr.astype(BF16) for kr in k_rows] + [zero] * (g0 + GRP - 1 - j),
                                      axis=0))
        qt = jnp.concatenate(qt_rows, axis=0).astype(BF16)
        to_end = jnp.exp(GC - G[C - SUB - 1:C - SUB, :])
        kd = jnp.concatenate([(kr * to_end).astype(BF16) for kr in k_rows], axis=0)
        for h in range(NH):
            sl = slice(h * D, (h + 1) * D)
            units.append(dict(b=b, h=h, sl=sl, qt=qt[:, sl], kd=kd[:, sl], iv=ivb[:, sl],
                              PC=PC[:, sl], g=gate[:, sl],
                              qcat=[jnp.concatenate([q[:, sl] for q in qs[g:g + GRP]], axis=1)
                                    for g in range(0, NBLK, GRP)],
                              kcat=[jnp.concatenate([k[:, sl] for k in ks[g:g + GRP]], axis=1)
                                    for g in range(0, NBLK, GRP)]))
        yield

    halves = (units[:len(units) // 2], units[len(units) // 2:])
    for part in halves:
        for u in part:
            u["A"] = [_mm_nt(q, k) for q, k in zip(u["qcat"], u["kcat"])]
        yield
    for part in halves:
        for u in part:
            u["S"] = state[u["b"] * NH + u["h"]]
            u["inter"] = _mm_nt(u["qt"], u["S"])
        yield
    for part in halves:
        for u in part:
            rows = []
            for g, A in enumerate(u["A"]):
                hi = (g + 1) * GRP * SUB
                mask = causal[hi - GRP * SUB:hi, :hi]
                rows.append(_mm(jnp.where(mask, A, 0.0), u["iv"][:hi]))
            u["o"] = jnp.concatenate(rows, axis=0) + u["inter"]
        yield
    for part in halves:
        for u in part:
            state[u["b"] * NH + u["h"]] = u["S"] * u["PC"] + _mm_tn(u["iv"], u["kd"])
        yield
    for part in halves:
        for u in part:
            o = u["o"]
            g = u["g"]
            ms = jnp.mean(o * o, axis=-1, keepdims=True)
            res = o * lax.rsqrt(ms + NORM_EPS) * ng_ref[:, u["sl"]] * (g * _sigmoid(g))
            o_ref[u["b"], :, u["sl"]] = res.astype(o_ref.dtype)
        yield


def _mixers_kernel(layer, off, pm_ref, pl_ref, ph_ref, w0_ref, w2_ref, a0_ref, a2_ref,
                   kk_ref, ka_ref, rk_ref, lw_ref, lb_ref, ones_ref, lbp_ref, ng_ref,
                   or_ref, oh_ref, r_state, bd_ref, h_state):
    @pl.when(pl.program_id(0) == 0)
    def _():
        for ref in (r_state, bd_ref, h_state):
            ref[...] = jnp.zeros_like(ref)

    rw = _rwkv_stages(pm_ref, pl_ref, w0_ref, w2_ref, a0_ref, a2_ref, kk_ref, ka_ref, rk_ref,
                      lw_ref, lb_ref, ones_ref, or_ref, r_state, bd_ref)
    hg = _hgrn_stages(layer, off, ph_ref, lbp_ref, ng_ref, oh_ref, h_state)
    for stage, _ in enumerate(rw, start=1):
        for _ in range(HGRN_AFTER_RWKV_STAGE.count(stage)):
            next(hg, None)
    for _ in hg:
        pass


def _mixers(p3, main_blk, lora_blk, h_blk_w, h_blk, h_off, layer,
            w0, w2p, a0, a2p, k_k, k_a, r_k, lnx_w, lnx_b, ones_bd, lb_param, hgrn_g):
    B, T, _ = p3.shape
    W = w0.shape[1]
    HW = hgrn_g.shape[1]
    C = CHUNK
    NU = B * (W // QW)
    L = lb_param.shape[0]
    full = lambda shape: pl.BlockSpec(shape, lambda c: (0,) * len(shape))
    return pl.pallas_call(
        functools.partial(_mixers_kernel, layer, h_off),
        grid=(T // C,),
        in_specs=[
            pl.BlockSpec((B, C, 4 * W), lambda c: (0, c, main_blk)),
            pl.BlockSpec((B, C, LORA_BLK), lambda c: (0, c, lora_blk)),
            pl.BlockSpec((B, C, h_blk_w), lambda c: (0, c, h_blk)),
            full((1, W)), full((LANE, W)), full((1, W)), full((LORA_BLK, W)),
            full((1, W)), full((1, W)), full((1, W)), full((1, W)), full((1, W)),
            full((QW, QW)), full((L, HW)), full((1, HW)),
        ],
        out_specs=[pl.BlockSpec((B, C, W), lambda c: (0, c, 0)),
                   pl.BlockSpec((B, C, HW), lambda c: (0, c, 0))],
        out_shape=[jax.ShapeDtypeStruct((B, T, W), BF16),
                   jax.ShapeDtypeStruct((B, T, HW), BF16)],
        scratch_shapes=[
            pltpu.VMEM((NU, QW, QW), F32),
            pltpu.VMEM((NU * len(BD_ROLES), QW, QW), BF16),
            pltpu.VMEM((B * (HW // HGRN_HEAD), HGRN_HEAD, HGRN_HEAD), F32),
        ],
        compiler_params=pltpu.CompilerParams(
            dimension_semantics=("arbitrary",), vmem_limit_bytes=VMEM_LIMIT),
        name="mixers",
    )(p3, p3, p3, w0, w2p, a0, a2p, k_k, k_a, r_k, lnx_w, lnx_b, ones_bd, lb_param, hgrn_g)


def _outproj_kernel(final, yr_ref, yh_ref, x_ref, wr_ref, wh_ref, g_ref, o_ref, wb_ref):
    @pl.when(pl.program_id(0) == 0)
    def _():
        wb_ref[0] = wr_ref[...].astype(BF16)
        wb_ref[1] = wh_ref[...].astype(BF16)

    acc = jnp.dot(yr_ref[...], wb_ref[0], preferred_element_type=F32)
    acc += jnp.dot(yh_ref[...], wb_ref[1], preferred_element_type=F32)
    h = x_ref[...] + acc
    if final:
        ms = jnp.mean(h * h, axis=-1, keepdims=True)
        h = h * lax.rsqrt(ms + NORM_EPS) * g_ref[...]
    o_ref[...] = h


def _outproj(yr, yh, x2, w_all, layer, g, tm, final):
    m, d = x2.shape
    wr_w = yr.shape[1]
    wh_w = yh.shape[1]
    assert wr_w == wh_w and w_all.shape[1:] == (wr_w + wh_w, d)
    once = dict(pipeline_mode=pl.Buffered(1))
    return pl.pallas_call(
        functools.partial(_outproj_kernel, final),
        grid=(m // tm,),
        in_specs=[
            pl.BlockSpec((tm, wr_w), lambda i: (i, 0)),
            pl.BlockSpec((tm, wh_w), lambda i: (i, 0)),
            pl.BlockSpec((tm, d), lambda i: (i, 0)),
            pl.BlockSpec((None, wr_w, d), lambda i: (layer, 0, 0), **once),
            pl.BlockSpec((None, wh_w, d), lambda i: (layer, 1, 0), **once),
            pl.BlockSpec((1, d), lambda i: (0, 0)),
        ],
        out_specs=pl.BlockSpec((tm, d), lambda i: (i, 0)),
        out_shape=jax.ShapeDtypeStruct((m, d), F32),
        scratch_shapes=[pltpu.VMEM((2, wr_w, d), BF16)],
        compiler_params=pltpu.CompilerParams(
            dimension_semantics=("arbitrary",), vmem_limit_bytes=VMEM_LIMIT),
        name="outproj",
    )(yr, yh, x2, w_all, w_all, g)


def kernel(x, norm_g, w_in, mu, w0, w2, a0, a2, k_k, k_a, r_k, lnx_w, lnx_b,
           hgrn_norm_g, lb_param, w_out, final_g):
    B, T, D = x.shape
    depth = w_in.shape[0]
    RW = w0.shape[1]
    HW = hgrn_norm_g.shape[1]
    n_r = 4 * RW + 2 * LORA
    tn = 768
    gap = -n_r % LANE
    n_p = w_in.shape[2] + gap
    h_start = (n_r // LANE) * LANE
    h_off = n_r + gap - h_start
    h_blk_w = n_p - h_start
    assert (4 * RW) % LORA_BLK == 0 and 2 * LORA <= LORA_BLK and w2.shape[1] == LORA <= LANE
    assert h_start % h_blk_w == 0 and h_off + 4 * HW <= h_blk_w

    row = lambda z: z.reshape(1, -1).astype(F32)
    zr = lambda rows: jnp.zeros((rows, RW), F32)
    ones_bd = (jnp.arange(QW)[:, None] // RWKV_HEAD == jnp.arange(QW)[None, :] // RWKV_HEAD).astype(BF16)

    w_in_t = jnp.swapaxes(w_in, 1, 2)
    h = x.reshape(B * T, D)
    for l in range(depth):
        mu_full = jnp.concatenate([mu[l].astype(F32), jnp.zeros((n_p - n_r,), F32)]).reshape(1, -1)
        w2p = jnp.concatenate([w2[l].astype(F32), zr(LANE - LORA)], axis=0).astype(BF16)
        a2p = jnp.concatenate([zr(LORA), a2[l].astype(F32), zr(LORA_BLK - 2 * LORA)],
                              axis=0).astype(BF16)

        p = _inproj(h, row(norm_g[l]), w_in_t, l, mu_full, n_r, T, tm=1024, tn=tn, groups=4)
        p3 = p.reshape(B, T, n_p)
        y_r, y_h = _mixers(p3, 0, (4 * RW) // LORA_BLK, h_blk_w, h_start // h_blk_w, h_off, l,
                           row(w0[l]), w2p, row(a0[l]), a2p, row(k_k[l]), row(k_a[l]),
                           row(r_k[l]), row(lnx_w[l]), row(lnx_b[l]), ones_bd,
                           lb_param.astype(F32), row(hgrn_norm_g[l]))
        h = _outproj(y_r.reshape(B * T, RW), y_h.reshape(B * T, HW), h,
                     w_out, l, row(final_g), tm=512, final=(l == depth - 1))
    return h.reshape(B, T, D)
```

```python
import functools
import math

import jax
import jax.numpy as jnp
from jax import lax
from jax.experimental import pallas as pl
from jax.experimental.pallas import tpu as pltpu

F32 = jnp.float32
BF16 = jnp.bfloat16

NORM_EPS = 1e-6
LNX_EPS = 64e-5
RWKV_HEAD = 64
HGRN_HEAD = 128
LORA = 96
LORA_BLK = 256
LANE = 128
CHUNK = 64
CHUNKS_PER_STEP = 2
HGRN_SUB = 16
QUAD = 4
QW = QUAD * RWKV_HEAD
BD_ROLES = ("B", "K", "V", "N", "Z", "U")
HGRN_AFTER_RWKV_STAGE = tuple(range(6, 18))
EXP_M05 = math.exp(-0.5)
VMEM_LIMIT = 48 * 1024 * 1024
VMEM_MARGIN = 6 * 1024 * 1024


def _mm(a, b):
    return jnp.dot(a.astype(BF16), b.astype(BF16), preferred_element_type=F32)


def _mm_nt(a, b):
    return lax.dot_general(a.astype(BF16), b.astype(BF16), (((1,), (1,)), ((), ())),
                           preferred_element_type=F32)


def _mm_tn(a, b):
    return lax.dot_general(a.astype(BF16), b.astype(BF16), (((0,), (0,)), ((), ())),
                           preferred_element_type=F32)


def _cumsum_rows(x, tril_bf16):
    hi = x.astype(BF16)
    lo = (x - hi.astype(F32)).astype(BF16)
    dot = functools.partial(jnp.dot, preferred_element_type=F32)
    return dot(tril_bf16, hi) + dot(tril_bf16, lo)


def _sigmoid(x):
    return 0.5 + 0.5 * jnp.tanh(0.5 * x)


def _inproj_kernel(gap_tile, gap_at, gap, seq_len, x_ref, g_ref, w_ref, mu_ref, o_ref, hn_ref, carry_ref):
    q = pl.program_id(0)
    j = pl.program_id(1)
    i = pl.program_id(2)
    tm, tn = o_ref.shape

    @pl.when((q == 0) & (j == 0) & (i == 0))
    def _():
        carry_ref[...] = jnp.zeros_like(carry_ref)

    @pl.when(j == 0)
    def _():
        x = x_ref[...]
        ms = jnp.mean(x * x, axis=-1, keepdims=True)
        hn_ref[i] = (x * lax.rsqrt(ms + NORM_EPS) * g_ref[...]).astype(BF16)

    w = w_ref[0]
    if gap:
        w_gap = jnp.concatenate([w[:gap_at], jnp.zeros((gap, w.shape[1]), w.dtype),
                                 w[gap_at:tn - gap]], axis=0)
        w = jnp.where(j == gap_tile, w_gap, w)
    acc = _mm_nt(hn_ref[i], w)

    first_row = (q * pl.num_programs(2) + i) * tm
    prev_last = jnp.where(first_row % seq_len == 0, 0.0, carry_ref[j])
    carry_ref[j] = acc[tm - 1:tm, :]
    is_row0 = lax.broadcasted_iota(jnp.int32, (tm, 1), 0) == 0
    prev = jnp.where(is_row0, prev_last, pltpu.roll(acc, 1, axis=0))
    o_ref[...] = acc + mu_ref[...] * (prev - acc)


def _inproj(x2, g, wt_all, layer, mu_full, gap_col, seq_len, tm, tn, groups):
    m, d = x2.shape
    n = mu_full.shape[1]
    gap = n - wt_all.shape[1]
    gap_tile, gap_at = divmod(gap_col, tn)
    assert n % tn == 0 and 0 <= gap and gap_at + gap <= tn and gap % 8 == 0 and gap_at % 8 == 0
    w_map = lambda q, j, i: (
        layer, pl.multiple_of(jnp.where(j <= gap_tile, j * tn, j * tn - gap), 8), 0)
    ni = m // (groups * tm)
    assert seq_len % tm == 0 and mu_full.shape == (1, n)
    vmem = ni * tm * d * 2 + 2 * 4 * (tm * d + tn * d + tm * tn) + VMEM_MARGIN
    x_map = lambda q, j, i: (q * ni + jnp.where(j == 0, i, ni - 1), 0)
    return pl.pallas_call(
        functools.partial(_inproj_kernel, gap_tile, gap_at, gap, seq_len),
        grid=(groups, n // tn, ni),
        in_specs=[
            pl.BlockSpec((tm, d), x_map),
            pl.BlockSpec((1, d), lambda q, j, i: (0, 0)),
            pl.BlockSpec((pl.Element(1), pl.Element(tn), pl.Element(d)), w_map),
            pl.BlockSpec((1, tn), lambda q, j, i: (0, j)),
        ],
        out_specs=pl.BlockSpec((tm, tn), lambda q, j, i: (q * ni + i, j)),
        out_shape=jax.ShapeDtypeStruct((m, n), F32),
        scratch_shapes=[pltpu.VMEM((ni, tm, d), BF16), pltpu.VMEM((n // tn, 1, tn), F32)],
        compiler_params=pltpu.CompilerParams(
            dimension_semantics=("arbitrary", "arbitrary", "arbitrary"),
            vmem_limit_bytes=vmem),
        name="inproj",
    )(x2, g, wt_all, mu_full)


def _rwkv_stages(pm_ref, pl_ref, w0_ref, w2_ref, a0_ref, a2_ref,
                 kk_ref, ka_ref, rk_ref, lw_ref, lb_ref, ones_ref,
                 o_ref, state, bd_ref, ck):
    C = CHUNK
    NB = pm_ref.shape[0]
    W = w0_ref.shape[1]
    NQ = W // QW
    rows = slice(ck * C, (ck + 1) * C)

    t_i = lax.broadcasted_iota(jnp.int32, (C, C), 0)
    s_i = lax.broadcasted_iota(jnp.int32, (C, C), 1)
    tril = (t_i >= s_i).astype(BF16)
    ones_bd = ones_ref[...]
    dot = functools.partial(jnp.dot, preferred_element_type=F32)

    lane = lax.broadcasted_iota(jnp.int32, (C, QW), 1)
    lane_s = lane % RWKV_HEAD
    trow = lax.broadcasted_iota(jnp.int32, (C, QW), 0)
    strict = lane_s < trow
    incl = lane_s <= trow
    eye_q = jnp.where(lane_s == trow, 1.0, 0.0).astype(F32)
    bd_mask = (lax.broadcasted_iota(jnp.int32, (QW, QW), 0) // RWKV_HEAD
               == lax.broadcasted_iota(jnp.int32, (QW, QW), 1) // RWKV_HEAD)

    def bd(u, role, x):
        slot = (ck * NB * NQ + u["ui"]) * len(BD_ROLES) + BD_ROLES.index(role)
        xb = x.astype(BF16)
        for h in range(QUAD):
            hs = slice(h * RWKV_HEAD, (h + 1) * RWKV_HEAD)
            bd_ref[slot, h * C:(h + 1) * C, hs] = xb[:, hs]
        return bd_ref[slot]

    def segsum_all(xs):
        out = dot(jnp.concatenate([x.astype(BF16) for x in xs], axis=0), ones_bd)
        return [out[i * C:(i + 1) * C] for i in range(len(xs))]

    xl = jnp.concatenate([pl_ref[b, rows, :] for b in range(NB)], axis=0)
    z = w0_ref[...] + _mm(jnp.tanh(xl[:, :LANE]), w2_ref[...])
    ld_all = -EXP_M05 * _sigmoid(z)
    a_all = _sigmoid(a0_ref[...] + _mm(xl, a2_ref[...]))
    units = []
    for b in range(NB):
        r = pm_ref[b, rows, 0 * W:1 * W]
        k = pm_ref[b, rows, 1 * W:2 * W]
        v = pm_ref[b, rows, 2 * W:3 * W]
        gate = pm_ref[b, rows, 3 * W:4 * W]
        ld = ld_all[b * C:(b + 1) * C]
        a = a_all[b * C:(b + 1) * C]
        G = _cumsum_rows(ld, tril)
        Gx = G - ld
        GC = G[C - 1:C, :]
        for q in range(NQ):
            sl = slice(q * QW, (q + 1) * QW)
            units.append(dict(b=b, ui=b * NQ + q, sl=sl, r=r[:, sl], k=k[:, sl], v=v[:, sl],
                              g=gate[:, sl], a=a[:, sl], G=G[:, sl], Gx=Gx[:, sl], GC=GC[:, sl]))
    yield

    for u in units:
        u["kk"] = u["k"] * kk_ref[:, u["sl"]]
    for u, n2 in zip(units, segsum_all([u["kk"] * u["kk"] for u in units])):
        u["n2"] = n2
    yield

    def unit_chain(u):
        sl = u["sl"]
        kkn = u["kk"] * lax.rsqrt(jnp.maximum(u["n2"], 1e-24))
        k2 = u["k"] * (1.0 + (u["a"] - 1.0) * ka_ref[:, sl])
        av = -kkn
        bv = kkn * u["a"]
        einv = jnp.exp(-u["G"])
        eC = jnp.exp(u["GC"]) * einv
        u["rkk"] = u["r"] * k2 * rk_ref[:, sl]
        BDB = bd(u, "B", bv * einv)
        BDK = bd(u, "K", k2 * einv)
        BKe = jnp.concatenate([bv * eC, k2 * eC], axis=0).astype(BF16)
        BDV = bd(u, "V", u["v"])
        lhs = jnp.concatenate([av * jnp.exp(u["Gx"]), u["r"] * jnp.exp(u["G"])],
                              axis=0).astype(BF16)
        pc_col = jnp.transpose(jnp.broadcast_to(jnp.exp(u["GC"]), (LANE, QW)))
        PCc = jnp.concatenate([pc_col] * (QW // LANE), axis=1)
        yield
        SB = _mm_nt(lhs, BDB)
        yield
        SK = _mm_nt(lhs, BDK)
        yield
        N = jnp.where(strict, SB[:C], 0.0)
        Lrb = jnp.where(incl, SB[C:], 0.0)
        P = eye_q + N
        Np = _mm(N, bd(u, "N", N))
        yield
        akl = jnp.concatenate([jnp.where(strict, SK[:C], 0.0),
                               jnp.where(incl, SK[C:], 0.0)], axis=0)
        both = _mm(akl, BDV)
        yield
        for _ in range(4):
            out = _mm(jnp.concatenate([P, Np], axis=0), bd(u, "N", Np))
            P = P + out[:C]
            Np = out[C:]
            yield
        P = P + _mm(P, bd(u, "N", Np))
        yield
        H = state[u["ui"]]
        zy = _mm(lhs, H)
        yield
        U = _mm(P, bd(u, "Z", zy[:C] + both[:C]))
        yield
        u["Y"] = zy[C:] + _mm(Lrb, bd(u, "U", U)) + both[C:]
        yield
        upd = _mm_tn(BKe, jnp.concatenate([U, u["v"]], axis=0))
        state[u["ui"]] = H * PCc + jnp.where(bd_mask, upd, 0.0)

    chains = [unit_chain(u) for u in units]
    live = list(range(len(chains)))
    while live:
        for i in list(live):
            if next(chains[i], "done") == "done":
                live.remove(i)
        yield

    for u, s in zip(units, segsum_all([u["Y"] for u in units])):
        u["yc"] = u["Y"] - s * (1.0 / RWKV_HEAD)
    yield
    for u, s in zip(units, segsum_all([u["yc"] * u["yc"] for u in units])):
        u["var"] = s * (1.0 / RWKV_HEAD)
    yield
    for u, s in zip(units, segsum_all([u["rkk"] for u in units])):
        u["bonus"] = s
    yield
    for u in units:
        sl = u["sl"]
        yn = u["yc"] * lax.rsqrt(u["var"] + LNX_EPS) * lw_ref[:, sl] + lb_ref[:, sl]
        res = (yn + u["bonus"] * u["v"]) * (u["g"] * _sigmoid(u["g"]))
        o_ref[u["b"], rows, sl] = res.astype(o_ref.dtype)


def _hgrn_stages(layer, off, ph_ref, lbp_ref, ng_ref, o_ref, state, ck):
    C = CHUNK
    NB = ph_ref.shape[0]
    W = ng_ref.shape[1]
    D = HGRN_HEAD
    NH = W // D
    SUB = HGRN_SUB
    NBLK = C // SUB
    GRP = 2 * LANE // D

    lp = lbp_ref[...]
    e = jnp.exp(lp - jnp.max(lp, axis=0, keepdims=True))
    lb = jnp.sum(e[:layer + 1], axis=0, keepdims=True) / jnp.sum(e, axis=0, keepdims=True)

    t_i = lax.broadcasted_iota(jnp.int32, (C, C), 0)
    s_i = lax.broadcasted_iota(jnp.int32, (C, C), 1)
    causal = t_i >= s_i
    tril = causal.astype(BF16)

    units = []
    for b in range(NB):
        x = ph_ref[b, ck * C:(ck + 1) * C, :]
        qv = x[:, off + 0 * W:off + 1 * W]
        fr = x[:, off + 1 * W:off + 2 * W]
        iv = x[:, off + 2 * W:off + 3 * W]
        gate = x[:, off + 3 * W:off + 4 * W]
        f = lb + (1.0 - lb) * _sigmoid(fr)
        kx = 1.0 - f
        G = _cumsum_rows(jnp.log(f), tril)
        GC = G[C - 1:C, :]
        PC = jnp.exp(GC)
        ivb = iv.astype(BF16)
        zero = jnp.zeros((SUB, W), BF16)
        qs, ks, qt_rows, k_rows = [], [], [], []
        for j in range(NBLK):
            lo, hi = j * SUB, (j + 1) * SUB
            qj = qv[lo:hi] * jnp.exp(G[lo:hi] - G[lo - 1:lo, :]) if j else qv[lo:hi] * jnp.exp(G[lo:hi])
            kj = kx[lo:hi] * jnp.exp(G[lo - 1:lo, :] - G[lo:hi]) if j else kx[lo:hi] * jnp.exp(-G[lo:hi])
            if j:
                step = jnp.exp(G[lo - 1:lo, :] - (G[lo - SUB - 1:lo - SUB, :] if j > 1 else 0.0))
                k_rows = [kr * step for kr in k_rows]
                qt_rows.append(qj * jnp.exp(G[lo - 1:lo, :]))
            else:
                qt_rows.append(qj)
            k_rows.append(kj)
            g0 = (j // GRP) * GRP
            qs.append(jnp.concatenate([zero] * (j - g0) + [qj.astype(BF16)] + [zero] * (g0 + GRP - 1 - j),
                                      axis=0))
            ks.append(jnp.concatenate([kr.astype(BF16) for kr in k_rows] + [zero] * (g0 + GRP - 1 - j),
                                      axis=0))
        qt = jnp.concatenate(qt_rows, axis=0).astype(BF16)
        to_end = jnp.exp(GC - G[C - SUB - 1:C - SUB, :])
        kd = jnp.concatenate([(kr * to_end).astype(BF16) for kr in k_rows], axis=0)
        for h in range(NH):
            sl = slice(h * D, (h + 1) * D)
            units.append(dict(b=b, h=h, sl=sl, qt=qt[:, sl], kd=kd[:, sl], iv=ivb[:, sl],
                              PC=PC[:, sl], g=gate[:, sl],
                              qcat=[jnp.concatenate([q[:, sl] for q in qs[g:g + GRP]], axis=1)
                                    for g in range(0, NBLK, GRP)],
                              kcat=[jnp.concatenate([k[:, sl] for k in ks[g:g + GRP]], axis=1)
                                    for g in range(0, NBLK, GRP)]))
        yield

    halves = (units[:len(units) // 2], units[len(units) // 2:])
    for part in halves:
        for u in part:
            u["A"] = [_mm_nt(q, k) for q, k in zip(u["qcat"], u["kcat"])]
        yield
    for part in halves:
        for u in part:
            u["S"] = state[u["b"] * NH + u["h"]]
            u["inter"] = _mm_nt(u["qt"], u["S"])
        yield
    for part in halves:
        for u in part:
            rows = []
            for g, A in enumerate(u["A"]):
                hi = (g + 1) * GRP * SUB
                mask = causal[hi - GRP * SUB:hi, :hi]
                rows.append(_mm(jnp.where(mask, A, 0.0), u["iv"][:hi]))
            u["o"] = jnp.concatenate(rows, axis=0) + u["inter"]
        yield
    for part in halves:
        for u in part:
            state[u["b"] * NH + u["h"]] = u["S"] * u["PC"] + _mm_tn(u["iv"], u["kd"])
        yield
    for part in halves:
        for u in part:
            o = u["o"]
            g = u["g"]
            ms = jnp.mean(o * o, axis=-1, keepdims=True)
            res = o * lax.rsqrt(ms + NORM_EPS) * ng_ref[:, u["sl"]] * (g * _sigmoid(g))
            o_ref[u["b"], ck * C:(ck + 1) * C, u["sl"]] = res.astype(o_ref.dtype)
        yield


def _mixers_kernel(layer, off, pm_ref, pl_ref, ph_ref, w0_ref, w2_ref, a0_ref, a2_ref,
                   kk_ref, ka_ref, rk_ref, lw_ref, lb_ref, ones_ref, lbp_ref, ng_ref,
                   or_ref, oh_ref, r_state, bd_ref, h_state):
    @pl.when(pl.program_id(0) == 0)
    def _():
        for ref in (r_state, bd_ref, h_state):
            ref[...] = jnp.zeros_like(ref)

    for ck in range(CHUNKS_PER_STEP):
        rw = _rwkv_stages(pm_ref, pl_ref, w0_ref, w2_ref, a0_ref, a2_ref, kk_ref, ka_ref, rk_ref,
                          lw_ref, lb_ref, ones_ref, or_ref, r_state, bd_ref, ck)
        hg = _hgrn_stages(layer, off, ph_ref, lbp_ref, ng_ref, oh_ref, h_state, ck)
        for stage, _ in enumerate(rw, start=1):
            for _ in range(HGRN_AFTER_RWKV_STAGE.count(stage)):
                next(hg, None)
        for _ in hg:
            pass


def _mixers(p3, main_blk, lora_blk, h_blk_w, h_blk, h_off, layer,
            w0, w2p, a0, a2p, k_k, k_a, r_k, lnx_w, lnx_b, ones_bd, lb_param, hgrn_g):
    B, T, _ = p3.shape
    W = w0.shape[1]
    HW = hgrn_g.shape[1]
    TB = CHUNKS_PER_STEP * CHUNK
    NU = B * (W // QW)
    L = lb_param.shape[0]
    full = lambda shape: pl.BlockSpec(shape, lambda c: (0,) * len(shape))
    return pl.pallas_call(
        functools.partial(_mixers_kernel, layer, h_off),
        grid=(T // TB,),
        in_specs=[
            pl.BlockSpec((B, TB, 4 * W), lambda c: (0, c, main_blk)),
            pl.BlockSpec((B, TB, LORA_BLK), lambda c: (0, c, lora_blk)),
            pl.BlockSpec((B, TB, h_blk_w), lambda c: (0, c, h_blk)),
            full((1, W)), full((LANE, W)), full((1, W)), full((LORA_BLK, W)),
            full((1, W)), full((1, W)), full((1, W)), full((1, W)), full((1, W)),
            full((QW, QW)), full((L, HW)), full((1, HW)),
        ],
        out_specs=[pl.BlockSpec((B, TB, W), lambda c: (0, c, 0)),
                   pl.BlockSpec((B, TB, HW), lambda c: (0, c, 0))],
        out_shape=[jax.ShapeDtypeStruct((B, T, W), BF16),
                   jax.ShapeDtypeStruct((B, T, HW), BF16)],
        scratch_shapes=[
            pltpu.VMEM((NU, QW, QW), F32),
            pltpu.VMEM((CHUNKS_PER_STEP * NU * len(BD_ROLES), QW, QW), BF16),
            pltpu.VMEM((B * (HW // HGRN_HEAD), HGRN_HEAD, HGRN_HEAD), F32),
        ],
        compiler_params=pltpu.CompilerParams(
            dimension_semantics=("arbitrary",), vmem_limit_bytes=VMEM_LIMIT),
        name="mixers",
    )(p3, p3, p3, w0, w2p, a0, a2p, k_k, k_a, r_k, lnx_w, lnx_b, ones_bd, lb_param, hgrn_g)


def _outproj_kernel(final, yr_ref, yh_ref, x_ref, wr_ref, wh_ref, g_ref, o_ref, wb_ref):
    @pl.when(pl.program_id(0) == 0)
    def _():
        wb_ref[0] = wr_ref[...].astype(BF16)
        wb_ref[1] = wh_ref[...].astype(BF16)

    acc = jnp.dot(yr_ref[...], wb_ref[0], preferred_element_type=F32)
    acc += jnp.dot(yh_ref[...], wb_ref[1], preferred_element_type=F32)
    h = x_ref[...] + acc
    if final:
        ms = jnp.mean(h * h, axis=-1, keepdims=True)
        h = h * lax.rsqrt(ms + NORM_EPS) * g_ref[...]
    o_ref[...] = h


def _outproj(yr, yh, x2, w_all, layer, g, tm, final):
    m, d = x2.shape
    wr_w = yr.shape[1]
    wh_w = yh.shape[1]
    assert wr_w == wh_w and w_all.shape[1:] == (wr_w + wh_w, d)
    once = dict(pipeline_mode=pl.Buffered(1))
    return pl.pallas_call(
        functools.partial(_outproj_kernel, final),
        grid=(m // tm,),
        in_specs=[
            pl.BlockSpec((tm, wr_w), lambda i: (i, 0)),
            pl.BlockSpec((tm, wh_w), lambda i: (i, 0)),
            pl.BlockSpec((tm, d), lambda i: (i, 0)),
            pl.BlockSpec((None, wr_w, d), lambda i: (layer, 0, 0), **once),
            pl.BlockSpec((None, wh_w, d), lambda i: (layer, 1, 0), **once),
            pl.BlockSpec((1, d), lambda i: (0, 0)),
        ],
        out_specs=pl.BlockSpec((tm, d), lambda i: (i, 0)),
        out_shape=jax.ShapeDtypeStruct((m, d), F32),
        scratch_shapes=[pltpu.VMEM((2, wr_w, d), BF16)],
        compiler_params=pltpu.CompilerParams(
            dimension_semantics=("arbitrary",), vmem_limit_bytes=VMEM_LIMIT),
        name="outproj",
    )(yr, yh, x2, w_all, w_all, g)


def kernel(x, norm_g, w_in, mu, w0, w2, a0, a2, k_k, k_a, r_k, lnx_w, lnx_b,
           hgrn_norm_g, lb_param, w_out, final_g):
    B, T, D = x.shape
    depth = w_in.shape[0]
    RW = w0.shape[1]
    HW = hgrn_norm_g.shape[1]
    n_r = 4 * RW + 2 * LORA
    tn = 768
    gap = -n_r % LANE
    n_p = w_in.shape[2] + gap
    h_start = (n_r // LANE) * LANE
    h_off = n_r + gap - h_start
    h_blk_w = n_p - h_start
    assert (4 * RW) % LORA_BLK == 0 and 2 * LORA <= LORA_BLK and w2.shape[1] == LORA <= LANE
    assert h_start % h_blk_w == 0 and h_off + 4 * HW <= h_blk_w

    row = lambda z: z.reshape(1, -1).astype(F32)
    zr = lambda rows: jnp.zeros((rows, RW), F32)
    ones_bd = (jnp.arange(QW)[:, None] // RWKV_HEAD == jnp.arange(QW)[None, :] // RWKV_HEAD).astype(BF16)

    w_in_t = jnp.swapaxes(w_in, 1, 2)
    h = x.reshape(B * T, D)
    for l in range(depth):
        mu_full = jnp.concatenate([mu[l].astype(F32), jnp.zeros((n_p - n_r,), F32)]).reshape(1, -1)
        w2p = jnp.concatenate([w2[l].astype(F32), zr(LANE - LORA)], axis=0).astype(BF16)
        a2p = jnp.concatenate([zr(LORA), a2[l].astype(F32), zr(LORA_BLK - 2 * LORA)],
                              axis=0).astype(BF16)

        p = _inproj(h, row(norm_g[l]), w_in_t, l, mu_full, n_r, T, tm=1024, tn=tn, groups=4)
        p3 = p.reshape(B, T, n_p)
        y_r, y_h = _mixers(p3, 0, (4 * RW) // LORA_BLK, h_blk_w, h_start // h_blk_w, h_off, l,
                           row(w0[l]), w2p, row(a0[l]), a2p, row(k_k[l]), row(k_a[l]),
                           row(r_k[l]), row(lnx_w[l]), row(lnx_b[l]), ones_bd,
                           lb_param.astype(F32), row(hgrn_norm_g[l]))
        h = _outproj(y_r.reshape(B * T, RW), y_h.reshape(B * T, HW), h,
                     w_out, l, row(final_g), tm=512, final=(l == depth - 1))
    return h.reshape(B, T, D)
```

```python
import functools
import math

import jax
import jax.numpy as jnp
from jax import lax
from jax.experimental import pallas as pl
from jax.experimental.pallas import tpu as pltpu

F32 = jnp.float32
BF16 = jnp.bfloat16

NORM_EPS = 1e-6
LNX_EPS = 64e-5
RWKV_HEAD = 64
HGRN_HEAD = 128
LORA = 96
LORA_BLK = 256
LANE = 128
CHUNK = 64
HGRN_SUB = 16
QUAD = 4
QW = QUAD * RWKV_HEAD
BD_ROLES = ("B", "K", "V", "N", "Z", "U")
HGRN_AFTER_RWKV_STAGE = tuple(range(6, 18))
EXP_M05 = math.exp(-0.5)
VMEM_LIMIT = 48 * 1024 * 1024
VMEM_MARGIN = 6 * 1024 * 1024


def _mm(a, b):
    return jnp.dot(a.astype(BF16), b.astype(BF16), preferred_element_type=F32)


def _mm_nt(a, b):
    return lax.dot_general(a.astype(BF16), b.astype(BF16), (((1,), (1,)), ((), ())),
                           preferred_element_type=F32)


def _mm_tn(a, b):
    return lax.dot_general(a.astype(BF16), b.astype(BF16), (((0,), (0,)), ((), ())),
                           preferred_element_type=F32)


def _cumsum_rows(x, tril_bf16):
    hi = x.astype(BF16)
    lo = (x - hi.astype(F32)).astype(BF16)
    dot = functools.partial(jnp.dot, preferred_element_type=F32)
    return dot(tril_bf16, hi) + dot(tril_bf16, lo)


def _sigmoid(x):
    return 0.5 + 0.5 * jnp.tanh(0.5 * x)


def _inproj_kernel(gap_tile, gap_at, gap, seq_len, x_ref, g_ref, w_ref, mu_ref, o_ref, hn_ref, carry_ref):
    q = pl.program_id(0)
    j = pl.program_id(1)
    i = pl.program_id(2)
    tm, tn = o_ref.shape

    @pl.when((q == 0) & (j == 0) & (i == 0))
    def _():
        carry_ref[...] = jnp.zeros_like(carry_ref)

    @pl.when(j == 0)
    def _():
        x = x_ref[...]
        ms = jnp.mean(x * x, axis=-1, keepdims=True)
        hn_ref[i] = (x * lax.rsqrt(ms + NORM_EPS) * g_ref[...]).astype(BF16)

    w = w_ref[0]
    if gap:
        w_gap = jnp.concatenate([w[:gap_at], jnp.zeros((gap, w.shape[1]), w.dtype),
                                 w[gap_at:tn - gap]], axis=0)
        w = jnp.where(j == gap_tile, w_gap, w)
    acc = _mm_nt(hn_ref[i], w)

    first_row = (q * pl.num_programs(2) + i) * tm
    prev_last = jnp.where(first_row % seq_len == 0, 0.0, carry_ref[j])
    carry_ref[j] = acc[tm - 1:tm, :]
    is_row0 = lax.broadcasted_iota(jnp.int32, (tm, 1), 0) == 0
    prev = jnp.where(is_row0, prev_last, pltpu.roll(acc, 1, axis=0))
    o_ref[...] = acc + mu_ref[...] * (prev - acc)


def _inproj(x2, g, wt_all, layer, mu_full, gap_col, seq_len, tm, tn, groups):
    m, d = x2.shape
    n = mu_full.shape[1]
    gap = n - wt_all.shape[1]
    gap_tile, gap_at = divmod(gap_col, tn)
    assert n % tn == 0 and 0 <= gap and gap_at + gap <= tn and gap % 8 == 0 and gap_at % 8 == 0
    w_map = lambda q, j, i: (
        layer, pl.multiple_of(jnp.where(j <= gap_tile, j * tn, j * tn - gap), 8), 0)
    ni = m // (groups * tm)
    assert seq_len % tm == 0 and mu_full.shape == (1, n)
    vmem = ni * tm * d * 2 + 2 * 4 * (tm * d + tn * d + tm * tn) + VMEM_MARGIN
    x_map = lambda q, j, i: (q * ni + jnp.where(j == 0, i, ni - 1), 0)
    return pl.pallas_call(
        functools.partial(_inproj_kernel, gap_tile, gap_at, gap, seq_len),
        grid=(groups, n // tn, ni),
        in_specs=[
            pl.BlockSpec((tm, d), x_map),
            pl.BlockSpec((1, d), lambda q, j, i: (0, 0)),
            pl.BlockSpec((pl.Element(1), pl.Element(tn), pl.Element(d)), w_map),
            pl.BlockSpec((1, tn), lambda q, j, i: (0, j)),
        ],
        out_specs=pl.BlockSpec((tm, tn), lambda q, j, i: (q * ni + i, j)),
        out_shape=jax.ShapeDtypeStruct((m, n), F32),
        scratch_shapes=[pltpu.VMEM((ni, tm, d), BF16), pltpu.VMEM((n // tn, 1, tn), F32)],
        compiler_params=pltpu.CompilerParams(
            dimension_semantics=("arbitrary", "arbitrary", "arbitrary"),
            vmem_limit_bytes=vmem),
        name="inproj",
    )(x2, g, wt_all, mu_full)


def _rwkv_stages(pm_ref, pl_ref, w0_ref, w2_ref, a0_ref, a2_ref,
                 kk_ref, ka_ref, rk_ref, lw_ref, lb_ref, ones_ref,
                 o_ref, state, bd_ref):
    C = CHUNK
    NB = pm_ref.shape[0]
    W = w0_ref.shape[1]
    NQ = W // QW

    t_i = lax.broadcasted_iota(jnp.int32, (C, C), 0)
    s_i = lax.broadcasted_iota(jnp.int32, (C, C), 1)
    tril = (t_i >= s_i).astype(BF16)
    ones_bd = ones_ref[...]
    dot = functools.partial(jnp.dot, preferred_element_type=F32)

    lane = lax.broadcasted_iota(jnp.int32, (C, QW), 1)
    lane_s = lane % RWKV_HEAD
    lane_blk = lane // RWKV_HEAD
    trow = lax.broadcasted_iota(jnp.int32, (C, QW), 0)
    strict = lane_s < trow
    incl = lane_s <= trow
    eye_q = jnp.where(lane_s == trow, 1.0, 0.0).astype(F32)
    bd_mask = (lax.broadcasted_iota(jnp.int32, (QW, QW), 0) // RWKV_HEAD
               == lax.broadcasted_iota(jnp.int32, (QW, QW), 1) // RWKV_HEAD)

    def bd(u, role, x):
        xb = x.astype(BF16)
        zero = jnp.zeros_like(xb)
        return jnp.concatenate([jnp.where(lane_blk == h, xb, zero) for h in range(QUAD)], axis=0)

    def segsum_all(xs):
        out = dot(jnp.concatenate([x.astype(BF16) for x in xs], axis=0), ones_bd)
        return [out[i * C:(i + 1) * C] for i in range(len(xs))]

    xl = jnp.concatenate([pl_ref[b] for b in range(NB)], axis=0)
    z = w0_ref[...] + _mm(jnp.tanh(xl[:, :LANE]), w2_ref[...])
    ld_all = -EXP_M05 * _sigmoid(z)
    a_all = _sigmoid(a0_ref[...] + _mm(xl, a2_ref[...]))
    units = []
    for b in range(NB):
        r = pm_ref[b, :, 0 * W:1 * W]
        k = pm_ref[b, :, 1 * W:2 * W]
        v = pm_ref[b, :, 2 * W:3 * W]
        gate = pm_ref[b, :, 3 * W:4 * W]
        ld = ld_all[b * C:(b + 1) * C]
        a = a_all[b * C:(b + 1) * C]
        G = _cumsum_rows(ld, tril)
        Gx = G - ld
        GC = G[C - 1:C, :]
        for q in range(NQ):
            sl = slice(q * QW, (q + 1) * QW)
            units.append(dict(b=b, ui=b * NQ + q, sl=sl, r=r[:, sl], k=k[:, sl], v=v[:, sl],
                              g=gate[:, sl], a=a[:, sl], G=G[:, sl], Gx=Gx[:, sl], GC=GC[:, sl]))
    yield

    for u in units:
        u["kk"] = u["k"] * kk_ref[:, u["sl"]]
    for u, n2 in zip(units, segsum_all([u["kk"] * u["kk"] for u in units])):
        u["n2"] = n2
    yield

    def unit_chain(u):
        sl = u["sl"]
        kkn = u["kk"] * lax.rsqrt(jnp.maximum(u["n2"], 1e-24))
        k2 = u["k"] * (1.0 + (u["a"] - 1.0) * ka_ref[:, sl])
        av = -kkn
        bv = kkn * u["a"]
        einv = jnp.exp(-u["G"])
        eC = jnp.exp(u["GC"]) * einv
        u["rkk"] = u["r"] * k2 * rk_ref[:, sl]
        BDB = bd(u, "B", bv * einv)
        BDK = bd(u, "K", k2 * einv)
        BKe = jnp.concatenate([bv * eC, k2 * eC], axis=0).astype(BF16)
        BDV = bd(u, "V", u["v"])
        lhs = jnp.concatenate([av * jnp.exp(u["Gx"]), u["r"] * jnp.exp(u["G"])],
                              axis=0).astype(BF16)
        pc_col = jnp.transpose(jnp.broadcast_to(jnp.exp(u["GC"]), (LANE, QW)))
        PCc = jnp.concatenate([pc_col] * (QW // LANE), axis=1)
        yield
        SB = _mm_nt(lhs, BDB)
        yield
        SK = _mm_nt(lhs, BDK)
        yield
        N = jnp.where(strict, SB[:C], 0.0)
        Lrb = jnp.where(incl, SB[C:], 0.0)
        P = eye_q + N
        Np = _mm(N, bd(u, "N", N))
        yield
        akl = jnp.concatenate([jnp.where(strict, SK[:C], 0.0),
                               jnp.where(incl, SK[C:], 0.0)], axis=0)
        both = _mm(akl, BDV)
        yield
        for _ in range(4):
            out = _mm(jnp.concatenate([P, Np], axis=0), bd(u, "N", Np))
            P = P + out[:C]
            Np = out[C:]
            yield
        P = P + _mm(P, bd(u, "N", Np))
        yield
        H = state[u["ui"]]
        zy = _mm(lhs, H)
        yield
        U = _mm(P, bd(u, "Z", zy[:C] + both[:C]))
        yield
        u["Y"] = zy[C:] + _mm(Lrb, bd(u, "U", U)) + both[C:]
        yield
        upd = _mm_tn(BKe, jnp.concatenate([U, u["v"]], axis=0))
        state[u["ui"]] = H * PCc + jnp.where(bd_mask, upd, 0.0)

    chains = [unit_chain(u) for u in units]
    live = list(range(len(chains)))
    while live:
        for i in list(live):
            if next(chains[i], "done") == "done":
                live.remove(i)
        yield

    for u, s in zip(units, segsum_all([u["Y"] for u in units])):
        u["yc"] = u["Y"] - s * (1.0 / RWKV_HEAD)
    yield
    for u, s in zip(units, segsum_all([u["yc"] * u["yc"] for u in units])):
        u["var"] = s * (1.0 / RWKV_HEAD)
    yield
    for u, s in zip(units, segsum_all([u["rkk"] for u in units])):
        u["bonus"] = s
    yield
    for u in units:
        sl = u["sl"]
        yn = u["yc"] * lax.rsqrt(u["var"] + LNX_EPS) * lw_ref[:, sl] + lb_ref[:, sl]
        res = (yn + u["bonus"] * u["v"]) * (u["g"] * _sigmoid(u["g"]))
        o_ref[u["b"], :, sl] = res.astype(o_ref.dtype)


def _hgrn_stages(layer, off, ph_ref, lbp_ref, ng_ref, o_ref, state):
    C = CHUNK
    NB = ph_ref.shape[0]
    W = ng_ref.shape[1]
    D = HGRN_HEAD
    NH = W // D
    SUB = HGRN_SUB
    NBLK = C // SUB
    GRP = 2 * LANE // D

    lp = lbp_ref[...]
    e = jnp.exp(lp - jnp.max(lp, axis=0, keepdims=True))
    lb = jnp.sum(e[:layer + 1], axis=0, keepdims=True) / jnp.sum(e, axis=0, keepdims=True)

    t_i = lax.broadcasted_iota(jnp.int32, (C, C), 0)
    s_i = lax.broadcasted_iota(jnp.int32, (C, C), 1)
    causal = t_i >= s_i
    tril = causal.astype(BF16)

    units = []
    for b in range(NB):
        x = ph_ref[b]
        qv = x[:, off + 0 * W:off + 1 * W]
        fr = x[:, off + 1 * W:off + 2 * W]
        iv = x[:, off + 2 * W:off + 3 * W]
        gate = x[:, off + 3 * W:off + 4 * W]
        f = lb + (1.0 - lb) * _sigmoid(fr)
        kx = 1.0 - f
        G = _cumsum_rows(jnp.log(f), tril)
        GC = G[C - 1:C, :]
        PC = jnp.exp(GC)
        ivb = iv.astype(BF16)
        zero = jnp.zeros((SUB, W), BF16)
        qs, ks, qt_rows, k_rows = [], [], [], []
        for j in range(NBLK):
            lo, hi = j * SUB, (j + 1) * SUB
            qj = qv[lo:hi] * jnp.exp(G[lo:hi] - G[lo - 1:lo, :]) if j else qv[lo:hi] * jnp.exp(G[lo:hi])
            kj = kx[lo:hi] * jnp.exp(G[lo - 1:lo, :] - G[lo:hi]) if j else kx[lo:hi] * jnp.exp(-G[lo:hi])
            if j:
                step = jnp.exp(G[lo - 1:lo, :] - (G[lo - SUB - 1:lo - SUB, :] if j > 1 else 0.0))
                k_rows = [kr * step for kr in k_rows]
                qt_rows.append(qj * jnp.exp(G[lo - 1:lo, :]))
            else:
                qt_rows.append(qj)
            k_rows.append(kj)
            g0 = (j // GRP) * GRP
            qs.append(jnp.concatenate([zero] * (j - g0) + [qj.astype(BF16)] + [zero] * (g0 + GRP - 1 - j),
                                      axis=0))
            ks.append(jnp.concatenate([kr.astype(BF16) for kr in k_rows] + [zero] * (g0 + GRP - 1 - j),
                                      axis=0))
        qt = jnp.concatenate(qt_rows, axis=0).astype(BF16)
        to_end = jnp.exp(GC - G[C - SUB - 1:C - SUB, :])
        kd = jnp.concatenate([(kr * to_end).astype(BF16) for kr in k_rows], axis=0)
        for h in range(NH):
            sl = slice(h * D, (h + 1) * D)
            units.append(dict(b=b, h=h, sl=sl, qt=qt[:, sl], kd=kd[:, sl], iv=ivb[:, sl],
                              PC=PC[:, sl], g=gate[:, sl],
                              qcat=[jnp.concatenate([q[:, sl] for q in qs[g:g + GRP]], axis=1)
                                    for g in range(0, NBLK, GRP)],
                              kcat=[jnp.concatenate([k[:, sl] for k in ks[g:g + GRP]], axis=1)
                                    for g in range(0, NBLK, GRP)]))
        yield

    halves = (units[:len(units) // 2], units[len(units) // 2:])
    for part in halves:
        for u in part:
            u["A"] = [_mm_nt(q, k) for q, k in zip(u["qcat"], u["kcat"])]
        yield
    for part in halves:
        for u in part:
            u["S"] = state[u["b"] * NH + u["h"]]
            u["inter"] = _mm_nt(u["qt"], u["S"])
        yield
    for part in halves:
        for u in part:
            rows = []
            for g, A in enumerate(u["A"]):
                hi = (g + 1) * GRP * SUB
                mask = causal[hi - GRP * SUB:hi, :hi]
                rows.append(_mm(jnp.where(mask, A, 0.0), u["iv"][:hi]))
            u["o"] = jnp.concatenate(rows, axis=0) + u["inter"]
        yield
    for part in halves:
        for u in part:
            state[u["b"] * NH + u["h"]] = u["S"] * u["PC"] + _mm_tn(u["iv"], u["kd"])
        yield
    for part in halves:
        for u in part:
            o = u["o"]
            g = u["g"]
            ms = jnp.mean(o * o, axis=-1, keepdims=True)
            res = o * lax.rsqrt(ms + NORM_EPS) * ng_ref[:, u["sl"]] * (g * _sigmoid(g))
            o_ref[u["b"], :, u["sl"]] = res.astype(o_ref.dtype)
        yield


def _mixers_kernel(layer, off, pm_ref, pl_ref, ph_ref, w0_ref, w2_ref, a0_ref, a2_ref,
                   kk_ref, ka_ref, rk_ref, lw_ref, lb_ref, ones_ref, lbp_ref, ng_ref,
                   or_ref, oh_ref, r_state, bd_ref, h_state):
    @pl.when(pl.program_id(0) == 0)
    def _():
        for ref in (r_state, bd_ref, h_state):
            ref[...] = jnp.zeros_like(ref)

    rw = _rwkv_stages(pm_ref, pl_ref, w0_ref, w2_ref, a0_ref, a2_ref, kk_ref, ka_ref, rk_ref,
                      lw_ref, lb_ref, ones_ref, or_ref, r_state, bd_ref)
    hg = _hgrn_stages(layer, off, ph_ref, lbp_ref, ng_ref, oh_ref, h_state)
    for stage, _ in enumerate(rw, start=1):
        for _ in range(HGRN_AFTER_RWKV_STAGE.count(stage)):
            next(hg, None)
    for _ in hg:
        pass


def _mixers(p3, main_blk, lora_blk, h_blk_w, h_blk, h_off, layer,
            w0, w2p, a0, a2p, k_k, k_a, r_k, lnx_w, lnx_b, ones_bd, lb_param, hgrn_g):
    B, T, _ = p3.shape
    W = w0.shape[1]
    HW = hgrn_g.shape[1]
    C = CHUNK
    NU = B * (W // QW)
    L = lb_param.shape[0]
    full = lambda shape: pl.BlockSpec(shape, lambda c: (0,) * len(shape))
    return pl.pallas_call(
        functools.partial(_mixers_kernel, layer, h_off),
        grid=(T // C,),
        in_specs=[
            pl.BlockSpec((B, C, 4 * W), lambda c: (0, c, main_blk)),
            pl.BlockSpec((B, C, LORA_BLK), lambda c: (0, c, lora_blk)),
            pl.BlockSpec((B, C, h_blk_w), lambda c: (0, c, h_blk)),
            full((1, W)), full((LANE, W)), full((1, W)), full((LORA_BLK, W)),
            full((1, W)), full((1, W)), full((1, W)), full((1, W)), full((1, W)),
            full((QW, QW)), full((L, HW)), full((1, HW)),
        ],
        out_specs=[pl.BlockSpec((B, C, W), lambda c: (0, c, 0)),
                   pl.BlockSpec((B, C, HW), lambda c: (0, c, 0))],
        out_shape=[jax.ShapeDtypeStruct((B, T, W), BF16),
                   jax.ShapeDtypeStruct((B, T, HW), BF16)],
        scratch_shapes=[
            pltpu.VMEM((NU, QW, QW), F32),
            pltpu.VMEM((NU * len(BD_ROLES), QW, QW), BF16),
            pltpu.VMEM((B * (HW // HGRN_HEAD), HGRN_HEAD, HGRN_HEAD), F32),
        ],
        compiler_params=pltpu.CompilerParams(
            dimension_semantics=("arbitrary",), vmem_limit_bytes=VMEM_LIMIT),
        name="mixers",
    )(p3, p3, p3, w0, w2p, a0, a2p, k_k, k_a, r_k, lnx_w, lnx_b, ones_bd, lb_param, hgrn_g)


def _outproj_kernel(final, yr_ref, yh_ref, x_ref, wr_ref, wh_ref, g_ref, o_ref, wb_ref):
    @pl.when(pl.program_id(0) == 0)
    def _():
        wb_ref[0] = wr_ref[...].astype(BF16)
        wb_ref[1] = wh_ref[...].astype(BF16)

    acc = jnp.dot(yr_ref[...], wb_ref[0], preferred_element_type=F32)
    acc += jnp.dot(yh_ref[...], wb_ref[1], preferred_element_type=F32)
    h = x_ref[...] + acc
    if final:
        ms = jnp.mean(h * h, axis=-1, keepdims=True)
        h = h * lax.rsqrt(ms + NORM_EPS) * g_ref[...]
    o_ref[...] = h


def _outproj(yr, yh, x2, w_all, layer, g, tm, final):
    m, d = x2.shape
    wr_w = yr.shape[1]
    wh_w = yh.shape[1]
    assert wr_w == wh_w and w_all.shape[1:] == (wr_w + wh_w, d)
    once = dict(pipeline_mode=pl.Buffered(1))
    return pl.pallas_call(
        functools.partial(_outproj_kernel, final),
        grid=(m // tm,),
        in_specs=[
            pl.BlockSpec((tm, wr_w), lambda i: (i, 0)),
            pl.BlockSpec((tm, wh_w), lambda i: (i, 0)),
            pl.BlockSpec((tm, d), lambda i: (i, 0)),
            pl.BlockSpec((None, wr_w, d), lambda i: (layer, 0, 0), **once),
            pl.BlockSpec((None, wh_w, d), lambda i: (layer, 1, 0), **once),
            pl.BlockSpec((1, d), lambda i: (0, 0)),
        ],
        out_specs=pl.BlockSpec((tm, d), lambda i: (i, 0)),
        out_shape=jax.ShapeDtypeStruct((m, d), F32),
        scratch_shapes=[pltpu.VMEM((2, wr_w, d), BF16)],
        compiler_params=pltpu.CompilerParams(
            dimension_semantics=("arbitrary",), vmem_limit_bytes=VMEM_LIMIT),
        name="outproj",
    )(yr, yh, x2, w_all, w_all, g)


def kernel(x, norm_g, w_in, mu, w0, w2, a0, a2, k_k, k_a, r_k, lnx_w, lnx_b,
           hgrn_norm_g, lb_param, w_out, final_g):
    B, T, D = x.shape
    depth = w_in.shape[0]
    RW = w0.shape[1]
    HW = hgrn_norm_g.shape[1]
    n_r = 4 * RW + 2 * LORA
    tn = 768
    gap = -n_r % LANE
    n_p = w_in.shape[2] + gap
    h_start = (n_r // LANE) * LANE
    h_off = n_r + gap - h_start
    h_blk_w = n_p - h_start
    assert (4 * RW) % LORA_BLK == 0 and 2 * LORA <= LORA_BLK and w2.shape[1] == LORA <= LANE
    assert h_start % h_blk_w == 0 and h_off + 4 * HW <= h_blk_w

    row = lambda z: z.reshape(1, -1).astype(F32)
    zr = lambda rows: jnp.zeros((rows, RW), F32)
    ones_bd = (jnp.arange(QW)[:, None] // RWKV_HEAD == jnp.arange(QW)[None, :] // RWKV_HEAD).astype(BF16)

    w_in_t = jnp.swapaxes(w_in, 1, 2)
    h = x.reshape(B * T, D)
    for l in range(depth):
        mu_full = jnp.concatenate([mu[l].astype(F32), jnp.zeros((n_p - n_r,), F32)]).reshape(1, -1)
        w2p = jnp.concatenate([w2[l].astype(F32), zr(LANE - LORA)], axis=0).astype(BF16)
        a2p = jnp.concatenate([zr(LORA), a2[l].astype(F32), zr(LORA_BLK - 2 * LORA)],
                              axis=0).astype(BF16)

        p = _inproj(h, row(norm_g[l]), w_in_t, l, mu_full, n_r, T, tm=1024, tn=tn, groups=4)
        p3 = p.reshape(B, T, n_p)
        y_r, y_h = _mixers(p3, 0, (4 * RW) // LORA_BLK, h_blk_w, h_start // h_blk_w, h_off, l,
                           row(w0[l]), w2p, row(a0[l]), a2p, row(k_k[l]), row(k_a[l]),
                           row(r_k[l]), row(lnx_w[l]), row(lnx_b[l]), ones_bd,
                           lb_param.astype(F32), row(hgrn_norm_g[l]))
        h = _outproj(y_r.reshape(B * T, RW), y_h.reshape(B * T, HW), h,
                     w_out, l, row(final_g), tm=512, final=(l == depth - 1))
    return h.reshape(B, T, D)
```

```python
import functools
import math

import jax
import jax.numpy as jnp
from jax import lax
from jax.experimental import pallas as pl
from jax.experimental.pallas import tpu as pltpu

F32 = jnp.float32
BF16 = jnp.bfloat16

NORM_EPS = 1e-6
LNX_EPS = 64e-5
RWKV_HEAD = 64
HGRN_HEAD = 128
LORA = 96
LORA_BLK = 256
LANE = 128
CHUNK = 64
HGRN_SUB = 16
QUAD = 4
QW = QUAD * RWKV_HEAD
HGRN_AFTER_RWKV_STAGE = tuple(range(6, 18))
EXP_M05 = math.exp(-0.5)
VMEM_LIMIT = 48 * 1024 * 1024
VMEM_MARGIN = 6 * 1024 * 1024


def _mm(a, b):
    return jnp.dot(a.astype(BF16), b.astype(BF16), preferred_element_type=F32)


def _mm_nt(a, b):
    return lax.dot_general(a.astype(BF16), b.astype(BF16), (((1,), (1,)), ((), ())),
                           preferred_element_type=F32)


def _mm_tn(a, b):
    return lax.dot_general(a.astype(BF16), b.astype(BF16), (((0,), (0,)), ((), ())),
                           preferred_element_type=F32)


def _cumsum_rows(x, tril_bf16):
    hi = x.astype(BF16)
    lo = (x - hi.astype(F32)).astype(BF16)
    dot = functools.partial(jnp.dot, preferred_element_type=F32)
    return dot(tril_bf16, hi) + dot(tril_bf16, lo)


def _sigmoid(x):
    return 0.5 + 0.5 * jnp.tanh(0.5 * x)


def _inproj_kernel(gap_tile, gap_at, gap, seq_len, x_ref, g_ref, w_ref, mu_ref, o_ref, hn_ref, carry_ref):
    q = pl.program_id(0)
    j = pl.program_id(1)
    i = pl.program_id(2)
    tm, tn = o_ref.shape

    @pl.when((q == 0) & (j == 0) & (i == 0))
    def _():
        carry_ref[...] = jnp.zeros_like(carry_ref)

    @pl.when(j == 0)
    def _():
        x = x_ref[...]
        ms = jnp.mean(x * x, axis=-1, keepdims=True)
        hn_ref[i] = (x * lax.rsqrt(ms + NORM_EPS) * g_ref[...]).astype(BF16)

    w = w_ref[0]
    if gap:
        w_gap = jnp.concatenate([w[:gap_at], jnp.zeros((gap, w.shape[1]), w.dtype),
                                 w[gap_at:tn - gap]], axis=0)
        w = jnp.where(j == gap_tile, w_gap, w)
    acc = _mm_nt(hn_ref[i], w)

    first_row = (q * pl.num_programs(2) + i) * tm
    prev_last = jnp.where(first_row % seq_len == 0, 0.0, carry_ref[j])
    carry_ref[j] = acc[tm - 1:tm, :]
    is_row0 = lax.broadcasted_iota(jnp.int32, (tm, 1), 0) == 0
    prev = jnp.where(is_row0, prev_last, pltpu.roll(acc, 1, axis=0))
    o_ref[...] = acc + mu_ref[...] * (prev - acc)


def _inproj(x2, g, wt_all, layer, mu_full, gap_col, seq_len, tm, tn, groups):
    m, d = x2.shape
    n = mu_full.shape[1]
    gap = n - wt_all.shape[1]
    gap_tile, gap_at = divmod(gap_col, tn)
    assert n % tn == 0 and 0 <= gap and gap_at + gap <= tn and gap % 8 == 0 and gap_at % 8 == 0
    w_map = lambda q, j, i: (
        layer, pl.multiple_of(jnp.where(j <= gap_tile, j * tn, j * tn - gap), 8), 0)
    ni = m // (groups * tm)
    assert seq_len % tm == 0 and mu_full.shape == (1, n)
    vmem = ni * tm * d * 2 + 2 * 4 * (tm * d + tn * d + tm * tn) + VMEM_MARGIN
    x_map = lambda q, j, i: (q * ni + jnp.where(j == 0, i, ni - 1), 0)
    return pl.pallas_call(
        functools.partial(_inproj_kernel, gap_tile, gap_at, gap, seq_len),
        grid=(groups, n // tn, ni),
        in_specs=[
            pl.BlockSpec((tm, d), x_map),
            pl.BlockSpec((1, d), lambda q, j, i: (0, 0)),
            pl.BlockSpec((pl.Element(1), pl.Element(tn), pl.Element(d)), w_map),
            pl.BlockSpec((1, tn), lambda q, j, i: (0, j)),
        ],
        out_specs=pl.BlockSpec((tm, tn), lambda q, j, i: (q * ni + i, j)),
        out_shape=jax.ShapeDtypeStruct((m, n), F32),
        scratch_shapes=[pltpu.VMEM((ni, tm, d), BF16), pltpu.VMEM((n // tn, 1, tn), F32)],
        compiler_params=pltpu.CompilerParams(
            dimension_semantics=("arbitrary", "arbitrary", "arbitrary"),
            vmem_limit_bytes=vmem),
        name="inproj",
    )(x2, g, wt_all, mu_full)


def _rwkv_stages(pm_ref, pl_ref, w0_ref, w2_ref, a0_ref, a2_ref,
                 kk_ref, ka_ref, rk_ref, lw_ref, lb_ref, ones_ref,
                 o_ref, state):
    C = CHUNK
    NB = pm_ref.shape[0]
    W = w0_ref.shape[1]
    NQ = W // QW

    t_i = lax.broadcasted_iota(jnp.int32, (C, C), 0)
    s_i = lax.broadcasted_iota(jnp.int32, (C, C), 1)
    tril = (t_i >= s_i).astype(BF16)
    ones_bd = ones_ref[...]
    dot = functools.partial(jnp.dot, preferred_element_type=F32)

    lane = lax.broadcasted_iota(jnp.int32, (C, QW), 1)
    lane_s = lane % RWKV_HEAD
    lane_blk = lane // RWKV_HEAD
    trow = lax.broadcasted_iota(jnp.int32, (C, QW), 0)
    strict = lane_s < trow
    incl = lane_s <= trow
    eye_q = jnp.where(lane_s == trow, 1.0, 0.0).astype(F32)
    bd_mask = (lax.broadcasted_iota(jnp.int32, (QW, QW), 0) // RWKV_HEAD
               == lax.broadcasted_iota(jnp.int32, (QW, QW), 1) // RWKV_HEAD)

    def bd(x):
        xb = x.astype(BF16)
        zero = jnp.zeros_like(xb)
        return jnp.concatenate([jnp.where(lane_blk == h, xb, zero) for h in range(QUAD)], axis=0)

    def segsum_all(xs):
        out = dot(jnp.concatenate([x.astype(BF16) for x in xs], axis=0), ones_bd)
        return [out[i * C:(i + 1) * C] for i in range(len(xs))]

    xl = jnp.concatenate([pl_ref[b] for b in range(NB)], axis=0)
    z = w0_ref[...] + _mm(jnp.tanh(xl[:, :LANE]), w2_ref[...])
    ld_all = -EXP_M05 * _sigmoid(z)
    a_all = _sigmoid(a0_ref[...] + _mm(xl, a2_ref[...]))
    units = []
    for b in range(NB):
        r = pm_ref[b, :, 0 * W:1 * W]
        k = pm_ref[b, :, 1 * W:2 * W]
        v = pm_ref[b, :, 2 * W:3 * W]
        gate = pm_ref[b, :, 3 * W:4 * W]
        ld = ld_all[b * C:(b + 1) * C]
        a = a_all[b * C:(b + 1) * C]
        G = _cumsum_rows(ld, tril)
        Gx = G - ld
        GC = G[C - 1:C, :]
        for q in range(NQ):
            sl = slice(q * QW, (q + 1) * QW)
            units.append(dict(b=b, ui=b * NQ + q, sl=sl, r=r[:, sl], k=k[:, sl], v=v[:, sl],
                              g=gate[:, sl], a=a[:, sl], G=G[:, sl], Gx=Gx[:, sl], GC=GC[:, sl]))
    yield

    for u in units:
        u["kk"] = u["k"] * kk_ref[:, u["sl"]]
    for u, n2 in zip(units, segsum_all([u["kk"] * u["kk"] for u in units])):
        u["n2"] = n2
    yield

    def unit_chain(u):
        sl = u["sl"]
        kkn = u["kk"] * lax.rsqrt(jnp.maximum(u["n2"], 1e-24))
        k2 = u["k"] * (1.0 + (u["a"] - 1.0) * ka_ref[:, sl])
        av = -kkn
        bv = kkn * u["a"]
        einv = jnp.exp(-u["G"])
        eC = jnp.exp(u["GC"]) * einv
        u["rkk"] = u["r"] * k2 * rk_ref[:, sl]
        BDB = bd(bv * einv)
        BDK = bd(k2 * einv)
        BKe = jnp.concatenate([bv * eC, k2 * eC], axis=0).astype(BF16)
        BDV = bd(u["v"])
        lhs = jnp.concatenate([av * jnp.exp(u["Gx"]), u["r"] * jnp.exp(u["G"])],
                              axis=0).astype(BF16)
        pc_col = jnp.transpose(jnp.broadcast_to(jnp.exp(u["GC"]), (LANE, QW)))
        PCc = jnp.concatenate([pc_col] * (QW // LANE), axis=1)
        yield
        SB = _mm_nt(lhs, BDB)
        yield
        SK = _mm_nt(lhs, BDK)
        yield
        N = jnp.where(strict, SB[:C], 0.0)
        Lrb = jnp.where(incl, SB[C:], 0.0)
        P = eye_q + N
        Np = _mm(N, bd(N))
        yield
        akl = jnp.concatenate([jnp.where(strict, SK[:C], 0.0),
                               jnp.where(incl, SK[C:], 0.0)], axis=0)
        both = _mm(akl, BDV)
        yield
        for _ in range(4):
            out = _mm(jnp.concatenate([P, Np], axis=0), bd(Np))
            P = P + out[:C]
            Np = out[C:]
            yield
        P = P + _mm(P, bd(Np))
        yield
        H = state[u["ui"]]
        zy = _mm(lhs, H)
        yield
        U = _mm(P, bd(zy[:C] + both[:C]))
        yield
        u["Y"] = zy[C:] + _mm(Lrb, bd(U)) + both[C:]
        yield
        upd = _mm_tn(BKe, jnp.concatenate([U, u["v"]], axis=0))
        state[u["ui"]] = H * PCc + jnp.where(bd_mask, upd, 0.0)

    chains = [unit_chain(u) for u in units]
    live = list(range(len(chains)))
    while live:
        for i in list(live):
            if next(chains[i], "done") == "done":
                live.remove(i)
        yield

    for u, s in zip(units, segsum_all([u["Y"] for u in units])):
        u["yc"] = u["Y"] - s * (1.0 / RWKV_HEAD)
    yield
    for u, s in zip(units, segsum_all([u["yc"] * u["yc"] for u in units])):
        u["var"] = s * (1.0 / RWKV_HEAD)
    yield
    for u, s in zip(units, segsum_all([u["rkk"] for u in units])):
        u["bonus"] = s
    yield
    for u in units:
        sl = u["sl"]
        yn = u["yc"] * lax.rsqrt(u["var"] + LNX_EPS) * lw_ref[:, sl] + lb_ref[:, sl]
        res = (yn + u["bonus"] * u["v"]) * (u["g"] * _sigmoid(u["g"]))
        o_ref[u["b"], :, sl] = res.astype(o_ref.dtype)


def _hgrn_stages(layer, off, ph_ref, lbp_ref, ng_ref, o_ref, state):
    C = CHUNK
    NB = ph_ref.shape[0]
    W = ng_ref.shape[1]
    D = HGRN_HEAD
    NH = W // D
    SUB = HGRN_SUB
    NBLK = C // SUB
    GRP = 2 * LANE // D

    lp = lbp_ref[...]
    e = jnp.exp(lp - jnp.max(lp, axis=0, keepdims=True))
    lb = jnp.sum(e[:layer + 1], axis=0, keepdims=True) / jnp.sum(e, axis=0, keepdims=True)

    t_i = lax.broadcasted_iota(jnp.int32, (C, C), 0)
    s_i = lax.broadcasted_iota(jnp.int32, (C, C), 1)
    causal = t_i >= s_i
    tril = causal.astype(BF16)

    units = []
    for b in range(NB):
        x = ph_ref[b]
        qv = x[:, off + 0 * W:off + 1 * W]
        fr = x[:, off + 1 * W:off + 2 * W]
        iv = x[:, off + 2 * W:off + 3 * W]
        gate = x[:, off + 3 * W:off + 4 * W]
        f = lb + (1.0 - lb) * _sigmoid(fr)
        kx = 1.0 - f
        G = _cumsum_rows(jnp.log(f), tril)
        GC = G[C - 1:C, :]
        PC = jnp.exp(GC)
        ivb = iv.astype(BF16)
        zero = jnp.zeros((SUB, W), BF16)
        qs, ks, qt_rows, k_rows = [], [], [], []
        for j in range(NBLK):
            lo, hi = j * SUB, (j + 1) * SUB
            qj = qv[lo:hi] * jnp.exp(G[lo:hi] - G[lo - 1:lo, :]) if j else qv[lo:hi] * jnp.exp(G[lo:hi])
            kj = kx[lo:hi] * jnp.exp(G[lo - 1:lo, :] - G[lo:hi]) if j else kx[lo:hi] * jnp.exp(-G[lo:hi])
            if j:
                step = jnp.exp(G[lo - 1:lo, :] - (G[lo - SUB - 1:lo - SUB, :] if j > 1 else 0.0))
                k_rows = [kr * step for kr in k_rows]
                qt_rows.append(qj * jnp.exp(G[lo - 1:lo, :]))
            else:
                qt_rows.append(qj)
            k_rows.append(kj)
            g0 = (j // GRP) * GRP
            qs.append(jnp.concatenate([zero] * (j - g0) + [qj.astype(BF16)] + [zero] * (g0 + GRP - 1 - j),
                                      axis=0))
            ks.append(jnp.concatenate([kr.astype(BF16) for kr in k_rows] + [zero] * (g0 + GRP - 1 - j),
                                      axis=0))
        qt = jnp.concatenate(qt_rows, axis=0).astype(BF16)
        to_end = jnp.exp(GC - G[C - SUB - 1:C - SUB, :])
        kd = jnp.concatenate([(kr * to_end).astype(BF16) for kr in k_rows], axis=0)
        for h in range(NH):
            sl = slice(h * D, (h + 1) * D)
            units.append(dict(b=b, h=h, sl=sl, qt=qt[:, sl], kd=kd[:, sl], iv=ivb[:, sl],
                              PC=PC[:, sl], g=gate[:, sl],
                              qcat=[jnp.concatenate([q[:, sl] for q in qs[g:g + GRP]], axis=1)
                                    for g in range(0, NBLK, GRP)],
                              kcat=[jnp.concatenate([k[:, sl] for k in ks[g:g + GRP]], axis=1)
                                    for g in range(0, NBLK, GRP)]))
        yield

    halves = (units[:len(units) // 2], units[len(units) // 2:])
    for part in halves:
        for u in part:
            u["A"] = [_mm_nt(q, k) for q, k in zip(u["qcat"], u["kcat"])]
        yield
    for part in halves:
        for u in part:
            u["S"] = state[u["b"] * NH + u["h"]]
            u["inter"] = _mm_nt(u["qt"], u["S"])
        yield
    for part in halves:
        for u in part:
            rows = []
            for g, A in enumerate(u["A"]):
                hi = (g + 1) * GRP * SUB
                mask = causal[hi - GRP * SUB:hi, :hi]
                rows.append(_mm(jnp.where(mask, A, 0.0), u["iv"][:hi]))
            u["o"] = jnp.concatenate(rows, axis=0) + u["inter"]
        yield
    for part in halves:
        for u in part:
            state[u["b"] * NH + u["h"]] = u["S"] * u["PC"] + _mm_tn(u["iv"], u["kd"])
        yield
    for part in halves:
        for u in part:
            o = u["o"]
            g = u["g"]
            ms = jnp.mean(o * o, axis=-1, keepdims=True)
            res = o * lax.rsqrt(ms + NORM_EPS) * ng_ref[:, u["sl"]] * (g * _sigmoid(g))
            o_ref[u["b"], :, u["sl"]] = res.astype(o_ref.dtype)
        yield


def _mixers_kernel(layer, off, pm_ref, pl_ref, ph_ref, w0_ref, w2_ref, a0_ref, a2_ref,
                   kk_ref, ka_ref, rk_ref, lw_ref, lb_ref, ones_ref, lbp_ref, ng_ref,
                   or_ref, oh_ref, r_state, h_state):
    @pl.when(pl.program_id(0) == 0)
    def _():
        for ref in (r_state, h_state):
            ref[...] = jnp.zeros_like(ref)

    rw = _rwkv_stages(pm_ref, pl_ref, w0_ref, w2_ref, a0_ref, a2_ref, kk_ref, ka_ref, rk_ref,
                      lw_ref, lb_ref, ones_ref, or_ref, r_state)
    hg = _hgrn_stages(layer, off, ph_ref, lbp_ref, ng_ref, oh_ref, h_state)
    for stage, _ in enumerate(rw, start=1):
        for _ in range(HGRN_AFTER_RWKV_STAGE.count(stage)):
            next(hg, None)
    for _ in hg:
        pass


def _mixers(p3, main_blk, lora_blk, h_blk_w, h_blk, h_off, layer,
            w0, w2p, a0, a2p, k_k, k_a, r_k, lnx_w, lnx_b, ones_bd, lb_param, hgrn_g):
    B, T, _ = p3.shape
    W = w0.shape[1]
    HW = hgrn_g.shape[1]
    C = CHUNK
    NU = B * (W // QW)
    L = lb_param.shape[0]
    full = lambda shape: pl.BlockSpec(shape, lambda c: (0,) * len(shape))
    return pl.pallas_call(
        functools.partial(_mixers_kernel, layer, h_off),
        grid=(T // C,),
        in_specs=[
            pl.BlockSpec((B, C, 4 * W), lambda c: (0, c, main_blk)),
            pl.BlockSpec((B, C, LORA_BLK), lambda c: (0, c, lora_blk)),
            pl.BlockSpec((B, C, h_blk_w), lambda c: (0, c, h_blk)),
            full((1, W)), full((LANE, W)), full((1, W)), full((LORA_BLK, W)),
            full((1, W)), full((1, W)), full((1, W)), full((1, W)), full((1, W)),
            full((QW, QW)), full((L, HW)), full((1, HW)),
        ],
        out_specs=[pl.BlockSpec((B, C, W), lambda c: (0, c, 0)),
                   pl.BlockSpec((B, C, HW), lambda c: (0, c, 0))],
        out_shape=[jax.ShapeDtypeStruct((B, T, W), BF16),
                   jax.ShapeDtypeStruct((B, T, HW), BF16)],
        scratch_shapes=[
            pltpu.VMEM((NU, QW, QW), F32),
            pltpu.VMEM((B * (HW // HGRN_HEAD), HGRN_HEAD, HGRN_HEAD), F32),
        ],
        compiler_params=pltpu.CompilerParams(
            dimension_semantics=("arbitrary",), vmem_limit_bytes=VMEM_LIMIT),
        name="mixers",
    )(p3, p3, p3, w0, w2p, a0, a2p, k_k, k_a, r_k, lnx_w, lnx_b, ones_bd, lb_param, hgrn_g)


def _outproj_kernel(final, yr_ref, yh_ref, x_ref, wr_ref, wh_ref, g_ref, o_ref, wb_ref):
    @pl.when(pl.program_id(0) == 0)
    def _():
        wb_ref[0] = wr_ref[...].astype(BF16)
        wb_ref[1] = wh_ref[...].astype(BF16)

    acc = jnp.dot(yr_ref[...], wb_ref[0], preferred_element_type=F32)
    acc += jnp.dot(yh_ref[...], wb_ref[1], preferred_element_type=F32)
    h = x_ref[...] + acc
    if final:
        ms = jnp.mean(h * h, axis=-1, keepdims=True)
        h = h * lax.rsqrt(ms + NORM_EPS) * g_ref[...]
    o_ref[...] = h


def _outproj(yr, yh, x2, w_all, layer, g, tm, final):
    m, d = x2.shape
    wr_w = yr.shape[1]
    wh_w = yh.shape[1]
    assert wr_w == wh_w and w_all.shape[1:] == (wr_w + wh_w, d)
    once = dict(pipeline_mode=pl.Buffered(1))
    return pl.pallas_call(
        functools.partial(_outproj_kernel, final),
        grid=(m // tm,),
        in_specs=[
            pl.BlockSpec((tm, wr_w), lambda i: (i, 0)),
            pl.BlockSpec((tm, wh_w), lambda i: (i, 0)),
            pl.BlockSpec((tm, d), lambda i: (i, 0)),
            pl.BlockSpec((None, wr_w, d), lambda i: (layer, 0, 0), **once),
            pl.BlockSpec((None, wh_w, d), lambda i: (layer, 1, 0), **once),
            pl.BlockSpec((1, d), lambda i: (0, 0)),
        ],
        out_specs=pl.BlockSpec((tm, d), lambda i: (i, 0)),
        out_shape=jax.ShapeDtypeStruct((m, d), F32),
        scratch_shapes=[pltpu.VMEM((2, wr_w, d), BF16)],
        compiler_params=pltpu.CompilerParams(
            dimension_semantics=("arbitrary",), vmem_limit_bytes=VMEM_LIMIT),
        name="outproj",
    )(yr, yh, x2, w_all, w_all, g)


def kernel(x, norm_g, w_in, mu, w0, w2, a0, a2, k_k, k_a, r_k, lnx_w, lnx_b,
           hgrn_norm_g, lb_param, w_out, final_g):
    B, T, D = x.shape
    depth = w_in.shape[0]
    RW = w0.shape[1]
    HW = hgrn_norm_g.shape[1]
    n_r = 4 * RW + 2 * LORA
    tn = 768
    gap = -n_r % LANE
    n_p = w_in.shape[2] + gap
    h_start = (n_r // LANE) * LANE
    h_off = n_r + gap - h_start
    h_blk_w = n_p - h_start
    assert (4 * RW) % LORA_BLK == 0 and 2 * LORA <= LORA_BLK and w2.shape[1] == LORA <= LANE
    assert h_start % h_blk_w == 0 and h_off + 4 * HW <= h_blk_w

    row = lambda z: z.reshape(1, -1).astype(F32)
    zr = lambda rows: jnp.zeros((rows, RW), F32)
    ones_bd = (jnp.arange(QW)[:, None] // RWKV_HEAD == jnp.arange(QW)[None, :] // RWKV_HEAD).astype(BF16)

    w_in_t = jnp.swapaxes(w_in, 1, 2)
    h = x.reshape(B * T, D)
    for l in range(depth):
        mu_full = jnp.concatenate([mu[l].astype(F32), jnp.zeros((n_p - n_r,), F32)]).reshape(1, -1)
        w2p = jnp.concatenate([w2[l].astype(F32), zr(LANE - LORA)], axis=0).astype(BF16)
        a2p = jnp.concatenate([zr(LORA), a2[l].astype(F32), zr(LORA_BLK - 2 * LORA)],
                              axis=0).astype(BF16)

        p = _inproj(h, row(norm_g[l]), w_in_t, l, mu_full, n_r, T, tm=1024, tn=tn, groups=4)
        p3 = p.reshape(B, T, n_p)
        y_r, y_h = _mixers(p3, 0, (4 * RW) // LORA_BLK, h_blk_w, h_start // h_blk_w, h_off, l,
                           row(w0[l]), w2p, row(a0[l]), a2p, row(k_k[l]), row(k_a[l]),
                           row(r_k[l]), row(lnx_w[l]), row(lnx_b[l]), ones_bd,
                           lb_param.astype(F32), row(hgrn_norm_g[l]))
        h = _outproj(y_r.reshape(B * T, RW), y_h.reshape(B * T, HW), h,
                     w_out, l, row(final_g), tm=512, final=(l == depth - 1))
    return h.reshape(B, T, D)
```

```python
import functools
import math

import jax
import jax.numpy as jnp
from jax import lax
from jax.experimental import pallas as pl
from jax.experimental.pallas import tpu as pltpu

F32 = jnp.float32
BF16 = jnp.bfloat16

NORM_EPS = 1e-6
LNX_EPS = 64e-5
RWKV_HEAD = 64
HGRN_HEAD = 128
LORA = 96
LORA_BLK = 256
LANE = 128
CHUNK = 64
HGRN_SUB = 16
QUAD = 4
QW = QUAD * RWKV_HEAD
HGRN_AFTER_RWKV_STAGE = tuple(range(6, 18))
EXP_M05 = math.exp(-0.5)
VMEM_LIMIT = 48 * 1024 * 1024
VMEM_MARGIN = 6 * 1024 * 1024


def _mm(a, b):
    return jnp.dot(a.astype(BF16), b.astype(BF16), preferred_element_type=F32)


def _mm_nt(a, b):
    return lax.dot_general(a.astype(BF16), b.astype(BF16), (((1,), (1,)), ((), ())),
                           preferred_element_type=F32)


def _mm_tn(a, b):
    return lax.dot_general(a.astype(BF16), b.astype(BF16), (((0,), (0,)), ((), ())),
                           preferred_element_type=F32)


def _cumsum_rows(x, tril_bf16):
    hi = x.astype(BF16)
    lo = (x - hi.astype(F32)).astype(BF16)
    dot = functools.partial(jnp.dot, preferred_element_type=F32)
    return dot(tril_bf16, hi) + dot(tril_bf16, lo)


def _sigmoid(x):
    return 0.5 + 0.5 * jnp.tanh(0.5 * x)


def _inproj_kernel(gap_tile, gap_at, gap, seq_len, x_ref, g_ref, w_ref, mu_ref, o_ref, hn_ref, carry_ref):
    q = pl.program_id(0)
    j = pl.program_id(1)
    i = pl.program_id(2)
    tm, tn = o_ref.shape

    @pl.when((q == 0) & (j == 0) & (i == 0))
    def _():
        carry_ref[...] = jnp.zeros_like(carry_ref)

    @pl.when(j == 0)
    def _():
        x = x_ref[...]
        ms = jnp.mean(x * x, axis=-1, keepdims=True)
        hn_ref[i] = (x * lax.rsqrt(ms + NORM_EPS) * g_ref[...]).astype(BF16)

    w = w_ref[0]
    if gap:
        w_gap = jnp.concatenate([w[:gap_at], jnp.zeros((gap, w.shape[1]), w.dtype),
                                 w[gap_at:tn - gap]], axis=0)
        w = jnp.where(j == gap_tile, w_gap, w)
    acc = _mm_nt(hn_ref[i], w)

    first_row = (q * pl.num_programs(2) + i) * tm
    prev_last = jnp.where(first_row % seq_len == 0, 0.0, carry_ref[j])
    carry_ref[j] = acc[tm - 1:tm, :]
    is_row0 = lax.broadcasted_iota(jnp.int32, (tm, 1), 0) == 0
    prev = jnp.where(is_row0, prev_last, pltpu.roll(acc, 1, axis=0))
    o_ref[...] = acc + mu_ref[...] * (prev - acc)


def _inproj(x2, g, wt_all, layer, mu_full, gap_col, seq_len, tm, tn, groups):
    m, d = x2.shape
    n = mu_full.shape[1]
    gap = n - wt_all.shape[1]
    gap_tile, gap_at = divmod(gap_col, tn)
    assert n % tn == 0 and 0 <= gap and gap_at + gap <= tn and gap % 8 == 0 and gap_at % 8 == 0
    w_map = lambda q, j, i: (
        layer, pl.multiple_of(jnp.where(j <= gap_tile, j * tn, j * tn - gap), 8), 0)
    ni = m // (groups * tm)
    assert seq_len % tm == 0 and mu_full.shape == (1, n)
    vmem = ni * tm * d * 2 + 2 * 4 * (tm * d + tn * d + tm * tn) + VMEM_MARGIN
    x_map = lambda q, j, i: (q * ni + jnp.where(j == 0, i, ni - 1), 0)
    return pl.pallas_call(
        functools.partial(_inproj_kernel, gap_tile, gap_at, gap, seq_len),
        grid=(groups, n // tn, ni),
        in_specs=[
            pl.BlockSpec((tm, d), x_map),
            pl.BlockSpec((1, d), lambda q, j, i: (0, 0)),
            pl.BlockSpec((pl.Element(1), pl.Element(tn), pl.Element(d)), w_map),
            pl.BlockSpec((1, tn), lambda q, j, i: (0, j)),
        ],
        out_specs=pl.BlockSpec((tm, tn), lambda q, j, i: (q * ni + i, j)),
        out_shape=jax.ShapeDtypeStruct((m, n), F32),
        scratch_shapes=[pltpu.VMEM((ni, tm, d), BF16), pltpu.VMEM((n // tn, 1, tn), F32)],
        compiler_params=pltpu.CompilerParams(
            dimension_semantics=("arbitrary", "arbitrary", "arbitrary"),
            vmem_limit_bytes=vmem),
        name="inproj",
    )(x2, g, wt_all, mu_full)


def _rwkv_stages(pm_ref, pl_ref, w0_ref, w2_ref, a0_ref, a2_ref,
                 kk_ref, ka_ref, rk_ref, lw_ref, lb_ref, ones_ref,
                 o_ref, state):
    C = CHUNK
    NB = pm_ref.shape[0]
    W = w0_ref.shape[1]
    NQ = W // QW

    t_i = lax.broadcasted_iota(jnp.int32, (C, C), 0)
    s_i = lax.broadcasted_iota(jnp.int32, (C, C), 1)
    tril = (t_i >= s_i).astype(BF16)
    ones_bd = ones_ref[...]
    dot = functools.partial(jnp.dot, preferred_element_type=F32)

    lane = lax.broadcasted_iota(jnp.int32, (C, QW), 1)
    lane_s = lane % RWKV_HEAD
    lane_blk = lane // RWKV_HEAD
    trow = lax.broadcasted_iota(jnp.int32, (C, QW), 0)
    strict = lane_s < trow
    incl = lane_s <= trow
    eye_q = jnp.where(lane_s == trow, 1.0, 0.0).astype(F32)
    bd_mask = (lax.broadcasted_iota(jnp.int32, (QW, QW), 0) // RWKV_HEAD
               == lax.broadcasted_iota(jnp.int32, (QW, QW), 1) // RWKV_HEAD)

    def bd(x):
        xb = x.astype(BF16)
        zero = jnp.zeros_like(xb)
        return jnp.concatenate([jnp.where(lane_blk == h, xb, zero) for h in range(QUAD)], axis=0)

    def bd_t(x):
        z = jnp.zeros((LANE, LANE), x.dtype)
        t0 = jnp.transpose(x[:LANE, :LANE])
        t1 = jnp.transpose(x[LANE:, LANE:])
        return jnp.concatenate([jnp.concatenate([t0, z], axis=1),
                                jnp.concatenate([z, t1], axis=1)], axis=0)

    def segsum_all(xs):
        out = dot(jnp.concatenate([x.astype(BF16) for x in xs], axis=0), ones_bd)
        return [out[i * C:(i + 1) * C] for i in range(len(xs))]

    xl = jnp.concatenate([pl_ref[b] for b in range(NB)], axis=0)
    z = w0_ref[...] + _mm(jnp.tanh(xl[:, :LANE]), w2_ref[...])
    ld_all = -EXP_M05 * _sigmoid(z)
    a_all = _sigmoid(a0_ref[...] + _mm(xl, a2_ref[...]))
    units = []
    for b in range(NB):
        r = pm_ref[b, :, 0 * W:1 * W]
        k = pm_ref[b, :, 1 * W:2 * W]
        v = pm_ref[b, :, 2 * W:3 * W]
        gate = pm_ref[b, :, 3 * W:4 * W]
        ld = ld_all[b * C:(b + 1) * C]
        a = a_all[b * C:(b + 1) * C]
        G = _cumsum_rows(ld, tril)
        Gx = G - ld
        GC = G[C - 1:C, :]
        for q in range(NQ):
            sl = slice(q * QW, (q + 1) * QW)
            units.append(dict(b=b, ui=b * NQ + q, sl=sl, r=r[:, sl], k=k[:, sl], v=v[:, sl],
                              g=gate[:, sl], a=a[:, sl], G=G[:, sl], Gx=Gx[:, sl], GC=GC[:, sl]))
    yield

    for u in units:
        u["kk"] = u["k"] * kk_ref[:, u["sl"]]
    for u, n2 in zip(units, segsum_all([u["kk"] * u["kk"] for u in units])):
        u["n2"] = n2
    yield

    def unit_chain(u):
        sl = u["sl"]
        kkn = u["kk"] * lax.rsqrt(jnp.maximum(u["n2"], 1e-24))
        k2 = u["k"] * (1.0 + (u["a"] - 1.0) * ka_ref[:, sl])
        av = -kkn
        bv = kkn * u["a"]
        einv = jnp.exp(-u["G"])
        eC = jnp.exp(u["GC"]) * einv
        u["rkk"] = u["r"] * k2 * rk_ref[:, sl]
        BDB = bd_t(bd(bv * einv))
        BDK = bd_t(bd(k2 * einv))
        BKe = jnp.concatenate([bv * eC, k2 * eC], axis=0).astype(BF16)
        BDV = bd(u["v"])
        lhs = jnp.concatenate([av * jnp.exp(u["Gx"]), u["r"] * jnp.exp(u["G"])],
                              axis=0).astype(BF16)
        pc_col = jnp.transpose(jnp.broadcast_to(jnp.exp(u["GC"]), (LANE, QW)))
        PCc = jnp.concatenate([pc_col] * (QW // LANE), axis=1)
        yield
        SB = _mm(lhs, BDB)
        yield
        SK = _mm(lhs, BDK)
        yield
        N = jnp.where(strict, SB[:C], 0.0)
        Lrb = jnp.where(incl, SB[C:], 0.0)
        P = eye_q + N
        Np = _mm(N, bd(N))
        yield
        akl = jnp.concatenate([jnp.where(strict, SK[:C], 0.0),
                               jnp.where(incl, SK[C:], 0.0)], axis=0)
        both = _mm(akl, BDV)
        yield
        for _ in range(4):
            out = _mm(jnp.concatenate([P, Np], axis=0), bd(Np))
            P = P + out[:C]
            Np = out[C:]
            yield
        P = P + _mm(P, bd(Np))
        yield
        H = state[u["ui"]]
        zy = _mm(lhs, H)
        yield
        U = _mm(P, bd(zy[:C] + both[:C]))
        yield
        u["Y"] = zy[C:] + _mm(Lrb, bd(U)) + both[C:]
        yield
        upd = _mm_tn(BKe, jnp.concatenate([U, u["v"]], axis=0))
        state[u["ui"]] = H * PCc + jnp.where(bd_mask, upd, 0.0)

    chains = [unit_chain(u) for u in units]
    live = list(range(len(chains)))
    while live:
        for i in list(live):
            if next(chains[i], "done") == "done":
                live.remove(i)
        yield

    for u, s in zip(units, segsum_all([u["Y"] for u in units])):
        u["yc"] = u["Y"] - s * (1.0 / RWKV_HEAD)
    yield
    for u, s in zip(units, segsum_all([u["yc"] * u["yc"] for u in units])):
        u["var"] = s * (1.0 / RWKV_HEAD)
    yield
    for u, s in zip(units, segsum_all([u["rkk"] for u in units])):
        u["bonus"] = s
    yield
    for u in units:
        sl = u["sl"]
        yn = u["yc"] * lax.rsqrt(u["var"] + LNX_EPS) * lw_ref[:, sl] + lb_ref[:, sl]
        res = (yn + u["bonus"] * u["v"]) * (u["g"] * _sigmoid(u["g"]))
        o_ref[u["b"], :, sl] = res.astype(o_ref.dtype)


def _hgrn_stages(layer, off, ph_ref, lbp_ref, ng_ref, o_ref, state):
    C = CHUNK
    NB = ph_ref.shape[0]
    W = ng_ref.shape[1]
    D = HGRN_HEAD
    NH = W // D
    SUB = HGRN_SUB
    NBLK = C // SUB
    GRP = 2 * LANE // D

    lp = lbp_ref[...]
    e = jnp.exp(lp - jnp.max(lp, axis=0, keepdims=True))
    lb = jnp.sum(e[:layer + 1], axis=0, keepdims=True) / jnp.sum(e, axis=0, keepdims=True)

    t_i = lax.broadcasted_iota(jnp.int32, (C, C), 0)
    s_i = lax.broadcasted_iota(jnp.int32, (C, C), 1)
    causal = t_i >= s_i
    tril = causal.astype(BF16)

    units = []
    for b in range(NB):
        x = ph_ref[b]
        qv = x[:, off + 0 * W:off + 1 * W]
        fr = x[:, off + 1 * W:off + 2 * W]
        iv = x[:, off + 2 * W:off + 3 * W]
        gate = x[:, off + 3 * W:off + 4 * W]
        f = lb + (1.0 - lb) * _sigmoid(fr)
        kx = 1.0 - f
        G = _cumsum_rows(jnp.log(f), tril)
        GC = G[C - 1:C, :]
        PC = jnp.exp(GC)
        ivb = iv.astype(BF16)
        zero = jnp.zeros((SUB, W), BF16)
        qs, ks, qt_rows, k_rows = [], [], [], []
        for j in range(NBLK):
            lo, hi = j * SUB, (j + 1) * SUB
            qj = qv[lo:hi] * jnp.exp(G[lo:hi] - G[lo - 1:lo, :]) if j else qv[lo:hi] * jnp.exp(G[lo:hi])
            kj = kx[lo:hi] * jnp.exp(G[lo - 1:lo, :] - G[lo:hi]) if j else kx[lo:hi] * jnp.exp(-G[lo:hi])
            if j:
                step = jnp.exp(G[lo - 1:lo, :] - (G[lo - SUB - 1:lo - SUB, :] if j > 1 else 0.0))
                k_rows = [kr * step for kr in k_rows]
                qt_rows.append(qj * jnp.exp(G[lo - 1:lo, :]))
            else:
                qt_rows.append(qj)
            k_rows.append(kj)
            g0 = (j // GRP) * GRP
            qs.append(jnp.concatenate([zero] * (j - g0) + [qj.astype(BF16)] + [zero] * (g0 + GRP - 1 - j),
                                      axis=0))
            ks.append(jnp.concatenate([kr.astype(BF16) for kr in k_rows] + [zero] * (g0 + GRP - 1 - j),
                                      axis=0))
        qt = jnp.concatenate(qt_rows, axis=0).astype(BF16)
        to_end = jnp.exp(GC - G[C - SUB - 1:C - SUB, :])
        kd = jnp.concatenate([(kr * to_end).astype(BF16) for kr in k_rows], axis=0)
        for h in range(NH):
            sl = slice(h * D, (h + 1) * D)
            units.append(dict(b=b, h=h, sl=sl, qt=qt[:, sl], kd=kd[:, sl], iv=ivb[:, sl],
                              PC=PC[:, sl], g=gate[:, sl],
                              qcat=[jnp.concatenate([q[:, sl] for q in qs[g:g + GRP]], axis=1)
                                    for g in range(0, NBLK, GRP)],
                              kcat=[jnp.concatenate([k[:, sl] for k in ks[g:g + GRP]], axis=1)
                                    for g in range(0, NBLK, GRP)]))
        yield

    halves = (units[:len(units) // 2], units[len(units) // 2:])
    for part in halves:
        for u in part:
            u["A"] = [_mm_nt(q, k) for q, k in zip(u["qcat"], u["kcat"])]
        yield
    for part in halves:
        for u in part:
            u["S"] = state[u["b"] * NH + u["h"]]
            u["inter"] = _mm_nt(u["qt"], u["S"])
        yield
    for part in halves:
        for u in part:
            rows = []
            for g, A in enumerate(u["A"]):
                hi = (g + 1) * GRP * SUB
                mask = causal[hi - GRP * SUB:hi, :hi]
                rows.append(_mm(jnp.where(mask, A, 0.0), u["iv"][:hi]))
            u["o"] = jnp.concatenate(rows, axis=0) + u["inter"]
        yield
    for part in halves:
        for u in part:
            state[u["b"] * NH + u["h"]] = u["S"] * u["PC"] + _mm_tn(u["iv"], u["kd"])
        yield
    for part in halves:
        for u in part:
            o = u["o"]
            g = u["g"]
            ms = jnp.mean(o * o, axis=-1, keepdims=True)
            res = o * lax.rsqrt(ms + NORM_EPS) * ng_ref[:, u["sl"]] * (g * _sigmoid(g))
            o_ref[u["b"], :, u["sl"]] = res.astype(o_ref.dtype)
        yield


def _mixers_kernel(layer, off, pm_ref, pl_ref, ph_ref, w0_ref, w2_ref, a0_ref, a2_ref,
                   kk_ref, ka_ref, rk_ref, lw_ref, lb_ref, ones_ref, lbp_ref, ng_ref,
                   or_ref, oh_ref, r_state, h_state):
    @pl.when(pl.program_id(0) == 0)
    def _():
        for ref in (r_state, h_state):
            ref[...] = jnp.zeros_like(ref)

    rw = _rwkv_stages(pm_ref, pl_ref, w0_ref, w2_ref, a0_ref, a2_ref, kk_ref, ka_ref, rk_ref,
                      lw_ref, lb_ref, ones_ref, or_ref, r_state)
    hg = _hgrn_stages(layer, off, ph_ref, lbp_ref, ng_ref, oh_ref, h_state)
    for stage, _ in enumerate(rw, start=1):
        for _ in range(HGRN_AFTER_RWKV_STAGE.count(stage)):
            next(hg, None)
    for _ in hg:
        pass


def _mixers(p3, main_blk, lora_blk, h_blk_w, h_blk, h_off, layer,
            w0, w2p, a0, a2p, k_k, k_a, r_k, lnx_w, lnx_b, ones_bd, lb_param, hgrn_g):
    B, T, _ = p3.shape
    W = w0.shape[1]
    HW = hgrn_g.shape[1]
    C = CHUNK
    NU = B * (W // QW)
    L = lb_param.shape[0]
    full = lambda shape: pl.BlockSpec(shape, lambda c: (0,) * len(shape))
    return pl.pallas_call(
        functools.partial(_mixers_kernel, layer, h_off),
        grid=(T // C,),
        in_specs=[
            pl.BlockSpec((B, C, 4 * W), lambda c: (0, c, main_blk)),
            pl.BlockSpec((B, C, LORA_BLK), lambda c: (0, c, lora_blk)),
            pl.BlockSpec((B, C, h_blk_w), lambda c: (0, c, h_blk)),
            full((1, W)), full((LANE, W)), full((1, W)), full((LORA_BLK, W)),
            full((1, W)), full((1, W)), full((1, W)), full((1, W)), full((1, W)),
            full((QW, QW)), full((L, HW)), full((1, HW)),
        ],
        out_specs=[pl.BlockSpec((B, C, W), lambda c: (0, c, 0)),
                   pl.BlockSpec((B, C, HW), lambda c: (0, c, 0))],
        out_shape=[jax.ShapeDtypeStruct((B, T, W), BF16),
                   jax.ShapeDtypeStruct((B, T, HW), BF16)],
        scratch_shapes=[
            pltpu.VMEM((NU, QW, QW), F32),
            pltpu.VMEM((B * (HW // HGRN_HEAD), HGRN_HEAD, HGRN_HEAD), F32),
        ],
        compiler_params=pltpu.CompilerParams(
            dimension_semantics=("arbitrary",), vmem_limit_bytes=VMEM_LIMIT),
        name="mixers",
    )(p3, p3, p3, w0, w2p, a0, a2p, k_k, k_a, r_k, lnx_w, lnx_b, ones_bd, lb_param, hgrn_g)


def _outproj_kernel(final, yr_ref, yh_ref, x_ref, wr_ref, wh_ref, g_ref, o_ref, wb_ref):
    @pl.when(pl.program_id(0) == 0)
    def _():
        wb_ref[0] = wr_ref[...].astype(BF16)
        wb_ref[1] = wh_ref[...].astype(BF16)

    acc = jnp.dot(yr_ref[...], wb_ref[0], preferred_element_type=F32)
    acc += jnp.dot(yh_ref[...], wb_ref[1], preferred_element_type=F32)
    h = x_ref[...] + acc
    if final:
        ms = jnp.mean(h * h, axis=-1, keepdims=True)
        h = h * lax.rsqrt(ms + NORM_EPS) * g_ref[...]
    o_ref[...] = h


def _outproj(yr, yh, x2, w_all, layer, g, tm, final):
    m, d = x2.shape
    wr_w = yr.shape[1]
    wh_w = yh.shape[1]
    assert wr_w == wh_w and w_all.shape[1:] == (wr_w + wh_w, d)
    once = dict(pipeline_mode=pl.Buffered(1))
    return pl.pallas_call(
        functools.partial(_outproj_kernel, final),
        grid=(m // tm,),
        in_specs=[
            pl.BlockSpec((tm, wr_w), lambda i: (i, 0)),
            pl.BlockSpec((tm, wh_w), lambda i: (i, 0)),
            pl.BlockSpec((tm, d), lambda i: (i, 0)),
            pl.BlockSpec((None, wr_w, d), lambda i: (layer, 0, 0), **once),
            pl.BlockSpec((None, wh_w, d), lambda i: (layer, 1, 0), **once),
            pl.BlockSpec((1, d), lambda i: (0, 0)),
        ],
        out_specs=pl.BlockSpec((tm, d), lambda i: (i, 0)),
        out_shape=jax.ShapeDtypeStruct((m, d), F32),
        scratch_shapes=[pltpu.VMEM((2, wr_w, d), BF16)],
        compiler_params=pltpu.CompilerParams(
            dimension_semantics=("arbitrary",), vmem_limit_bytes=VMEM_LIMIT),
        name="outproj",
    )(yr, yh, x2, w_all, w_all, g)


def kernel(x, norm_g, w_in, mu, w0, w2, a0, a2, k_k, k_a, r_k, lnx_w, lnx_b,
           hgrn_norm_g, lb_param, w_out, final_g):
    B, T, D = x.shape
    depth = w_in.shape[0]
    RW = w0.shape[1]
    HW = hgrn_norm_g.shape[1]
    n_r = 4 * RW + 2 * LORA
    tn = 768
    gap = -n_r % LANE
    n_p = w_in.shape[2] + gap
    h_start = (n_r // LANE) * LANE
    h_off = n_r + gap - h_start
    h_blk_w = n_p - h_start
    assert (4 * RW) % LORA_BLK == 0 and 2 * LORA <= LORA_BLK and w2.shape[1] == LORA <= LANE
    assert h_start % h_blk_w == 0 and h_off + 4 * HW <= h_blk_w

    row = lambda z: z.reshape(1, -1).astype(F32)
    zr = lambda rows: jnp.zeros((rows, RW), F32)
    ones_bd = (jnp.arange(QW)[:, None] // RWKV_HEAD == jnp.arange(QW)[None, :] // RWKV_HEAD).astype(BF16)

    w_in_t = jnp.swapaxes(w_in, 1, 2)
    h = x.reshape(B * T, D)
    for l in range(depth):
        mu_full = jnp.concatenate([mu[l].astype(F32), jnp.zeros((n_p - n_r,), F32)]).reshape(1, -1)
        w2p = jnp.concatenate([w2[l].astype(F32), zr(LANE - LORA)], axis=0).astype(BF16)
        a2p = jnp.concatenate([zr(LORA), a2[l].astype(F32), zr(LORA_BLK - 2 * LORA)],
                              axis=0).astype(BF16)

        p = _inproj(h, row(norm_g[l]), w_in_t, l, mu_full, n_r, T, tm=1024, tn=tn, groups=4)
        p3 = p.reshape(B, T, n_p)
        y_r, y_h = _mixers(p3, 0, (4 * RW) // LORA_BLK, h_blk_w, h_start // h_blk_w, h_off, l,
                           row(w0[l]), w2p, row(a0[l]), a2p, row(k_k[l]), row(k_a[l]),
                           row(r_k[l]), row(lnx_w[l]), row(lnx_b[l]), ones_bd,
                           lb_param.astype(F32), row(hgrn_norm_g[l]))
        h = _outproj(y_r.reshape(B * T, RW), y_h.reshape(B * T, HW), h,
                     w_out, l, row(final_g), tm=512, final=(l == depth - 1))
    return h.reshape(B, T, D)
```

```python
import functools
import math

import jax
import jax.numpy as jnp
from jax import lax
from jax.experimental import pallas as pl
from jax.experimental.pallas import tpu as pltpu

F32 = jnp.float32
BF16 = jnp.bfloat16

NORM_EPS = 1e-6
LNX_EPS = 64e-5
RWKV_HEAD = 64
HGRN_HEAD = 128
LORA = 96
LORA_BLK = 256
LANE = 128
CHUNK = 64
HGRN_SUB = 16
QUAD = 4
QW = QUAD * RWKV_HEAD
HGRN_AFTER_RWKV_STAGE = tuple(range(6, 18))
EXP_M05 = math.exp(-0.5)
VMEM_LIMIT = 48 * 1024 * 1024
VMEM_MARGIN = 6 * 1024 * 1024


def _mm(a, b):
    return jnp.dot(a.astype(BF16), b.astype(BF16), preferred_element_type=F32)


def _mm_nt(a, b):
    return lax.dot_general(a.astype(BF16), b.astype(BF16), (((1,), (1,)), ((), ())),
                           preferred_element_type=F32)


def _mm_tn(a, b):
    return lax.dot_general(a.astype(BF16), b.astype(BF16), (((0,), (0,)), ((), ())),
                           preferred_element_type=F32)


def _cumsum_rows(x, tril_bf16):
    hi = x.astype(BF16)
    lo = (x - hi.astype(F32)).astype(BF16)
    dot = functools.partial(jnp.dot, preferred_element_type=F32)
    return dot(tril_bf16, hi) + dot(tril_bf16, lo)


def _sigmoid(x):
    return 0.5 + 0.5 * jnp.tanh(0.5 * x)


def _inproj_kernel(gap_tile, gap_at, gap, seq_len, x_ref, g_ref, w_ref, mu_ref, o_ref, hn_ref, carry_ref):
    q = pl.program_id(0)
    j = pl.program_id(1)
    i = pl.program_id(2)
    tm, tn = o_ref.shape

    @pl.when((q == 0) & (j == 0) & (i == 0))
    def _():
        carry_ref[...] = jnp.zeros_like(carry_ref)

    @pl.when(j == 0)
    def _():
        x = x_ref[...]
        ms = jnp.mean(x * x, axis=-1, keepdims=True)
        hn_ref[i] = (x * lax.rsqrt(ms + NORM_EPS) * g_ref[...]).astype(BF16)

    w = w_ref[0]
    if gap:
        w_gap = jnp.concatenate([w[:gap_at], jnp.zeros((gap, w.shape[1]), w.dtype),
                                 w[gap_at:tn - gap]], axis=0)
        w = jnp.where(j == gap_tile, w_gap, w)
    acc = _mm_nt(hn_ref[i], w)

    first_row = (q * pl.num_programs(2) + i) * tm
    prev_last = jnp.where(first_row % seq_len == 0, 0.0, carry_ref[j])
    carry_ref[j] = acc[tm - 1:tm, :]
    is_row0 = lax.broadcasted_iota(jnp.int32, (tm, 1), 0) == 0
    prev = jnp.where(is_row0, prev_last, pltpu.roll(acc, 1, axis=0))
    o_ref[...] = acc + mu_ref[...] * (prev - acc)


def _inproj(x2, g, wt_all, layer, mu_full, gap_col, seq_len, tm, tn, groups):
    m, d = x2.shape
    n = mu_full.shape[1]
    gap = n - wt_all.shape[1]
    gap_tile, gap_at = divmod(gap_col, tn)
    assert n % tn == 0 and 0 <= gap and gap_at + gap <= tn and gap % 8 == 0 and gap_at % 8 == 0
    w_map = lambda q, j, i: (
        layer, pl.multiple_of(jnp.where(j <= gap_tile, j * tn, j * tn - gap), 8), 0)
    ni = m // (groups * tm)
    assert seq_len % tm == 0 and mu_full.shape == (1, n)
    vmem = ni * tm * d * 2 + 2 * 4 * (tm * d + tn * d + tm * tn) + VMEM_MARGIN
    x_map = lambda q, j, i: (q * ni + jnp.where(j == 0, i, ni - 1), 0)
    return pl.pallas_call(
        functools.partial(_inproj_kernel, gap_tile, gap_at, gap, seq_len),
        grid=(groups, n // tn, ni),
        in_specs=[
            pl.BlockSpec((tm, d), x_map),
            pl.BlockSpec((1, d), lambda q, j, i: (0, 0)),
            pl.BlockSpec((pl.Element(1), pl.Element(tn), pl.Element(d)), w_map),
            pl.BlockSpec((1, tn), lambda q, j, i: (0, j)),
        ],
        out_specs=pl.BlockSpec((tm, tn), lambda q, j, i: (q * ni + i, j)),
        out_shape=jax.ShapeDtypeStruct((m, n), F32),
        scratch_shapes=[pltpu.VMEM((ni, tm, d), BF16), pltpu.VMEM((n // tn, 1, tn), F32)],
        compiler_params=pltpu.CompilerParams(
            dimension_semantics=("arbitrary", "arbitrary", "arbitrary"),
            vmem_limit_bytes=vmem),
        name="inproj",
    )(x2, g, wt_all, mu_full)


def _rwkv_stages(pm_ref, pl_ref, w0_ref, w2_ref, a0_ref, a2_ref,
                 kk_ref, ka_ref, rk_ref, lw_ref, lb_ref, ones_ref,
                 o_ref, state):
    C = CHUNK
    NB = pm_ref.shape[0]
    W = w0_ref.shape[1]
    NQ = W // QW

    t_i = lax.broadcasted_iota(jnp.int32, (C, C), 0)
    s_i = lax.broadcasted_iota(jnp.int32, (C, C), 1)
    tril = (t_i >= s_i).astype(BF16)
    ones_bd = ones_ref[...]
    dot = functools.partial(jnp.dot, preferred_element_type=F32)

    lane = lax.broadcasted_iota(jnp.int32, (C, QW), 1)
    lane_s = lane % RWKV_HEAD
    lane_blk = lane // RWKV_HEAD
    trow = lax.broadcasted_iota(jnp.int32, (C, QW), 0)
    strict = lane_s < trow
    incl = lane_s <= trow
    eye_q = jnp.where(lane_s == trow, 1.0, 0.0).astype(F32)
    bd_mask = (lax.broadcasted_iota(jnp.int32, (QW, QW), 0) // RWKV_HEAD
               == lax.broadcasted_iota(jnp.int32, (QW, QW), 1) // RWKV_HEAD)

    def bd(x):
        xb = x.astype(BF16)
        zero = jnp.zeros_like(xb)
        return jnp.concatenate([jnp.where(lane_blk == h, xb, zero) for h in range(QUAD)], axis=0)

    def bd_t(x):
        z = jnp.zeros((LANE, LANE), x.dtype)
        t0 = jnp.transpose(x[:LANE, :LANE])
        t1 = jnp.transpose(x[LANE:, LANE:])
        return jnp.concatenate([jnp.concatenate([t0, z], axis=1),
                                jnp.concatenate([z, t1], axis=1)], axis=0)

    def segsum_all(xs):
        out = dot(jnp.concatenate([x.astype(BF16) for x in xs], axis=0), ones_bd)
        return [out[i * C:(i + 1) * C] for i in range(len(xs))]

    xl = jnp.concatenate([pl_ref[b] for b in range(NB)], axis=0)
    z = w0_ref[...] + _mm(jnp.tanh(xl[:, :LANE]), w2_ref[...])
    ld_all = -EXP_M05 * _sigmoid(z)
    a_all = _sigmoid(a0_ref[...] + _mm(xl, a2_ref[...]))
    units = []
    for b in range(NB):
        r = pm_ref[b, :, 0 * W:1 * W]
        k = pm_ref[b, :, 1 * W:2 * W]
        v = pm_ref[b, :, 2 * W:3 * W]
        gate = pm_ref[b, :, 3 * W:4 * W]
        ld = ld_all[b * C:(b + 1) * C]
        a = a_all[b * C:(b + 1) * C]
        G = _cumsum_rows(ld, tril)
        Gx = G - ld
        GC = G[C - 1:C, :]
        for q in range(NQ):
            sl = slice(q * QW, (q + 1) * QW)
            units.append(dict(b=b, ui=b * NQ + q, sl=sl, r=r[:, sl], k=k[:, sl], v=v[:, sl],
                              g=gate[:, sl], a=a[:, sl], G=G[:, sl], Gx=Gx[:, sl], GC=GC[:, sl]))
    yield

    for u in units:
        u["kk"] = u["k"] * kk_ref[:, u["sl"]]
    for u, n2 in zip(units, segsum_all([u["kk"] * u["kk"] for u in units])):
        u["n2"] = n2
    yield

    def unit_chain(u):
        sl = u["sl"]
        kkn = u["kk"] * lax.rsqrt(jnp.maximum(u["n2"], 1e-24))
        k2 = u["k"] * (1.0 + (u["a"] - 1.0) * ka_ref[:, sl])
        av = -kkn
        bv = kkn * u["a"]
        einv = jnp.exp(-u["G"])
        eC = jnp.exp(u["GC"]) * einv
        u["rkk"] = u["r"] * k2 * rk_ref[:, sl]
        BDB = bd_t(bd(bv * einv))
        BDK = bd_t(bd(k2 * einv))
        BKe = jnp.concatenate([bv * eC, k2 * eC], axis=0).astype(BF16)
        BDV = bd(u["v"])
        lhs = jnp.concatenate([av * jnp.exp(u["Gx"]), u["r"] * jnp.exp(u["G"])],
                              axis=0).astype(BF16)
        pc_col = jnp.transpose(jnp.broadcast_to(jnp.exp(u["GC"]), (LANE, QW)))
        PCc = jnp.concatenate([pc_col] * (QW // LANE), axis=1)
        yield
        SB = _mm(lhs, BDB)
        yield
        SK = _mm(lhs, BDK)
        yield
        N = jnp.where(strict, SB[:C], 0.0)
        Lrb = jnp.where(incl, SB[C:], 0.0)
        P = eye_q + N
        Np = _mm(N, bd(N))
        yield
        akl = jnp.concatenate([jnp.where(strict, SK[:C], 0.0),
                               jnp.where(incl, SK[C:], 0.0)], axis=0)
        both = _mm(akl, BDV)
        yield
        for _ in range(4):
            out = _mm(jnp.concatenate([P, Np], axis=0), bd(Np))
            P = P + out[:C]
            Np = out[C:]
            yield
        P = P + _mm(P, bd(Np))
        yield
        H = state[u["ui"]]
        zy = _mm(lhs, H)
        yield
        U = _mm(P, bd(zy[:C] + both[:C]))
        yield
        u["Y"] = zy[C:] + _mm(Lrb, bd(U)) + both[C:]
        yield
        upd = _mm_tn(BKe, jnp.concatenate([U, u["v"]], axis=0))
        state[u["ui"]] = H * PCc + jnp.where(bd_mask, upd, 0.0)

    chains = [unit_chain(u) for u in units]
    live = list(range(len(chains)))
    while live:
        for i in list(live):
            if next(chains[i], "done") == "done":
                live.remove(i)
        yield

    for u, s in zip(units, segsum_all([u["Y"] for u in units])):
        u["yc"] = u["Y"] - s * (1.0 / RWKV_HEAD)
    yield
    for u, s in zip(units, segsum_all([u["yc"] * u["yc"] for u in units])):
        u["var"] = s * (1.0 / RWKV_HEAD)
    yield
    for u, s in zip(units, segsum_all([u["rkk"] for u in units])):
        u["bonus"] = s
    yield
    for u in units:
        sl = u["sl"]
        yn = u["yc"] * lax.rsqrt(u["var"] + LNX_EPS) * lw_ref[:, sl] + lb_ref[:, sl]
        res = (yn + u["bonus"] * u["v"]) * (u["g"] * _sigmoid(u["g"]))
        o_ref[u["b"], :, sl] = res.astype(o_ref.dtype)


def _hgrn_stages(layer, off, ph_ref, lbp_ref, ng_ref, o_ref, state):
    C = CHUNK
    NB = ph_ref.shape[0]
    W = ng_ref.shape[1]
    D = HGRN_HEAD
    NH = W // D
    SUB = HGRN_SUB
    NBLK = C // SUB
    GRP = 2 * LANE // D

    lp = lbp_ref[...]
    e = jnp.exp(lp - jnp.max(lp, axis=0, keepdims=True))
    lb = jnp.sum(e[:layer + 1], axis=0, keepdims=True) / jnp.sum(e, axis=0, keepdims=True)

    t_i = lax.broadcasted_iota(jnp.int32, (C, C), 0)
    s_i = lax.broadcasted_iota(jnp.int32, (C, C), 1)
    causal = t_i >= s_i
    tril = causal.astype(BF16)

    units = []
    for b in range(NB):
        x = ph_ref[b]
        qv = x[:, off + 0 * W:off + 1 * W]
        fr = x[:, off + 1 * W:off + 2 * W]
        iv = x[:, off + 2 * W:off + 3 * W]
        gate = x[:, off + 3 * W:off + 4 * W]
        f = lb + (1.0 - lb) * _sigmoid(fr)
        kx = 1.0 - f
        G = _cumsum_rows(jnp.log(f), tril)
        GC = G[C - 1:C, :]
        PC = jnp.exp(GC)
        ivb = iv.astype(BF16)
        zero = jnp.zeros((SUB, W), BF16)
        qs, ks, qt_rows, k_rows = [], [], [], []
        for j in range(NBLK):
            lo, hi = j * SUB, (j + 1) * SUB
            qj = qv[lo:hi] * jnp.exp(G[lo:hi] - G[lo - 1:lo, :]) if j else qv[lo:hi] * jnp.exp(G[lo:hi])
            kj = kx[lo:hi] * jnp.exp(G[lo - 1:lo, :] - G[lo:hi]) if j else kx[lo:hi] * jnp.exp(-G[lo:hi])
            if j:
                step = jnp.exp(G[lo - 1:lo, :] - (G[lo - SUB - 1:lo - SUB, :] if j > 1 else 0.0))
                k_rows = [kr * step for kr in k_rows]
                qt_rows.append(qj * jnp.exp(G[lo - 1:lo, :]))
            else:
                qt_rows.append(qj)
            k_rows.append(kj)
            g0 = (j // GRP) * GRP
            qs.append(jnp.concatenate([zero] * (j - g0) + [qj.astype(BF16)] + [zero] * (g0 + GRP - 1 - j),
                                      axis=0))
            ks.append(jnp.concatenate([kr.astype(BF16) for kr in k_rows] + [zero] * (g0 + GRP - 1 - j),
                                      axis=0))
        qt = jnp.concatenate(qt_rows, axis=0).astype(BF16)
        to_end = jnp.exp(GC - G[C - SUB - 1:C - SUB, :])
        kd = jnp.concatenate([(kr * to_end).astype(BF16) for kr in k_rows], axis=0)
        for h in range(NH):
            sl = slice(h * D, (h + 1) * D)
            units.append(dict(b=b, h=h, sl=sl, qt=qt[:, sl], kd=kd[:, sl], iv=ivb[:, sl],
                              PC=PC[:, sl], g=gate[:, sl],
                              qcat=[jnp.concatenate([q[:, sl] for q in qs[g:g + GRP]], axis=1)
                                    for g in range(0, NBLK, GRP)],
                              kcat=[jnp.concatenate([k[:, sl] for k in ks[g:g + GRP]], axis=1)
                                    for g in range(0, NBLK, GRP)]))
        yield

    halves = (units[:len(units) // 2], units[len(units) // 2:])
    for part in halves:
        for u in part:
            u["A"] = [_mm_nt(q, k) for q, k in zip(u["qcat"], u["kcat"])]
        yield
    for part in halves:
        for u in part:
            u["S"] = state[u["b"] * NH + u["h"]]
            u["inter"] = _mm(u["qt"], jnp.transpose(u["S"]))
        yield
    for part in halves:
        for u in part:
            rows = []
            for g, A in enumerate(u["A"]):
                hi = (g + 1) * GRP * SUB
                mask = causal[hi - GRP * SUB:hi, :hi]
                rows.append(_mm(jnp.where(mask, A, 0.0), u["iv"][:hi]))
            u["o"] = jnp.concatenate(rows, axis=0) + u["inter"]
        yield
    for part in halves:
        for u in part:
            state[u["b"] * NH + u["h"]] = u["S"] * u["PC"] + _mm_tn(u["iv"], u["kd"])
        yield
    for part in halves:
        for u in part:
            o = u["o"]
            g = u["g"]
            ms = jnp.mean(o * o, axis=-1, keepdims=True)
            res = o * lax.rsqrt(ms + NORM_EPS) * ng_ref[:, u["sl"]] * (g * _sigmoid(g))
            o_ref[u["b"], :, u["sl"]] = res.astype(o_ref.dtype)
        yield


def _mixers_kernel(layer, off, pm_ref, pl_ref, ph_ref, w0_ref, w2_ref, a0_ref, a2_ref,
                   kk_ref, ka_ref, rk_ref, lw_ref, lb_ref, ones_ref, lbp_ref, ng_ref,
                   or_ref, oh_ref, r_state, h_state):
    @pl.when(pl.program_id(0) == 0)
    def _():
        for ref in (r_state, h_state):
            ref[...] = jnp.zeros_like(ref)

    rw = _rwkv_stages(pm_ref, pl_ref, w0_ref, w2_ref, a0_ref, a2_ref, kk_ref, ka_ref, rk_ref,
                      lw_ref, lb_ref, ones_ref, or_ref, r_state)
    hg = _hgrn_stages(layer, off, ph_ref, lbp_ref, ng_ref, oh_ref, h_state)
    for stage, _ in enumerate(rw, start=1):
        for _ in range(HGRN_AFTER_RWKV_STAGE.count(stage)):
            next(hg, None)
    for _ in hg:
        pass


def _mixers(p3, main_blk, lora_blk, h_blk_w, h_blk, h_off, layer,
            w0, w2p, a0, a2p, k_k, k_a, r_k, lnx_w, lnx_b, ones_bd, lb_param, hgrn_g):
    B, T, _ = p3.shape
    W = w0.shape[1]
    HW = hgrn_g.shape[1]
    C = CHUNK
    NU = B * (W // QW)
    L = lb_param.shape[0]
    full = lambda shape: pl.BlockSpec(shape, lambda c: (0,) * len(shape))
    return pl.pallas_call(
        functools.partial(_mixers_kernel, layer, h_off),
        grid=(T // C,),
        in_specs=[
            pl.BlockSpec((B, C, 4 * W), lambda c: (0, c, main_blk)),
            pl.BlockSpec((B, C, LORA_BLK), lambda c: (0, c, lora_blk)),
            pl.BlockSpec((B, C, h_blk_w), lambda c: (0, c, h_blk)),
            full((1, W)), full((LANE, W)), full((1, W)), full((LORA_BLK, W)),
            full((1, W)), full((1, W)), full((1, W)), full((1, W)), full((1, W)),
            full((QW, QW)), full((L, HW)), full((1, HW)),
        ],
        out_specs=[pl.BlockSpec((B, C, W), lambda c: (0, c, 0)),
                   pl.BlockSpec((B, C, HW), lambda c: (0, c, 0))],
        out_shape=[jax.ShapeDtypeStruct((B, T, W), BF16),
                   jax.ShapeDtypeStruct((B, T, HW), BF16)],
        scratch_shapes=[
            pltpu.VMEM((NU, QW, QW), F32),
            pltpu.VMEM((B * (HW // HGRN_HEAD), HGRN_HEAD, HGRN_HEAD), F32),
        ],
        compiler_params=pltpu.CompilerParams(
            dimension_semantics=("arbitrary",), vmem_limit_bytes=VMEM_LIMIT),
        name="mixers",
    )(p3, p3, p3, w0, w2p, a0, a2p, k_k, k_a, r_k, lnx_w, lnx_b, ones_bd, lb_param, hgrn_g)


def _outproj_kernel(final, yr_ref, yh_ref, x_ref, wr_ref, wh_ref, g_ref, o_ref, wb_ref):
    @pl.when(pl.program_id(0) == 0)
    def _():
        wb_ref[0] = wr_ref[...].astype(BF16)
        wb_ref[1] = wh_ref[...].astype(BF16)

    acc = jnp.dot(yr_ref[...], wb_ref[0], preferred_element_type=F32)
    acc += jnp.dot(yh_ref[...], wb_ref[1], preferred_element_type=F32)
    h = x_ref[...] + acc
    if final:
        ms = jnp.mean(h * h, axis=-1, keepdims=True)
        h = h * lax.rsqrt(ms + NORM_EPS) * g_ref[...]
    o_ref[...] = h


def _outproj(yr, yh, x2, w_all, layer, g, tm, final):
    m, d = x2.shape
    wr_w = yr.shape[1]
    wh_w = yh.shape[1]
    assert wr_w == wh_w and w_all.shape[1:] == (wr_w + wh_w, d)
    once = dict(pipeline_mode=pl.Buffered(1))
    return pl.pallas_call(
        functools.partial(_outproj_kernel, final),
        grid=(m // tm,),
        in_specs=[
            pl.BlockSpec((tm, wr_w), lambda i: (i, 0)),
            pl.BlockSpec((tm, wh_w), lambda i: (i, 0)),
            pl.BlockSpec((tm, d), lambda i: (i, 0)),
            pl.BlockSpec((None, wr_w, d), lambda i: (layer, 0, 0), **once),
            pl.BlockSpec((None, wh_w, d), lambda i: (layer, 1, 0), **once),
            pl.BlockSpec((1, d), lambda i: (0, 0)),
        ],
        out_specs=pl.BlockSpec((tm, d), lambda i: (i, 0)),
        out_shape=jax.ShapeDtypeStruct((m, d), F32),
        scratch_shapes=[pltpu.VMEM((2, wr_w, d), BF16)],
        compiler_params=pltpu.CompilerParams(
            dimension_semantics=("arbitrary",), vmem_limit_bytes=VMEM_LIMIT),
        name="outproj",
    )(yr, yh, x2, w_all, w_all, g)


def kernel(x, norm_g, w_in, mu, w0, w2, a0, a2, k_k, k_a, r_k, lnx_w, lnx_b,
           hgrn_norm_g, lb_param, w_out, final_g):
    B, T, D = x.shape
    depth = w_in.shape[0]
    RW = w0.shape[1]
    HW = hgrn_norm_g.shape[1]
    n_r = 4 * RW + 2 * LORA
    tn = 768
    gap = -n_r % LANE
    n_p = w_in.shape[2] + gap
    h_start = (n_r // LANE) * LANE
    h_off = n_r + gap - h_start
    h_blk_w = n_p - h_start
    assert (4 * RW) % LORA_BLK == 0 and 2 * LORA <= LORA_BLK and w2.shape[1] == LORA <= LANE
    assert h_start % h_blk_w == 0 and h_off + 4 * HW <= h_blk_w

    row = lambda z: z.reshape(1, -1).astype(F32)
    zr = lambda rows: jnp.zeros((rows, RW), F32)
    ones_bd = (jnp.arange(QW)[:, None] // RWKV_HEAD == jnp.arange(QW)[None, :] // RWKV_HEAD).astype(BF16)

    w_in_t = jnp.swapaxes(w_in, 1, 2)
    h = x.reshape(B * T, D)
    for l in range(depth):
        mu_full = jnp.concatenate([mu[l].astype(F32), jnp.zeros((n_p - n_r,), F32)]).reshape(1, -1)
        w2p = jnp.concatenate([w2[l].astype(F32), zr(LANE - LORA)], axis=0).astype(BF16)
        a2p = jnp.concatenate([zr(LORA), a2[l].astype(F32), zr(LORA_BLK - 2 * LORA)],
                              axis=0).astype(BF16)

        p = _inproj(h, row(norm_g[l]), w_in_t, l, mu_full, n_r, T, tm=1024, tn=tn, groups=4)
        p3 = p.reshape(B, T, n_p)
        y_r, y_h = _mixers(p3, 0, (4 * RW) // LORA_BLK, h_blk_w, h_start // h_blk_w, h_off, l,
                           row(w0[l]), w2p, row(a0[l]), a2p, row(k_k[l]), row(k_a[l]),
                           row(r_k[l]), row(lnx_w[l]), row(lnx_b[l]), ones_bd,
                           lb_param.astype(F32), row(hgrn_norm_g[l]))
        h = _outproj(y_r.reshape(B * T, RW), y_h.reshape(B * T, HW), h,
                     w_out, l, row(final_g), tm=512, final=(l == depth - 1))
    return h.reshape(B, T, D)
```

```python
import functools
import math

import jax
import jax.numpy as jnp
from jax import lax
from jax.experimental import pallas as pl
from jax.experimental.pallas import tpu as pltpu

F32 = jnp.float32
BF16 = jnp.bfloat16

NORM_EPS = 1e-6
LNX_EPS = 64e-5
RWKV_HEAD = 64
HGRN_HEAD = 128
LORA = 96
LORA_BLK = 256
LANE = 128
CHUNK = 64
HGRN_SUB = 16
QUAD = 4
QW = QUAD * RWKV_HEAD
HGRN_AFTER_RWKV_STAGE = tuple(range(6, 18))
EXP_M05 = math.exp(-0.5)
VMEM_LIMIT = 48 * 1024 * 1024
OUT_SUBTILES = 4
VMEM_MARGIN = 6 * 1024 * 1024


def _mm(a, b):
    return jnp.dot(a.astype(BF16), b.astype(BF16), preferred_element_type=F32)


def _mm_nt(a, b):
    return lax.dot_general(a.astype(BF16), b.astype(BF16), (((1,), (1,)), ((), ())),
                           preferred_element_type=F32)


def _mm_tn(a, b):
    return lax.dot_general(a.astype(BF16), b.astype(BF16), (((0,), (0,)), ((), ())),
                           preferred_element_type=F32)


def _cumsum_rows(x, tril_bf16):
    hi = x.astype(BF16)
    lo = (x - hi.astype(F32)).astype(BF16)
    dot = functools.partial(jnp.dot, preferred_element_type=F32)
    return dot(tril_bf16, hi) + dot(tril_bf16, lo)


def _sigmoid(x):
    return 0.5 + 0.5 * jnp.tanh(0.5 * x)


def _inproj_kernel(gap_tile, gap_at, gap, seq_len, x_ref, g_ref, w_ref, mu_ref, o_ref, hn_ref, carry_ref):
    q = pl.program_id(0)
    j = pl.program_id(1)
    i = pl.program_id(2)
    tm, tn = o_ref.shape

    @pl.when((q == 0) & (j == 0) & (i == 0))
    def _():
        carry_ref[...] = jnp.zeros_like(carry_ref)

    @pl.when(j == 0)
    def _():
        x = x_ref[...]
        ms = jnp.mean(x * x, axis=-1, keepdims=True)
        hn_ref[i] = (x * lax.rsqrt(ms + NORM_EPS) * g_ref[...]).astype(BF16)

    w = w_ref[0]
    if gap:
        w_gap = jnp.concatenate([w[:gap_at], jnp.zeros((gap, w.shape[1]), w.dtype),
                                 w[gap_at:tn - gap]], axis=0)
        w = jnp.where(j == gap_tile, w_gap, w)
    acc = _mm_nt(hn_ref[i], w)

    first_row = (q * pl.num_programs(2) + i) * tm
    prev_last = jnp.where(first_row % seq_len == 0, 0.0, carry_ref[j])
    carry_ref[j] = acc[tm - 1:tm, :]
    is_row0 = lax.broadcasted_iota(jnp.int32, (tm, 1), 0) == 0
    prev = jnp.where(is_row0, prev_last, pltpu.roll(acc, 1, axis=0))
    o_ref[...] = acc + mu_ref[...] * (prev - acc)


def _inproj(x2, g, wt_all, layer, mu_full, gap_col, seq_len, tm, tn, groups):
    m, d = x2.shape
    n = mu_full.shape[1]
    gap = n - wt_all.shape[1]
    gap_tile, gap_at = divmod(gap_col, tn)
    assert n % tn == 0 and 0 <= gap and gap_at + gap <= tn and gap % 8 == 0 and gap_at % 8 == 0
    w_map = lambda q, j, i: (
        layer, pl.multiple_of(jnp.where(j <= gap_tile, j * tn, j * tn - gap), 8), 0)
    ni = m // (groups * tm)
    assert seq_len % tm == 0 and mu_full.shape == (1, n)
    vmem = ni * tm * d * 2 + 2 * 4 * (tm * d + tn * d + tm * tn) + VMEM_MARGIN
    x_map = lambda q, j, i: (q * ni + jnp.where(j == 0, i, ni - 1), 0)
    return pl.pallas_call(
        functools.partial(_inproj_kernel, gap_tile, gap_at, gap, seq_len),
        grid=(groups, n // tn, ni),
        in_specs=[
            pl.BlockSpec((tm, d), x_map),
            pl.BlockSpec((1, d), lambda q, j, i: (0, 0)),
            pl.BlockSpec((pl.Element(1), pl.Element(tn), pl.Element(d)), w_map),
            pl.BlockSpec((1, tn), lambda q, j, i: (0, j)),
        ],
        out_specs=pl.BlockSpec((tm, tn), lambda q, j, i: (q * ni + i, j)),
        out_shape=jax.ShapeDtypeStruct((m, n), F32),
        scratch_shapes=[pltpu.VMEM((ni, tm, d), BF16), pltpu.VMEM((n // tn, 1, tn), F32)],
        compiler_params=pltpu.CompilerParams(
            dimension_semantics=("arbitrary", "arbitrary", "arbitrary"),
            vmem_limit_bytes=vmem),
        name="inproj",
    )(x2, g, wt_all, mu_full)


def _rwkv_stages(pm_ref, pl_ref, w0_ref, w2_ref, a0_ref, a2_ref,
                 kk_ref, ka_ref, rk_ref, lw_ref, lb_ref, ones_ref,
                 o_ref, state):
    C = CHUNK
    NB = pm_ref.shape[0]
    W = w0_ref.shape[1]
    NQ = W // QW

    t_i = lax.broadcasted_iota(jnp.int32, (C, C), 0)
    s_i = lax.broadcasted_iota(jnp.int32, (C, C), 1)
    tril = (t_i >= s_i).astype(BF16)
    ones_bd = ones_ref[...]
    dot = functools.partial(jnp.dot, preferred_element_type=F32)

    lane = lax.broadcasted_iota(jnp.int32, (C, QW), 1)
    lane_s = lane % RWKV_HEAD
    lane_blk = lane // RWKV_HEAD
    trow = lax.broadcasted_iota(jnp.int32, (C, QW), 0)
    strict = lane_s < trow
    incl = lane_s <= trow
    eye_q = jnp.where(lane_s == trow, 1.0, 0.0).astype(F32)
    bd_mask = (lax.broadcasted_iota(jnp.int32, (QW, QW), 0) // RWKV_HEAD
               == lax.broadcasted_iota(jnp.int32, (QW, QW), 1) // RWKV_HEAD)

    def bd(x):
        xb = x.astype(BF16)
        zero = jnp.zeros_like(xb)
        return jnp.concatenate([jnp.where(lane_blk == h, xb, zero) for h in range(QUAD)], axis=0)

    def bd_t(x):
        z = jnp.zeros((LANE, LANE), x.dtype)
        t0 = jnp.transpose(x[:LANE, :LANE])
        t1 = jnp.transpose(x[LANE:, LANE:])
        return jnp.concatenate([jnp.concatenate([t0, z], axis=1),
                                jnp.concatenate([z, t1], axis=1)], axis=0)

    def segsum_all(xs):
        out = dot(jnp.concatenate([x.astype(BF16) for x in xs], axis=0), ones_bd)
        return [out[i * C:(i + 1) * C] for i in range(len(xs))]

    xl = jnp.concatenate([pl_ref[b] for b in range(NB)], axis=0)
    z = w0_ref[...] + _mm(jnp.tanh(xl[:, :LANE]), w2_ref[...])
    ld_all = -EXP_M05 * _sigmoid(z)
    a_all = _sigmoid(a0_ref[...] + _mm(xl, a2_ref[...]))
    units = []
    for b in range(NB):
        r = pm_ref[b, :, 0 * W:1 * W]
        k = pm_ref[b, :, 1 * W:2 * W]
        v = pm_ref[b, :, 2 * W:3 * W]
        gate = pm_ref[b, :, 3 * W:4 * W]
        ld = ld_all[b * C:(b + 1) * C]
        a = a_all[b * C:(b + 1) * C]
        G = _cumsum_rows(ld, tril)
        Gx = G - ld
        GC = G[C - 1:C, :]
        for q in range(NQ):
            sl = slice(q * QW, (q + 1) * QW)
            units.append(dict(b=b, ui=b * NQ + q, sl=sl, r=r[:, sl], k=k[:, sl], v=v[:, sl],
                              g=gate[:, sl], a=a[:, sl], G=G[:, sl], Gx=Gx[:, sl], GC=GC[:, sl]))
    yield

    for u in units:
        u["kk"] = u["k"] * kk_ref[:, u["sl"]]
    for u, n2 in zip(units, segsum_all([u["kk"] * u["kk"] for u in units])):
        u["n2"] = n2
    yield

    def unit_chain(u):
        sl = u["sl"]
        kkn = u["kk"] * lax.rsqrt(jnp.maximum(u["n2"], 1e-24))
        k2 = u["k"] * (1.0 + (u["a"] - 1.0) * ka_ref[:, sl])
        av = -kkn
        bv = kkn * u["a"]
        einv = jnp.exp(-u["G"])
        eC = jnp.exp(u["GC"]) * einv
        u["rkk"] = u["r"] * k2 * rk_ref[:, sl]
        BDB = bd_t(bd(bv * einv))
        BDK = bd_t(bd(k2 * einv))
        BKe = jnp.concatenate([bv * eC, k2 * eC], axis=0).astype(BF16)
        BDV = bd(u["v"])
        lhs = jnp.concatenate([av * jnp.exp(u["Gx"]), u["r"] * jnp.exp(u["G"])],
                              axis=0).astype(BF16)
        pc_col = jnp.transpose(jnp.broadcast_to(jnp.exp(u["GC"]), (LANE, QW)))
        PCc = jnp.concatenate([pc_col] * (QW // LANE), axis=1)
        yield
        SB = _mm(lhs, BDB)
        yield
        SK = _mm(lhs, BDK)
        yield
        N = jnp.where(strict, SB[:C], 0.0)
        Lrb = jnp.where(incl, SB[C:], 0.0)
        P = eye_q + N
        Np = _mm(N, bd(N))
        yield
        akl = jnp.concatenate([jnp.where(strict, SK[:C], 0.0),
                               jnp.where(incl, SK[C:], 0.0)], axis=0)
        both = _mm(akl, BDV)
        yield
        for _ in range(4):
            out = _mm(jnp.concatenate([P, Np], axis=0), bd(Np))
            P = P + out[:C]
            Np = out[C:]
            yield
        P = P + _mm(P, bd(Np))
        yield
        H = state[u["ui"]]
        zy = _mm(lhs, H)
        yield
        U = _mm(P, bd(zy[:C] + both[:C]))
        yield
        u["Y"] = zy[C:] + _mm(Lrb, bd(U)) + both[C:]
        yield
        upd = _mm_tn(BKe, jnp.concatenate([U, u["v"]], axis=0))
        state[u["ui"]] = H * PCc + jnp.where(bd_mask, upd, 0.0)

    chains = [unit_chain(u) for u in units]
    live = list(range(len(chains)))
    while live:
        for i in list(live):
            if next(chains[i], "done") == "done":
                live.remove(i)
        yield

    for u, s in zip(units, segsum_all([u["Y"] for u in units])):
        u["yc"] = u["Y"] - s * (1.0 / RWKV_HEAD)
    yield
    for u, s in zip(units, segsum_all([u["yc"] * u["yc"] for u in units])):
        u["var"] = s * (1.0 / RWKV_HEAD)
    yield
    for u, s in zip(units, segsum_all([u["rkk"] for u in units])):
        u["bonus"] = s
    yield
    for u in units:
        sl = u["sl"]
        yn = u["yc"] * lax.rsqrt(u["var"] + LNX_EPS) * lw_ref[:, sl] + lb_ref[:, sl]
        res = (yn + u["bonus"] * u["v"]) * (u["g"] * _sigmoid(u["g"]))
        o_ref[u["b"], :, sl] = res.astype(o_ref.dtype)


def _hgrn_stages(layer, off, ph_ref, lbp_ref, ng_ref, o_ref, state):
    C = CHUNK
    NB = ph_ref.shape[0]
    W = ng_ref.shape[1]
    D = HGRN_HEAD
    NH = W // D
    SUB = HGRN_SUB
    NBLK = C // SUB
    GRP = 2 * LANE // D

    lp = lbp_ref[...]
    e = jnp.exp(lp - jnp.max(lp, axis=0, keepdims=True))
    lb = jnp.sum(e[:layer + 1], axis=0, keepdims=True) / jnp.sum(e, axis=0, keepdims=True)

    t_i = lax.broadcasted_iota(jnp.int32, (C, C), 0)
    s_i = lax.broadcasted_iota(jnp.int32, (C, C), 1)
    causal = t_i >= s_i
    tril = causal.astype(BF16)

    units = []
    for b in range(NB):
        x = ph_ref[b]
        qv = x[:, off + 0 * W:off + 1 * W]
        fr = x[:, off + 1 * W:off + 2 * W]
        iv = x[:, off + 2 * W:off + 3 * W]
        gate = x[:, off + 3 * W:off + 4 * W]
        f = lb + (1.0 - lb) * _sigmoid(fr)
        kx = 1.0 - f
        G = _cumsum_rows(jnp.log(f), tril)
        GC = G[C - 1:C, :]
        PC = jnp.exp(GC)
        ivb = iv.astype(BF16)
        zero = jnp.zeros((SUB, W), BF16)
        qs, ks, qt_rows, k_rows = [], [], [], []
        for j in range(NBLK):
            lo, hi = j * SUB, (j + 1) * SUB
            qj = qv[lo:hi] * jnp.exp(G[lo:hi] - G[lo - 1:lo, :]) if j else qv[lo:hi] * jnp.exp(G[lo:hi])
            kj = kx[lo:hi] * jnp.exp(G[lo - 1:lo, :] - G[lo:hi]) if j else kx[lo:hi] * jnp.exp(-G[lo:hi])
            if j:
                step = jnp.exp(G[lo - 1:lo, :] - (G[lo - SUB - 1:lo - SUB, :] if j > 1 else 0.0))
                k_rows = [kr * step for kr in k_rows]
                qt_rows.append(qj * jnp.exp(G[lo - 1:lo, :]))
            else:
                qt_rows.append(qj)
            k_rows.append(kj)
            g0 = (j // GRP) * GRP
            qs.append(jnp.concatenate([zero] * (j - g0) + [qj.astype(BF16)] + [zero] * (g0 + GRP - 1 - j),
                                      axis=0))
            ks.append(jnp.concatenate([kr.astype(BF16) for kr in k_rows] + [zero] * (g0 + GRP - 1 - j),
                                      axis=0))
        qt = jnp.concatenate(qt_rows, axis=0).astype(BF16)
        to_end = jnp.exp(GC - G[C - SUB - 1:C - SUB, :])
        kd = jnp.concatenate([(kr * to_end).astype(BF16) for kr in k_rows], axis=0)
        for h in range(NH):
            sl = slice(h * D, (h + 1) * D)
            units.append(dict(b=b, h=h, sl=sl, qt=qt[:, sl], kd=kd[:, sl], iv=ivb[:, sl],
                              PC=PC[:, sl], g=gate[:, sl],
                              qcat=[jnp.concatenate([q[:, sl] for q in qs[g:g + GRP]], axis=1)
                                    for g in range(0, NBLK, GRP)],
                              kcat=[jnp.concatenate([k[:, sl] for k in ks[g:g + GRP]], axis=1)
                                    for g in range(0, NBLK, GRP)]))
        yield

    halves = (units[:len(units) // 2], units[len(units) // 2:])
    for part in halves:
        for u in part:
            u["A"] = [_mm_nt(q, k) for q, k in zip(u["qcat"], u["kcat"])]
        yield
    for part in halves:
        for u in part:
            u["S"] = state[u["b"] * NH + u["h"]]
            u["inter"] = _mm(u["qt"], jnp.transpose(u["S"]))
        yield
    for part in halves:
        for u in part:
            rows = []
            for g, A in enumerate(u["A"]):
                hi = (g + 1) * GRP * SUB
                mask = causal[hi - GRP * SUB:hi, :hi]
                rows.append(_mm(jnp.where(mask, A, 0.0), u["iv"][:hi]))
            u["o"] = jnp.concatenate(rows, axis=0) + u["inter"]
        yield
    for part in halves:
        for u in part:
            state[u["b"] * NH + u["h"]] = u["S"] * u["PC"] + _mm_tn(u["iv"], u["kd"])
        yield
    for part in halves:
        for u in part:
            o = u["o"]
            g = u["g"]
            ms = jnp.mean(o * o, axis=-1, keepdims=True)
            res = o * lax.rsqrt(ms + NORM_EPS) * ng_ref[:, u["sl"]] * (g * _sigmoid(g))
            o_ref[u["b"], :, u["sl"]] = res.astype(o_ref.dtype)
        yield


def _mixers_kernel(layer, off, pm_ref, pl_ref, ph_ref, w0_ref, w2_ref, a0_ref, a2_ref,
                   kk_ref, ka_ref, rk_ref, lw_ref, lb_ref, ones_ref, lbp_ref, ng_ref,
                   or_ref, oh_ref, r_state, h_state):
    @pl.when(pl.program_id(0) == 0)
    def _():
        for ref in (r_state, h_state):
            ref[...] = jnp.zeros_like(ref)

    rw = _rwkv_stages(pm_ref, pl_ref, w0_ref, w2_ref, a0_ref, a2_ref, kk_ref, ka_ref, rk_ref,
                      lw_ref, lb_ref, ones_ref, or_ref, r_state)
    hg = _hgrn_stages(layer, off, ph_ref, lbp_ref, ng_ref, oh_ref, h_state)
    for stage, _ in enumerate(rw, start=1):
        for _ in range(HGRN_AFTER_RWKV_STAGE.count(stage)):
            next(hg, None)
    for _ in hg:
        pass


def _mixers(p3, main_blk, lora_blk, h_blk_w, h_blk, h_off, layer,
            w0, w2p, a0, a2p, k_k, k_a, r_k, lnx_w, lnx_b, ones_bd, lb_param, hgrn_g):
    B, T, _ = p3.shape
    W = w0.shape[1]
    HW = hgrn_g.shape[1]
    C = CHUNK
    NU = B * (W // QW)
    L = lb_param.shape[0]
    full = lambda shape: pl.BlockSpec(shape, lambda c: (0,) * len(shape))
    return pl.pallas_call(
        functools.partial(_mixers_kernel, layer, h_off),
        grid=(T // C,),
        in_specs=[
            pl.BlockSpec((B, C, 4 * W), lambda c: (0, c, main_blk)),
            pl.BlockSpec((B, C, LORA_BLK), lambda c: (0, c, lora_blk)),
            pl.BlockSpec((B, C, h_blk_w), lambda c: (0, c, h_blk)),
            full((1, W)), full((LANE, W)), full((1, W)), full((LORA_BLK, W)),
            full((1, W)), full((1, W)), full((1, W)), full((1, W)), full((1, W)),
            full((QW, QW)), full((L, HW)), full((1, HW)),
        ],
        out_specs=[pl.BlockSpec((B, C, W), lambda c: (0, c, 0)),
                   pl.BlockSpec((B, C, HW), lambda c: (0, c, 0))],
        out_shape=[jax.ShapeDtypeStruct((B, T, W), BF16),
                   jax.ShapeDtypeStruct((B, T, HW), BF16)],
        scratch_shapes=[
            pltpu.VMEM((NU, QW, QW), F32),
            pltpu.VMEM((B * (HW // HGRN_HEAD), HGRN_HEAD, HGRN_HEAD), F32),
        ],
        compiler_params=pltpu.CompilerParams(
            dimension_semantics=("arbitrary",), vmem_limit_bytes=VMEM_LIMIT),
        name="mixers",
    )(p3, p3, p3, w0, w2p, a0, a2p, k_k, k_a, r_k, lnx_w, lnx_b, ones_bd, lb_param, hgrn_g)


def _outproj_kernel(final, yr_ref, yh_ref, x_ref, wr_ref, wh_ref, g_ref, o_ref, wb_ref):
    @pl.when(pl.program_id(0) == 0)
    def _():
        wb_ref[0] = wr_ref[...].astype(BF16)
        wb_ref[1] = wh_ref[...].astype(BF16)

    tm = x_ref.shape[0]
    sub = tm // OUT_SUBTILES
    for s0 in range(0, tm, sub):
        rows = pl.ds(s0, sub)
        acc = jnp.dot(yr_ref[rows, :], wb_ref[0], preferred_element_type=F32)
        acc += jnp.dot(yh_ref[rows, :], wb_ref[1], preferred_element_type=F32)
        h = x_ref[rows, :] + acc
        if final:
            ms = jnp.mean(h * h, axis=-1, keepdims=True)
            h = h * lax.rsqrt(ms + NORM_EPS) * g_ref[...]
        o_ref[rows, :] = h


def _outproj(yr, yh, x2, w_all, layer, g, tm, final):
    m, d = x2.shape
    wr_w = yr.shape[1]
    wh_w = yh.shape[1]
    assert wr_w == wh_w and w_all.shape[1:] == (wr_w + wh_w, d)
    once = dict(pipeline_mode=pl.Buffered(1))
    return pl.pallas_call(
        functools.partial(_outproj_kernel, final),
        grid=(m // tm,),
        in_specs=[
            pl.BlockSpec((tm, wr_w), lambda i: (i, 0)),
            pl.BlockSpec((tm, wh_w), lambda i: (i, 0)),
            pl.BlockSpec((tm, d), lambda i: (i, 0)),
            pl.BlockSpec((None, wr_w, d), lambda i: (layer, 0, 0), **once),
            pl.BlockSpec((None, wh_w, d), lambda i: (layer, 1, 0), **once),
            pl.BlockSpec((1, d), lambda i: (0, 0)),
        ],
        out_specs=pl.BlockSpec((tm, d), lambda i: (i, 0)),
        out_shape=jax.ShapeDtypeStruct((m, d), F32),
        scratch_shapes=[pltpu.VMEM((2, wr_w, d), BF16)],
        compiler_params=pltpu.CompilerParams(
            dimension_semantics=("arbitrary",), vmem_limit_bytes=VMEM_LIMIT),
        name="outproj",
    )(yr, yh, x2, w_all, w_all, g)


def kernel(x, norm_g, w_in, mu, w0, w2, a0, a2, k_k, k_a, r_k, lnx_w, lnx_b,
           hgrn_norm_g, lb_param, w_out, final_g):
    B, T, D = x.shape
    depth = w_in.shape[0]
    RW = w0.shape[1]
    HW = hgrn_norm_g.shape[1]
    n_r = 4 * RW + 2 * LORA
    tn = 768
    gap = -n_r % LANE
    n_p = w_in.shape[2] + gap
    h_start = (n_r // LANE) * LANE
    h_off = n_r + gap - h_start
    h_blk_w = n_p - h_start
    assert (4 * RW) % LORA_BLK == 0 and 2 * LORA <= LORA_BLK and w2.shape[1] == LORA <= LANE
    assert h_start % h_blk_w == 0 and h_off + 4 * HW <= h_blk_w

    row = lambda z: z.reshape(1, -1).astype(F32)
    zr = lambda rows: jnp.zeros((rows, RW), F32)
    ones_bd = (jnp.arange(QW)[:, None] // RWKV_HEAD == jnp.arange(QW)[None, :] // RWKV_HEAD).astype(BF16)

    w_in_t = jnp.swapaxes(w_in, 1, 2)
    h = x.reshape(B * T, D)
    for l in range(depth):
        mu_full = jnp.concatenate([mu[l].astype(F32), jnp.zeros((n_p - n_r,), F32)]).reshape(1, -1)
        w2p = jnp.concatenate([w2[l].astype(F32), zr(LANE - LORA)], axis=0).astype(BF16)
        a2p = jnp.concatenate([zr(LORA), a2[l].astype(F32), zr(LORA_BLK - 2 * LORA)],
                              axis=0).astype(BF16)

        p = _inproj(h, row(norm_g[l]), w_in_t, l, mu_full, n_r, T, tm=1024, tn=tn, groups=4)
        p3 = p.reshape(B, T, n_p)
        y_r, y_h = _mixers(p3, 0, (4 * RW) // LORA_BLK, h_blk_w, h_start // h_blk_w, h_off, l,
                           row(w0[l]), w2p, row(a0[l]), a2p, row(k_k[l]), row(k_a[l]),
                           row(r_k[l]), row(lnx_w[l]), row(lnx_b[l]), ones_bd,
                           lb_param.astype(F32), row(hgrn_norm_g[l]))
        h = _outproj(y_r.reshape(B * T, RW), y_h.reshape(B * T, HW), h,
                     w_out, l, row(final_g), tm=512, final=(l == depth - 1))
    return h.reshape(B, T, D)
```
